```python
import functools
import jax
import jax.numpy as jnp
from jax import lax
import numpy as np

D_MODEL = 2048
BATCH = 8
SEQ = 2048
DEPTH = 1

RWKV_HEAD = 64
RWKV_HEADS = D_MODEL // RWKV_HEAD
RWKV_DIM = RWKV_HEADS * RWKV_HEAD
D_DECAY_LORA = max(32, int(round(1.8 * RWKV_DIM ** 0.5 / 32)) * 32)
D_AAA_LORA = max(32, int(round(1.8 * RWKV_DIM ** 0.5 / 32)) * 32)
D_GATE_LORA = max(32, int(round(0.6 * RWKV_DIM ** 0.8 / 32)) * 32)
RWKV_GN_EPS = 64e-5
SHIFT_W = 3 * RWKV_DIM + D_DECAY_LORA + D_AAA_LORA + D_GATE_LORA

MOBA_HEAD = 128
MOBA_HEADS = D_MODEL // MOBA_HEAD
MOBA_DIM = MOBA_HEADS * MOBA_HEAD
MOBA_BLOCK = 256
MOBA_TOPK = 3
Q_CHUNK = 128
ROPE_THETA = 10000.0

N_IN = SHIFT_W + 3 * MOBA_DIM + 2 * D_MODEL

N_EXPERTS = 256
N_GROUPS = 8
EXPERTS_PER_GROUP = N_EXPERTS // N_GROUPS
TOPK_GROUPS = 4
TOP_K = 8
EXPERT_FF = 512
SHARED_FF = 512
ROUTED_SCALE = 2.5
EXPERT_ROWS = 128

LN_EPS = 1e-5
ALPHA = (2 * DEPTH) ** 0.25
BETA = (8 * DEPTH) ** -0.25

kernel_name = "rwkv7_moba_gated_deepnorm_moe"


def _layer_norm(x, g, b):
    xf = x.astype(jnp.float32)
    mu = xf.mean(-1, keepdims=True)
    var = jnp.square(xf - mu).mean(-1, keepdims=True)
    return (xf - mu) * lax.rsqrt(var + LN_EPS) * g + b


def _wkv7_scan(r, decay, k, v, a, b):
    B, S, H, N = r.shape

    def step(state, inp):
        r_t, w_t, k_t, v_t, a_t, b_t = inp
        sa = jnp.einsum('bhij,bhj->bhi', state, a_t)
        state = (state * w_t[:, :, None, :] + sa[..., None] * b_t[:, :, None, :]
                 + v_t[..., None] * k_t[:, :, None, :])
        y_t = jnp.einsum('bhij,bhj->bhi', state, r_t)
        return state, y_t

    xs = tuple(t.transpose(1, 0, 2, 3) for t in (r, decay, k, v, a, b))
    s0 = jnp.zeros((B, H, N, N), jnp.float32)
    _, y = lax.scan(step, s0, xs)
    return y.transpose(1, 0, 2, 3)


def _rwkv7_time_mix(z, w_lora_up, w0, a_lora_up, a0, g_lora_up, k_k, k_a, r_k, ln_x_w, ln_x_b):
    B, S, _ = z.shape
    H, N, C = RWKV_HEADS, RWKV_HEAD, RWKV_DIM
    cuts = [C, 2 * C, 3 * C, 3 * C + D_DECAY_LORA, 3 * C + D_DECAY_LORA + D_AAA_LORA]
    r, k, v, zw, za, zg = jnp.split(z, cuts, axis=-1)
    w = -jax.nn.softplus(-(w0 + jnp.tanh(zw) @ w_lora_up)) - 0.5
    a = jax.nn.sigmoid(a0 + za @ a_lora_up)
    g = jax.nn.sigmoid(zg) @ g_lora_up
    heads = lambda t: t.reshape(B, S, H, N)
    kk = heads(k * k_k)
    kk = kk / jnp.maximum(jnp.linalg.norm(kk, axis=-1, keepdims=True), 1e-12)
    k = k * (1.0 + (a - 1.0) * k_a)
    r_h, k_h, v_h, a_h = heads(r), heads(k), heads(v), heads(a)
    decay = jnp.exp(-jnp.exp(heads(w)))
    y = _wkv7_scan(r_h, decay, k_h, v_h, -kk, kk * a_h)
    mu = y.mean(-1, keepdims=True)
    var = jnp.square(y - mu).mean(-1, keepdims=True)
    y = ((y - mu) * lax.rsqrt(var + RWKV_GN_EPS)).reshape(B, S, C) * ln_x_w + ln_x_b
    bonus = ((r_h * k_h * r_k).sum(-1, keepdims=True) * v_h).reshape(B, S, C)
    return (y + bonus) * g


def _rope_tables(seq, dim):
    inv = 1.0 / (ROPE_THETA ** (jnp.arange(0, dim, 2, dtype=jnp.float32) / dim))
    ang = jnp.arange(seq, dtype=jnp.float32)[:, None] * inv[None, :]
    ang = jnp.concatenate([ang, ang], axis=-1)
    return jnp.cos(ang), jnp.sin(ang)


def _apply_rope(t, cos, sin):
    t1, t2 = jnp.split(t, 2, axis=-1)
    rot = jnp.concatenate([-t2, t1], axis=-1)
    return t * cos[None, :, None, :] + rot * sin[None, :, None, :]


def _moba_past_one(q, k_blocks, v_blocks, sel, scale):
    sp, dh = q.shape
    nb = k_blocks.shape[0]
    n_sel = sel.shape[1]
    P = sp * n_sel
    pair_blk = sel.reshape(P)
    order = jnp.argsort(pair_blk).astype(jnp.int32)
    sorted_blk = pair_blk[order]
    counts = jax.ops.segment_sum(jnp.ones((P,), jnp.int32), pair_blk, num_segments=nb + 1)[:nb]
    padded = (counts + Q_CHUNK - 1) // Q_CHUNK * Q_CHUNK
    pad_end = jnp.cumsum(padded)
    pad_start = pad_end - padded
    grp_start = jnp.cumsum(counts) - counts
    safe_blk = jnp.minimum(sorted_blk, nb - 1)
    L = (-(-P // Q_CHUNK) + nb) * Q_CHUNK
    dest = pad_start[safe_blk] + jnp.arange(P, dtype=jnp.int32) - grp_start[safe_blk]
    dest = jnp.where(sorted_blk < nb, dest, L)
    row_q = jnp.full((L,), sp, jnp.int32).at[dest].set(order // n_sel, mode='drop')
    row_slot = jnp.full((L,), P, jnp.int32).at[dest].set(order, mode='drop')
    n_chunks = L // Q_CHUNK
    chunk_blk = jnp.minimum(
        jnp.searchsorted(pad_end, jnp.arange(n_chunks, dtype=jnp.int32) * Q_CHUNK, side='right'),
        nb - 1)
    q_pad = jnp.concatenate([q, jnp.zeros((1, dh), q.dtype)], axis=0)
    q_rows = q_pad[row_q].reshape(n_chunks, Q_CHUNK, dh)
    kc = k_blocks[chunk_blk]
    vc = v_blocks[chunk_blk]
    logits = jnp.einsum('cqd,ckd->cqk', q_rows, kc) * scale
    m = logits.max(-1)
    p = jnp.exp(logits - m[..., None])
    l = p.sum(-1)
    o = jnp.einsum('cqk,ckd->cqd', p, vc)
    m_slot = jnp.full((P + 1,), -jnp.inf, jnp.float32).at[row_slot].set(m.reshape(L))[:P]
    l_slot = jnp.zeros((P + 1,), jnp.float32).at[row_slot].set(l.reshape(L))[:P]
    o_slot = jnp.zeros((P + 1, dh), jnp.float32).at[row_slot].set(o.reshape(L, dh))[:P]
    return m_slot.reshape(sp, n_sel), l_slot.reshape(sp, n_sel), o_slot.reshape(sp, n_sel, dh)


def _moba_attention(q, k, v):
    B, S, H, Dh = q.shape
    nb = -(-S // MOBA_BLOCK)
    sp = nb * MOBA_BLOCK
    scale = Dh ** -0.5

    def prep(t):
        t = jnp.pad(t, ((0, 0), (0, sp - S), (0, 0), (0, 0)))
        return t.transpose(0, 2, 1, 3)

    q, k, v = prep(q), prep(k), prep(v)
    qb = q.reshape(B, H, nb, MOBA_BLOCK, Dh)
    kb = k.reshape(B, H, nb, MOBA_BLOCK, Dh)
    vb = v.reshape(B, H, nb, MOBA_BLOCK, Dh)

    k_mean = kb.mean(axis=3)
    gate = jnp.einsum('bhsd,bhnd->bhsn', q, k_mean)
    q_blk = jnp.arange(sp) // MOBA_BLOCK
    is_past = jnp.arange(nb)[None, :] < q_blk[:, None]
    gate = jnp.where(is_past, gate, -jnp.inf)
    n_sel = min(MOBA_TOPK, nb)
    _, sel = lax.top_k(gate, n_sel)
    n_valid = jnp.minimum(q_blk, MOBA_TOPK)
    sel = jnp.where(jnp.arange(n_sel)[None, :] < n_valid[:, None], sel, nb).astype(jnp.int32)

    logit_own = jnp.einsum('bhnqd,bhnkd->bhnqk', qb, kb) * scale
    causal = jnp.tril(jnp.ones((MOBA_BLOCK, MOBA_BLOCK), dtype=bool))
    logit_own = jnp.where(causal, logit_own, -jnp.inf)
    m_own = logit_own.max(-1)
    p_own = jnp.exp(logit_own - m_own[..., None])
    l_own = p_own.sum(-1).reshape(B, H, sp)
    o_own = jnp.einsum('bhnqk,bhnkd->bhnqd', p_own, vb).reshape(B, H, sp, Dh)
    m_own = m_own.reshape(B, H, sp)

    past_fn = jax.vmap(functools.partial(_moba_past_one, scale=scale))
    m_p, l_p, o_p = lax.map(
        lambda args: past_fn(*args),
        (q.transpose(1, 0, 2, 3), kb.transpose(1, 0, 2, 3, 4),
         vb.transpose(1, 0, 2, 3, 4), sel.transpose(1, 0, 2, 3)))
    m_p = m_p.transpose(1, 0, 2, 3)
    l_p = l_p.transpose(1, 0, 2, 3)
    o_p = o_p.transpose(1, 0, 2, 3, 4)

    m_all = jnp.maximum(m_own, m_p.max(-1))
    w_own = jnp.exp(m_own - m_all)
    w_p = jnp.exp(m_p - m_all[..., None])
    num = o_own * w_own[..., None] + jnp.einsum('bhsk,bhskd->bhsd', w_p, o_p)
    den = l_own * w_own + (l_p * w_p).sum(-1)
    out = num / den[..., None]
    return out[:, :, :S].transpose(0, 2, 1, 3)


def _moe(h, w_router, router_bias, w_gate_e, w_up_e, w_down_e, w_gate_s, w_up_s, w_down_s):
    B, S, D = h.shape
    T = B * S
    xt = h.reshape(T, D)
    scores = jax.nn.sigmoid((xt @ w_router).astype(jnp.float32))
    biased = scores + router_bias
    grp = biased.reshape(T, N_GROUPS, EXPERTS_PER_GROUP)
    grp_score = lax.top_k(grp, 2)[0].sum(-1)
    _, grp_idx = lax.top_k(grp_score, TOPK_GROUPS)
    grp_mask = jnp.any(grp_idx[..., None] == jnp.arange(N_GROUPS), axis=-2)
    masked = jnp.where(jnp.repeat(grp_mask, EXPERTS_PER_GROUP, axis=-1), biased, -jnp.inf)
    _, top_idx = lax.top_k(masked, TOP_K)
    top_w = jnp.take_along_axis(scores, top_idx, axis=1)
    top_w = top_w / top_w.sum(-1, keepdims=True) * ROUTED_SCALE

    A = T * TOP_K
    e_flat = top_idx.reshape(A).astype(jnp.int32)
    order = jnp.argsort(e_flat).astype(jnp.int32)
    sorted_e = e_flat[order]
    counts = jax.ops.segment_sum(jnp.ones((A,), jnp.int32), e_flat, num_segments=N_EXPERTS)
    padded = (counts + EXPERT_ROWS - 1) // EXPERT_ROWS * EXPERT_ROWS
    pad_end = jnp.cumsum(padded)
    pad_start = pad_end - padded
    grp_start = jnp.cumsum(counts) - counts
    dest = pad_start[sorted_e] + jnp.arange(A, dtype=jnp.int32) - grp_start[sorted_e]
    L = (-(-A // EXPERT_ROWS) + N_EXPERTS) * EXPERT_ROWS
    row_tok = jnp.full((L,), T, jnp.int32).at[dest].set(order // TOP_K)
    row_w = jnp.zeros((L,), jnp.float32).at[dest].set(top_w.reshape(A)[order])
    n_blk = L // EXPERT_ROWS
    blk_e = jnp.minimum(
        jnp.searchsorted(pad_end, jnp.arange(n_blk, dtype=jnp.int32) * EXPERT_ROWS, side='right'),
        N_EXPERTS - 1)
    x_pad = jnp.concatenate([xt, jnp.zeros((1, D), xt.dtype)], axis=0)

    def body(acc, inp):
        toks, ws, e = inp
        xb = x_pad[toks]
        hb = jax.nn.silu(xb @ w_gate_e[e]) * (xb @ w_up_e[e])
        yb = (hb @ w_down_e[e]).astype(jnp.float32) * ws[:, None]
        return acc.at[toks].add(yb), None

    acc0 = jnp.zeros((T + 1, D), jnp.float32)
    acc, _ = lax.scan(body, acc0, (row_tok.reshape(n_blk, EXPERT_ROWS),
                                   row_w.reshape(n_blk, EXPERT_ROWS), blk_e))
    shared = ((jax.nn.silu(xt @ w_gate_s) * (xt @ w_up_s)) @ w_down_s).astype(jnp.float32)
    return (acc[:T] + shared).reshape(B, S, D).astype(h.dtype)


def _hybrid_layer(h, w_in, shift_mu, w_lora_up, w0, a_lora_up, a0, g_lora_up, k_k, k_a, r_k,
                  ln_x_w, ln_x_b, w_o_rwkv, w_o_moba, w_out, ln1_g, ln1_b, w_router, router_bias,
                  w_gate_e, w_up_e, w_down_e, w_gate_s, w_up_s, w_down_s, ln2_g, ln2_b):
    B, S, _ = h.shape
    dt = h.dtype
    u = h @ w_in
    cuts = [SHIFT_W, SHIFT_W + MOBA_DIM, SHIFT_W + 2 * MOBA_DIM, SHIFT_W + 3 * MOBA_DIM,
            SHIFT_W + 3 * MOBA_DIM + D_MODEL]
    u_rwkv, u_q, u_k, u_v, u_ga, u_gb = jnp.split(u, cuts, axis=-1)

    z = u_rwkv.astype(jnp.float32)
    z_prev = jnp.pad(z, ((0, 0), (1, 0), (0, 0)))[:, :-1]
    z = z + (z_prev - z) * shift_mu
    o_rwkv = _rwkv7_time_mix(z, w_lora_up, w0, a_lora_up, a0, g_lora_up, k_k, k_a, r_k,
                             ln_x_w, ln_x_b).astype(dt)

    cos, sin = _rope_tables(S, MOBA_HEAD)
    heads = lambda t: t.astype(jnp.float32).reshape(B, S, MOBA_HEADS, MOBA_HEAD)
    q = _apply_rope(heads(u_q), cos, sin)
    k = _apply_rope(heads(u_k), cos, sin)
    o_moba = _moba_attention(q, k, heads(u_v)).reshape(B, S, MOBA_DIM).astype(dt)

    mixed = jax.nn.sigmoid(u_ga) * (o_rwkv @ w_o_rwkv) + jax.nn.sigmoid(u_gb) * (o_moba @ w_o_moba)
    h = _layer_norm(ALPHA * h + mixed @ w_out, ln1_g, ln1_b).astype(dt)

    moe_out = _moe(h, w_router, router_bias, w_gate_e, w_up_e, w_down_e, w_gate_s, w_up_s, w_down_s)
    h = _layer_norm(ALPHA * h + moe_out, ln2_g, ln2_b).astype(dt)
    return h


def setup_inputs(seed: int = 0) -> dict:
    key = jax.random.key(seed)
    ks = jax.random.split(key, 32)
    f32 = jnp.float32
    L_ = DEPTH

    def nrm(k, shape, scale):
        return jax.random.normal(k, (L_,) + shape, f32) * scale

    x = jax.random.normal(ks[0], (BATCH, SEQ, D_MODEL), f32)
    w_in = nrm(ks[1], (D_MODEL, N_IN), D_MODEL ** -0.5)
    shift_mu = jax.random.uniform(ks[2], (L_, SHIFT_W), f32)
    w_lora_up = nrm(ks[3], (D_DECAY_LORA, RWKV_DIM), 0.1 * D_DECAY_LORA ** -0.5)
    w0 = jax.random.uniform(ks[4], (L_, RWKV_DIM), f32, -6.5, -1.5)
    a_lora_up = nrm(ks[5], (D_AAA_LORA, RWKV_DIM), 0.1 * D_AAA_LORA ** -0.5)
    a0 = nrm(ks[6], (RWKV_DIM,), 0.1)
    g_lora_up = nrm(ks[7], (D_GATE_LORA, RWKV_DIM), D_GATE_LORA ** -0.5)
    k_k = 0.85 + nrm(ks[8], (RWKV_DIM,), 0.02)
    k_a = 1.0 + nrm(ks[9], (RWKV_DIM,), 0.02)
    r_k = nrm(ks[10], (RWKV_HEADS, RWKV_HEAD), 0.1)
    ln_x_w = 1.0 + nrm(ks[11], (RWKV_DIM,), 0.02)
    ln_x_b = nrm(ks[12], (RWKV_DIM,), 0.02)
    w_o_rwkv = nrm(ks[13], (RWKV_DIM, D_MODEL), RWKV_DIM ** -0.5)
    w_o_moba = nrm(ks[14], (MOBA_DIM, D_MODEL), MOBA_DIM ** -0.5)
    w_out = nrm(ks[15], (D_MODEL, D_MODEL), BETA * D_MODEL ** -0.5)
    ln1_g = 1.0 + nrm(ks[16], (D_MODEL,), 0.02)
    ln1_b = nrm(ks[17], (D_MODEL,), 0.02)
    w_router = nrm(ks[18], (D_MODEL, N_EXPERTS), D_MODEL ** -0.5)
    router_bias = nrm(ks[19], (N_EXPERTS,), 0.01)
    w_gate_e = nrm(ks[20], (N_EXPERTS, D_MODEL, EXPERT_FF), D_MODEL ** -0.5)
    w_up_e = nrm(ks[21], (N_EXPERTS, D_MODEL, EXPERT_FF), D_MODEL ** -0.5)
    w_down_e = nrm(ks[22], (N_EXPERTS, EXPERT_FF, D_MODEL), BETA * EXPERT_FF ** -0.5)
    w_gate_s = nrm(ks[23], (D_MODEL, SHARED_FF), D_MODEL ** -0.5)
    w_up_s = nrm(ks[24], (D_MODEL, SHARED_FF), D_MODEL ** -0.5)
    w_down_s = nrm(ks[25], (SHARED_FF, D_MODEL), BETA * SHARED_FF ** -0.5)
    ln2_g = 1.0 + nrm(ks[26], (D_MODEL,), 0.02)
    ln2_b = nrm(ks[27], (D_MODEL,), 0.02)
    return {'x': x, 'w_in': w_in, 'shift_mu': shift_mu, 'w_lora_up': w_lora_up, 'w0': w0,
            'a_lora_up': a_lora_up, 'a0': a0, 'g_lora_up': g_lora_up, 'k_k': k_k, 'k_a': k_a,
            'r_k': r_k, 'ln_x_w': ln_x_w, 'ln_x_b': ln_x_b, 'w_o_rwkv': w_o_rwkv,
            'w_o_moba': w_o_moba, 'w_out': w_out, 'ln1_g': ln1_g, 'ln1_b': ln1_b,
            'w_router': w_router, 'router_bias': router_bias, 'w_gate_e': w_gate_e,
            'w_up_e': w_up_e, 'w_down_e': w_down_e, 'w_gate_s': w_gate_s, 'w_up_s': w_up_s,
            'w_down_s': w_down_s, 'ln2_g': ln2_g, 'ln2_b': ln2_b}


def reference(x, w_in, shift_mu, w_lora_up, w0, a_lora_up, a0, g_lora_up, k_k, k_a, r_k,
              ln_x_w, ln_x_b, w_o_rwkv, w_o_moba, w_out, ln1_g, ln1_b, w_router, router_bias,
              w_gate_e, w_up_e, w_down_e, w_gate_s, w_up_s, w_down_s, ln2_g, ln2_b):
    layer_params = (w_in, shift_mu, w_lora_up, w0, a_lora_up, a0, g_lora_up, k_k, k_a, r_k,
                    ln_x_w, ln_x_b, w_o_rwkv, w_o_moba, w_out, ln1_g, ln1_b, w_router,
                    router_bias, w_gate_e, w_up_e, w_down_e, w_gate_s, w_up_s, w_down_s,
                    ln2_g, ln2_b)
    h = x
    for layer in range(DEPTH):
        h = _hybrid_layer(h, *[p[layer] for p in layer_params])
    return h
```

```python
import functools

import jax
import jax.numpy as jnp
from jax import lax
from jax.experimental import pallas as pl
from jax.experimental.pallas import tpu as pltpu

F32 = jnp.float32
BF16 = jnp.bfloat16

LANES = 128
WKV_HEAD = 64
WKV_CHUNK = 64
WKV_PAIRS = 4


def _dot(a, b):
    return jnp.dot(a.astype(BF16), b.astype(BF16), preferred_element_type=F32)


def _dot_nt(a, b):
    return lax.dot_general(a.astype(BF16), b.astype(BF16), (((1,), (1,)), ((), ())),
                           preferred_element_type=F32)


def _dot_tn(a, b):
    return lax.dot_general(a.astype(BF16), b.astype(BF16), (((0,), (0,)), ((), ())),
                           preferred_element_type=F32)


def _split3(x):
    hi = x.astype(BF16)
    r1 = x - hi.astype(F32)
    mid = r1.astype(BF16)
    lo = (r1 - mid.astype(F32)).astype(BF16)
    return hi, mid, lo


def _dot_exact_lhs(a_bf16, x):
    hi, mid, lo = _split3(x)
    d = lambda t: jnp.dot(a_bf16, t, preferred_element_type=F32)
    return d(hi) + d(mid) + d(lo)


def _dot_f32(a, b):
    a_hi = a.astype(BF16)
    a_lo = (a - a_hi.astype(F32)).astype(BF16)
    b_hi = b.astype(BF16)
    b_lo = (b - b_hi.astype(F32)).astype(BF16)
    d = lambda p, q: jnp.dot(p, q, preferred_element_type=F32)
    return d(a_hi, b_hi) + d(a_hi, b_lo) + d(a_lo, b_hi)


def _wkv_pair(r, ld, k, v, a, b, m_state):
    c = WKV_CHUNK
    c2 = 2 * c
    lane = lax.broadcasted_iota(jnp.int32, (c, LANES), 1)
    head0 = lane < WKV_HEAD

    def stack(x):
        return jnp.concatenate([jnp.where(head0, x, 0.0), jnp.where(head0, 0.0, x)], axis=0)

    row = lax.broadcasted_iota(jnp.int32, (c, c), 0)
    col = lax.broadcasted_iota(jnp.int32, (c, c), 1)
    tri = jnp.where(row >= col, 1.0, 0.0).astype(BF16)
    cw = _dot_exact_lhs(tri, ld)
    cw_last = cw[c - 1:c, :]
    e_in = jnp.exp(cw)
    e_out = jnp.exp(-cw)
    a_t = a * jnp.exp(cw - ld)
    r_t = r * e_in
    b_t = b * e_out
    k_t = k * e_out
    e_tail = jnp.exp(cw_last - cw)
    p_last = jnp.exp(cw_last)

    a_st = stack(a_t)
    r_st = stack(r_t)
    v_st = stack(v)
    aa = _dot_nt(jnp.concatenate([a_st, r_st], axis=0),
                 jnp.concatenate([stack(b_t), stack(k_t)], axis=0))

    row2 = lax.broadcasted_iota(jnp.int32, (c2, c2), 0)
    col2 = lax.broadcasted_iota(jnp.int32, (c2, c2), 1)
    t_row = jnp.where(row2 < c, row2, row2 - c)
    t_col = jnp.where(col2 < c, col2, col2 - c)
    strict = t_row > t_col
    incl = t_row >= t_col
    n_ab = jnp.where(strict, aa[:c2, :c2], 0.0)
    a_ak = jnp.where(strict, aa[:c2, c2:], 0.0)
    a_rb = jnp.where(incl, aa[c2:, :c2], 0.0)
    a_rk = jnp.where(incl, aa[c2:, c2:], 0.0)

    eye = jnp.where(row2 == col2, 1.0, 0.0)
    t_inv = eye + n_ab
    n_pow = n_ab
    steps = c.bit_length() - 2
    for _ in range(steps):
        n_pow = _dot(n_pow, n_pow)
        t_inv = t_inv + _dot(t_inv, n_pow)

    w1 = _dot(a_ak, v_st)
    tw = _dot(t_inv, jnp.concatenate([w1, a_st], axis=1))
    u0 = tw[:, :LANES]
    a_hat = tw[:, LANES:]
    r_hat = r_st + _dot(a_rb, a_hat)
    y0 = _dot(jnp.concatenate([a_rb, a_rk], axis=1), jnp.concatenate([u0, v_st], axis=0))
    bp_st = stack(b * e_tail)
    kp_st = stack(k * e_tail)
    g = jnp.where(row2 == col2, p_last, 0.0) + _dot_tn(bp_st, a_hat)
    h = _dot_tn(jnp.concatenate([bp_st, kp_st], axis=0), jnp.concatenate([u0, v_st], axis=0))

    rm = _dot_f32(jnp.concatenate([r_hat, g], axis=0), m_state)
    y_st = rm[:c2] + y0
    m_new = rm[c2:] + h
    return y_st[:c] + y_st[c:], m_new


def _wkv_kernel(r_ref, ld_ref, k_ref, v_ref, a_ref, b_ref, y_ref, m_ref):
    @pl.when(pl.program_id(2) == 0)
    def _():
        m_ref[...] = jnp.zeros_like(m_ref)

    for p in range(WKV_PAIRS):
        sl = slice(p * LANES, (p + 1) * LANES)
        y, m_new = _wkv_pair(r_ref[:, sl], ld_ref[:, sl], k_ref[:, sl], v_ref[:, sl],
                             a_ref[:, sl], b_ref[:, sl], m_ref[p])
        y_ref[:, sl] = y
        m_ref[p] = m_new


def _wkv(r, ld, k, v, a, b, *, batch, seq):
    t, width = r.shape
    assert t == batch * seq and seq % WKV_CHUNK == 0
    group = WKV_PAIRS * LANES
    assert width % group == 0
    n_chunks = seq // WKV_CHUNK
    spec = pl.BlockSpec((WKV_CHUNK, group), lambda bi, gi, ci: (bi * n_chunks + ci, gi))
    return pl.pallas_call(
        _wkv_kernel,
        grid=(batch, width // group, n_chunks),
        in_specs=[spec] * 6,
        out_specs=spec,
        out_shape=jax.ShapeDtypeStruct((t, width), F32),
        scratch_shapes=[pltpu.VMEM((WKV_PAIRS, LANES, LANES), F32)],
        compiler_params=pltpu.CompilerParams(
            dimension_semantics=("parallel", "parallel", "arbitrary")),
        name="wkv7_chunked",
    )(r, ld, k, v, a, b)


MOBA_BLOCK = 256
MOBA_TOPK = 3
MOBA_HEAD = 128


def _rope(t, cos, sin_signed):
    return t * cos + pltpu.roll(t, MOBA_HEAD // 2, axis=1) * sin_signed


def _moba_kernel(q_ref, k_ref, v_ref, cos_ref, sin_ref, o_ref, kr_ref, kmean_ref, *, nb, scale):
    i = pl.program_id(2)
    blk = MOBA_BLOCK

    @pl.when(i == 0)
    def _():
        kmean_ref[...] = jnp.zeros_like(kmean_ref)
        for n in range(nb):
            rows = slice(n * blk, (n + 1) * blk)
            kr = _rope(k_ref[rows, :], cos_ref[rows, :], sin_ref[rows, :])
            kr_ref[rows, :] = kr.astype(BF16)
            kmean_ref[n:n + 1, :] = jnp.mean(kr, axis=0, keepdims=True)

    q0 = pl.multiple_of(i * blk, blk)
    q = _rope(q_ref[...], cos_ref[pl.ds(q0, blk), :], sin_ref[pl.ds(q0, blk), :])

    gate = lax.dot_general(q, kmean_ref[...], (((1,), (1,)), ((), ())),
                           precision=lax.Precision.HIGHEST, preferred_element_type=F32)
    lane = lax.broadcasted_iota(jnp.int32, (blk, LANES), 1)
    gate = jnp.where(lane < i, gate, -jnp.inf)
    rank = jnp.zeros((blk, LANES), jnp.int32)
    for m in range(nb):
        g_m = gate[:, m:m + 1]
        ahead = jnp.where(g_m > gate, 1, jnp.where(g_m == gate, jnp.where(lane > m, 1, 0), 0))
        rank = rank + ahead
    chosen = jnp.where(lane < i, jnp.where(rank < MOBA_TOPK, 1.0, 0.0), 0.0)

    qb = q.astype(BF16)
    row = lax.broadcasted_iota(jnp.int32, (blk, blk), 0)
    col = lax.broadcasted_iota(jnp.int32, (blk, blk), 1)

    s = _dot_nt(qb, kr_ref[pl.ds(q0, blk), :]) * scale
    s = jnp.where(row >= col, s, -jnp.inf)
    m_run = jnp.max(s, axis=1, keepdims=True)
    p = jnp.exp(s - m_run)
    l_run = jnp.sum(p, axis=1, keepdims=True)
    acc = _dot(p, v_ref[pl.ds(q0, blk), :])

    def past(n, carry):
        m_run, l_run, acc = carry
        k0 = pl.multiple_of(n * blk, blk)
        take = jnp.max(jnp.where(lane == n, chosen, 0.0), axis=1, keepdims=True) > 0.5
        s = _dot_nt(qb, kr_ref[pl.ds(k0, blk), :]) * scale
        s = jnp.where(take, s, -jnp.inf)
        m_new = jnp.maximum(m_run, jnp.max(s, axis=1, keepdims=True))
        alpha = jnp.exp(m_run - m_new)
        p = jnp.exp(s - m_new)
        l_new = alpha * l_run + jnp.sum(p, axis=1, keepdims=True)
        acc_new = alpha * acc + _dot(p, v_ref[pl.ds(k0, blk), :])
        return m_new, l_new, acc_new

    m_run, l_run, acc = lax.fori_loop(0, i, past, (m_run, l_run, acc))
    o_ref[...] = (acc / l_run).astype(o_ref.dtype)


def _moba(u_qkv, cos, sin_signed, *, batch, seq, heads):
    t = u_qkv.shape[0]
    assert seq % MOBA_BLOCK == 0
    nb = seq // MOBA_BLOCK
    kernel = functools.partial(_moba_kernel, nb=nb, scale=MOBA_HEAD ** -0.5)
    full = lambda off: pl.BlockSpec((seq, MOBA_HEAD), lambda bi, hi, qi: (bi, off + hi))
    table = pl.BlockSpec((seq, MOBA_HEAD), lambda bi, hi, qi: (0, 0))
    return pl.pallas_call(
        kernel,
        grid=(batch, heads, nb),
        in_specs=[pl.BlockSpec((MOBA_BLOCK, MOBA_HEAD), lambda bi, hi, qi: (bi * nb + qi, hi)),
                  full(heads), full(2 * heads), table, table],
        out_specs=pl.BlockSpec((MOBA_BLOCK, MOBA_HEAD), lambda bi, hi, qi: (bi * nb + qi, hi)),
        out_shape=jax.ShapeDtypeStruct((t, heads * MOBA_HEAD), BF16),
        scratch_shapes=[pltpu.VMEM((seq, MOBA_HEAD), BF16), pltpu.VMEM((LANES, MOBA_HEAD), F32)],
        compiler_params=pltpu.CompilerParams(
            dimension_semantics=("parallel", "parallel", "arbitrary")),
        name="moba_attention",
    )(u_qkv, u_qkv, u_qkv, cos, sin_signed)


VMEM_LIMIT = 56 * 1024 * 1024


def _matmul_kernel(x_ref, w_ref, o_ref):
    o_ref[...] = jnp.dot(x_ref[...], w_ref[...], preferred_element_type=F32).astype(o_ref.dtype)


def _matmul(x, w, *, tm, tn, out_dtype=F32):
    m, k = x.shape
    n = w.shape[1]
    assert m % tm == 0 and n % tn == 0
    return pl.pallas_call(
        _matmul_kernel,
        grid=(m // tm, n // tn),
        in_specs=[pl.BlockSpec((tm, k), lambda i, j: (i, 0)),
                  pl.BlockSpec((k, tn), lambda i, j: (0, j))],
        out_specs=pl.BlockSpec((tm, tn), lambda i, j: (i, j)),
        out_shape=jax.ShapeDtypeStruct((m, n), out_dtype),
        compiler_params=pltpu.CompilerParams(
            dimension_semantics=("parallel", "parallel"), vmem_limit_bytes=VMEM_LIMIT),
        name="matmul",
    )(x, w)


def _sigmoid(x):
    return 1.0 / (1.0 + jnp.exp(-x))


def _merge_kernel(a_ref, b_ref, pa_ref, pb_ref, ga_ref, gb_ref, o_ref):
    ya = jnp.dot(a_ref[...], pa_ref[...], preferred_element_type=F32)
    yb = jnp.dot(b_ref[...], pb_ref[...], preferred_element_type=F32)
    o_ref[...] = (_sigmoid(ga_ref[...]) * ya + _sigmoid(gb_ref[...]) * yb).astype(o_ref.dtype)


def _merge(o_a, o_b, p_a, p_b, gates, *, tm, tn):
    m, k = o_a.shape
    n = p_a.shape[1]
    nj = n // tn
    row = pl.BlockSpec((tm, k), lambda i, j: (i, 0))
    wcol = pl.BlockSpec((k, tn), lambda i, j: (0, j))
    return pl.pallas_call(
        _merge_kernel,
        grid=(m // tm, nj),
        in_specs=[row, row, wcol, wcol,
                  pl.BlockSpec((tm, tn), lambda i, j: (i, j)),
                  pl.BlockSpec((tm, tn), lambda i, j: (i, nj + j))],
        out_specs=pl.BlockSpec((tm, tn), lambda i, j: (i, j)),
        out_shape=jax.ShapeDtypeStruct((m, n), BF16),
        compiler_params=pltpu.CompilerParams(
            dimension_semantics=("parallel", "parallel"), vmem_limit_bytes=VMEM_LIMIT),
        name="gated_merge",
    )(o_a, o_b, p_a, p_b, gates, gates)


def _layer_norm(x, g, b, eps):
    mu = jnp.mean(x, axis=-1, keepdims=True)
    xc = x - mu
    var = jnp.mean(xc * xc, axis=-1, keepdims=True)
    return xc * lax.rsqrt(var + eps) * g + b


def _proj_ln_kernel(m_ref, w_ref, x_ref, g_ref, b_ref, o_ref, *, alpha, eps):
    y = jnp.dot(m_ref[...], w_ref[...], preferred_element_type=F32)
    o_ref[...] = _layer_norm(alpha * x_ref[...] + y, g_ref[...], b_ref[...], eps)


def _proj_ln(mixed, w, x, g, b, *, tm, alpha, eps):
    m, k = mixed.shape
    n = w.shape[1]
    vec = pl.BlockSpec((1, n), lambda i: (0, 0))
    return pl.pallas_call(
        functools.partial(_proj_ln_kernel, alpha=alpha, eps=eps),
        grid=(m // tm,),
        in_specs=[pl.BlockSpec((tm, k), lambda i: (i, 0)),
                  pl.BlockSpec((k, n), lambda i: (0, 0)),
                  pl.BlockSpec((tm, n), lambda i: (i, 0)), vec, vec],
        out_specs=pl.BlockSpec((tm, n), lambda i: (i, 0)),
        out_shape=jax.ShapeDtypeStruct((m, n), F32),
        compiler_params=pltpu.CompilerParams(
            dimension_semantics=("parallel",), vmem_limit_bytes=VMEM_LIMIT),
        name="out_proj_layernorm",
    )(mixed, w, x, g.reshape(1, n), b.reshape(1, n))


N_EXPERTS = 256
N_GROUPS = 8
TOPK_GROUPS = 4
TOP_K = 8
ROUTED_SCALE = 2.5


def _router_kernel(h_ref, wt_ref, bias_ref, idx_ref, wgt_ref):
    tm = h_ref.shape[0]
    per = N_EXPERTS // N_GROUPS
    logits = lax.dot_general(wt_ref[...], h_ref[...], (((1,), (1,)), ((), ())),
                             precision=lax.Precision.HIGHEST, preferred_element_type=F32)
    scores = _sigmoid(logits)
    biased = scores + bias_ref[...]
    neg = -jnp.inf

    grp = biased.reshape(N_GROUPS, per, tm)
    slot = lax.broadcasted_iota(jnp.int32, (N_GROUPS, per, tm), 1)
    top1 = jnp.max(grp, axis=1, keepdims=True)
    first = jnp.min(jnp.where(grp == top1, slot, per), axis=1, keepdims=True)
    top2 = jnp.max(jnp.where(slot == first, neg, grp), axis=1, keepdims=True)
    grp_score = (top1 + top2).reshape(N_GROUPS, tm)

    gid = lax.broadcasted_iota(jnp.int32, (N_GROUPS, tm), 0)
    rank = jnp.zeros((N_GROUPS, tm), jnp.int32)
    for m in range(N_GROUPS):
        s_m = grp_score[m:m + 1, :]
        rank = rank + jnp.where(s_m > grp_score, 1,
                                jnp.where(s_m == grp_score, jnp.where(gid > m, 1, 0), 0))
    keep = (rank < TOPK_GROUPS).reshape(N_GROUPS, 1, tm)
    masked = jnp.where(keep, grp, neg).reshape(N_EXPERTS, tm)

    eid = lax.broadcasted_iota(jnp.int32, (N_EXPERTS, tm), 0)
    idxs, wgts = [], []
    for _ in range(TOP_K):
        best = jnp.max(masked, axis=0, keepdims=True)
        pick = jnp.min(jnp.where(masked == best, eid, N_EXPERTS), axis=0, keepdims=True)
        hit = eid == pick
        idxs.append(pick)
        wgts.append(jnp.sum(jnp.where(hit, scores, 0.0), axis=0, keepdims=True))
        masked = jnp.where(hit, neg, masked)
    wgt = jnp.concatenate(wgts, axis=0)
    wgt = wgt / jnp.sum(wgt, axis=0, keepdims=True) * ROUTED_SCALE
    idx_ref[...] = jnp.concatenate(idxs, axis=0)
    wgt_ref[...] = wgt


def _router(h, w_router_t, bias, *, tm):
    t, d = h.shape
    out = pl.BlockSpec((TOP_K, tm), lambda i: (0, i))
    return pl.pallas_call(
        _router_kernel,
        grid=(t // tm,),
        in_specs=[pl.BlockSpec((tm, d), lambda i: (i, 0)),
                  pl.BlockSpec((N_EXPERTS, d), lambda i: (0, 0)),
                  pl.BlockSpec((N_EXPERTS, 1), lambda i: (0, 0))],
        out_specs=[out, out],
        out_shape=[jax.ShapeDtypeStruct((TOP_K, t), jnp.int32),
                   jax.ShapeDtypeStruct((TOP_K, t), F32)],
        compiler_params=pltpu.CompilerParams(
            dimension_semantics=("parallel",), vmem_limit_bytes=VMEM_LIMIT),
        name="moe_router",
    )(h, w_router_t, bias.reshape(N_EXPERTS, 1))


EXPERT_ROWS = 128


def _silu(x):
    return x * _sigmoid(x)


def _experts_kernel(blk_e_ref, blk_src_ref, n_used_ref, x_ref, rw_ref, wg_ref, wu_ref, wd_ref,
                    o_ref, wg_bf, wu_bf, wd_bf):
    i = pl.program_id(0)
    used = i < n_used_ref[0]
    prev = blk_e_ref[jnp.maximum(i - 1, 0)]
    fresh = jnp.logical_or(i == 0, blk_e_ref[i] != prev)

    @pl.when(jnp.logical_and(used, fresh))
    def _():
        wg_bf[...] = wg_ref[0].astype(BF16)
        wu_bf[...] = wu_ref[0].astype(BF16)
        wd_bf[...] = wd_ref[0].astype(BF16)

    @pl.when(used)
    def _():
        x = x_ref[...]
        hid = _silu(jnp.dot(x, wg_bf[...], preferred_element_type=F32)) * jnp.dot(
            x, wu_bf[...], preferred_element_type=F32)
        y = jnp.dot(hid.astype(BF16), wd_bf[...], preferred_element_type=F32)
        o_ref[...] = y * rw_ref[...]

    @pl.when(jnp.logical_not(used))
    def _():
        o_ref[...] = jnp.zeros_like(o_ref)


def _experts(xs, row_w, blk_e, blk_src, n_used, w_gate, w_up, w_down):
    rows, d = xs.shape
    ff = w_gate.shape[2]
    n_blk = rows // EXPERT_ROWS
    grid_spec = pltpu.PrefetchScalarGridSpec(
        num_scalar_prefetch=3,
        grid=(n_blk,),
        in_specs=[pl.BlockSpec((EXPERT_ROWS, d), lambda i, e, s, n: (s[i], 0)),
                  pl.BlockSpec((EXPERT_ROWS, 1), lambda i, e, s, n: (s[i], 0)),
                  pl.BlockSpec((1, d, ff), lambda i, e, s, n: (e[i], 0, 0)),
                  pl.BlockSpec((1, d, ff), lambda i, e, s, n: (e[i], 0, 0)),
                  pl.BlockSpec((1, ff, d), lambda i, e, s, n: (e[i], 0, 0))],
        out_specs=pl.BlockSpec((EXPERT_ROWS, d), lambda i, e, s, n: (i, 0)),
        scratch_shapes=[pltpu.VMEM((d, ff), BF16), pltpu.VMEM((d, ff), BF16),
                        pltpu.VMEM((ff, d), BF16)])
    return pl.pallas_call(
        _experts_kernel,
        grid_spec=grid_spec,
        out_shape=jax.ShapeDtypeStruct((rows, d), F32),
        compiler_params=pltpu.CompilerParams(
            dimension_semantics=("arbitrary",), vmem_limit_bytes=VMEM_LIMIT),
        name="routed_experts",
    )(blk_e, blk_src, n_used, xs, row_w, w_gate, w_up, w_down)


def _shared_ln_kernel(h_ref, r_ref, wg_ref, wu_ref, wd_ref, g_ref, b_ref, o_ref, *, alpha, eps):
    h = h_ref[...]
    hb = h.astype(BF16)
    hid = _silu(jnp.dot(hb, wg_ref[...], preferred_element_type=F32)) * jnp.dot(
        hb, wu_ref[...], preferred_element_type=F32)
    shared = jnp.dot(hid.astype(BF16), wd_ref[...], preferred_element_type=F32)
    o_ref[...] = _layer_norm(alpha * h + (r_ref[...] + shared), g_ref[...], b_ref[...], eps)


def _shared_ln(h, routed, w_gate, w_up, w_down, g, b, *, tm, alpha, eps):
    t, d = h.shape
    ff = w_gate.shape[1]
    row = pl.BlockSpec((tm, d), lambda i: (i, 0))
    vec = pl.BlockSpec((1, d), lambda i: (0, 0))
    return pl.pallas_call(
        functools.partial(_shared_ln_kernel, alpha=alpha, eps=eps),
        grid=(t // tm,),
        in_specs=[row, row,
                  pl.BlockSpec((d, ff), lambda i: (0, 0)), pl.BlockSpec((d, ff), lambda i: (0, 0)),
                  pl.BlockSpec((ff, d), lambda i: (0, 0)), vec, vec],
        out_specs=row,
        out_shape=jax.ShapeDtypeStruct((t, d), F32),
        compiler_params=pltpu.CompilerParams(
            dimension_semantics=("parallel",), vmem_limit_bytes=VMEM_LIMIT),
        name="shared_expert_layernorm",
    )(h, routed, w_gate, w_up, w_down, g.reshape(1, d), b.reshape(1, d))


RWKV_GN_EPS = 64e-5
LN_EPS = 1e-5
ROPE_THETA = 10000.0


def _rope_tables(seq, dim):
    inv = 1.0 / (ROPE_THETA ** (jnp.arange(0, dim, 2, dtype=F32) / dim))
    ang = jnp.arange(seq, dtype=F32)[:, None] * inv[None, :]
    ang = jnp.concatenate([ang, ang], axis=-1)
    sign = jnp.where(jnp.arange(dim) < dim // 2, -1.0, 1.0).astype(F32)
    return jnp.cos(ang), jnp.sin(ang) * sign


def _rwkv_branch(u_rkv, u_lora, shift_mu, w_lora_up, w0, a_lora_up, a0, g_lora_up, k_k, k_a, r_k,
                 ln_x_w, ln_x_b, *, batch, seq):
    c = u_rkv.shape[1] // 3
    heads = c // WKV_HEAD
    z = jnp.concatenate([u_rkv, u_lora], axis=1).reshape(batch, seq, -1)
    z_prev = jnp.pad(z, ((0, 0), (1, 0), (0, 0)))[:, :-1]
    z = (z + (z_prev - z) * shift_mu).reshape(batch * seq, -1)
    d_decay, d_aaa = w_lora_up.shape[0], a_lora_up.shape[0]
    cuts = [c, 2 * c, 3 * c, 3 * c + d_decay, 3 * c + d_decay + d_aaa]
    r, k, v, zw, za, zg = jnp.split(z, cuts, axis=-1)
    w = -jax.nn.softplus(-(w0 + jnp.tanh(zw) @ w_lora_up)) - 0.5
    a = jax.nn.sigmoid(a0 + za @ a_lora_up)
    g = jax.nn.sigmoid(zg) @ g_lora_up
    hd = lambda t: t.reshape(batch * seq, heads, WKV_HEAD)
    kk = hd(k * k_k)
    kk = (kk / jnp.maximum(jnp.linalg.norm(kk, axis=-1, keepdims=True), 1e-12)).reshape(-1, c)
    k = k * (1.0 + (a - 1.0) * k_a)
    y = _wkv(r, -jnp.exp(w), k, v, -kk, kk * a, batch=batch, seq=seq)
    y = hd(y)
    mu = y.mean(-1, keepdims=True)
    var = jnp.square(y - mu).mean(-1, keepdims=True)
    y = ((y - mu) * lax.rsqrt(var + RWKV_GN_EPS)).reshape(-1, c) * ln_x_w + ln_x_b
    bonus = ((hd(r) * hd(k) * r_k).sum(-1, keepdims=True) * hd(v)).reshape(-1, c)
    return ((y + bonus) * g).astype(BF16)


def _dispatch(top_idx, top_w, n_tokens):
    n_assign = n_tokens * TOP_K
    e_flat = top_idx.T.reshape(n_assign).astype(jnp.int32)
    w_flat = top_w.T.reshape(n_assign)
    order = jnp.argsort(e_flat).astype(jnp.int32)
    sorted_e = e_flat[order]
    counts = jax.ops.segment_sum(jnp.ones((n_assign,), jnp.int32), e_flat, num_segments=N_EXPERTS)
    padded = (counts + EXPERT_ROWS - 1) // EXPERT_ROWS * EXPERT_ROWS
    pad_end = jnp.cumsum(padded)
    pad_start = pad_end - padded
    grp_start = jnp.cumsum(counts) - counts
    dest = pad_start[sorted_e] + jnp.arange(n_assign, dtype=jnp.int32) - grp_start[sorted_e]
    n_rows = (-(-n_assign // EXPERT_ROWS) + N_EXPERTS) * EXPERT_ROWS
    n_blk = n_rows // EXPERT_ROWS
    row_tok = jnp.full((n_rows,), n_tokens, jnp.int32).at[dest].set(order // TOP_K)
    row_w = jnp.zeros((n_rows,), F32).at[dest].set(w_flat[order])
    pos = jnp.zeros((n_assign,), jnp.int32).at[order].set(dest)
    n_used = (pad_end[-1] // EXPERT_ROWS).astype(jnp.int32)
    blk_src = jnp.minimum(jnp.arange(n_blk, dtype=jnp.int32), n_used - 1)
    blk_e = jnp.minimum(
        jnp.searchsorted(pad_end, blk_src * EXPERT_ROWS, side='right'), N_EXPERTS - 1).astype(jnp.int32)
    return row_tok, row_w, pos, blk_e, blk_src, n_used.reshape(1)


def kernel(x, w_in, shift_mu, w_lora_up, w0, a_lora_up, a0, g_lora_up, k_k, k_a, r_k, ln_x_w, ln_x_b,
           w_o_rwkv, w_o_moba, w_out, ln1_g, ln1_b, w_router, router_bias, w_gate_e, w_up_e,
           w_down_e, w_gate_s, w_up_s, w_down_s, ln2_g, ln2_b):
    batch, seq, d = x.shape
    depth = w_in.shape[0]
    alpha = (2 * depth) ** 0.25
    t = batch * seq
    c = d
    shift_w = shift_mu.shape[1]
    moba_heads = d // MOBA_HEAD
    cos, sin_signed = _rope_tables(seq, MOBA_HEAD)

    h = x.reshape(t, d)
    for layer in range(depth):
        wl = w_in[layer].astype(BF16)
        hb = h.astype(BF16)
        n_lora = shift_w - 3 * c
        lora_pad = -n_lora % LANES
        w_lora = jnp.pad(wl[:, 3 * c:shift_w], ((0, 0), (0, lora_pad)))
        u_rkv = _matmul(hb, wl[:, :3 * c], tm=1024, tn=1024)
        u_lora = _matmul(hb, w_lora, tm=1024, tn=n_lora + lora_pad)[:, :n_lora]
        u_qkv = _matmul(hb, wl[:, shift_w:shift_w + 3 * d], tm=1024, tn=1024)
        u_gate = _matmul(hb, wl[:, shift_w + 3 * d:], tm=1024, tn=1024)

        o_rwkv = _rwkv_branch(u_rkv, u_lora, shift_mu[layer], w_lora_up[layer], w0[layer],
                              a_lora_up[layer], a0[layer], g_lora_up[layer], k_k[layer], k_a[layer],
                              r_k[layer], ln_x_w[layer], ln_x_b[layer],
                              batch=batch, seq=seq)
        o_moba = _moba(u_qkv, cos, sin_signed, batch=batch, seq=seq, heads=moba_heads)

        mixed = _merge(o_rwkv, o_moba, w_o_rwkv[layer].astype(BF16), w_o_moba[layer].astype(BF16),
                       u_gate, tm=512, tn=1024)
        h = _proj_ln(mixed, w_out[layer].astype(BF16), h, ln1_g[layer], ln1_b[layer],
                     tm=512, alpha=alpha, eps=LN_EPS)

        top_idx, top_w = _router(h, w_router[layer].T, router_bias[layer], tm=512)
        row_tok, row_w, pos, blk_e, blk_src, n_used = _dispatch(top_idx, top_w, t)
        h_pad = jnp.concatenate([h.astype(BF16), jnp.zeros((1, d), BF16)], axis=0)
        xs = h_pad[row_tok]
        ys = _experts(xs, row_w.reshape(-1, 1), blk_e, blk_src, n_used,
                      w_gate_e[layer], w_up_e[layer], w_down_e[layer])
        routed = ys[pos].reshape(t, TOP_K, d).sum(axis=1)
        h = _shared_ln(h, routed, w_gate_s[layer].astype(BF16), w_up_s[layer].astype(BF16),
                       w_down_s[layer].astype(BF16), ln2_g[layer], ln2_b[layer],
                       tm=512, alpha=alpha, eps=LN_EPS)
    return h.reshape(batch, seq, d)
```

```python
import functools

import jax
import jax.numpy as jnp
from jax import lax
from jax.experimental import pallas as pl
from jax.experimental.pallas import tpu as pltpu

F32 = jnp.float32
BF16 = jnp.bfloat16

LANES = 128
WKV_HEAD = 64
WKV_CHUNK = 64
WKV_PAIRS = 16


def _dot(a, b):
    return jnp.dot(a.astype(BF16), b.astype(BF16), preferred_element_type=F32)


def _dot_nt(a, b):
    return lax.dot_general(a.astype(BF16), b.astype(BF16), (((1,), (1,)), ((), ())),
                           preferred_element_type=F32)


assert WKV_CHUNK == WKV_HEAD
def _each(f, *lists):
    return [f(*xs) for xs in zip(*lists)]


def _wkv_pairs(r, ld, k, v, a, b, m_state):
    c = WKV_CHUNK
    cat0 = lambda *xs: jnp.concatenate(xs, axis=0)
    cat1 = lambda *xs: jnp.concatenate(xs, axis=1)
    lane = lax.broadcasted_iota(jnp.int32, (c, LANES), 1)
    step = lax.broadcasted_iota(jnp.int32, (c, LANES), 0)
    head0 = lane < WKV_HEAD
    other = jnp.where(head0, lane, lane - WKV_HEAD)

    def stack(x):
        x = x.astype(BF16)
        zero = jnp.zeros_like(x)
        return cat0(jnp.where(head0, x, zero), jnp.where(head0, zero, x))

    def cumsum_rows(x):
        shift = 1
        while shift < c:
            x = x + jnp.where(step >= shift, pltpu.roll(x, shift, axis=0), 0.0)
            shift *= 2
        return x

    dot = lambda p, q: jnp.dot(p.astype(BF16), q, preferred_element_type=F32)

    cw = _each(cumsum_rows, ld)
    e_out = _each(lambda x: jnp.exp(-x), cw)
    a_t = _each(lambda x, w, l: x * jnp.exp(w - l), a, cw, ld)
    r_t = _each(lambda x, w: x * jnp.exp(w), r, cw)
    bk_st = _each(lambda p, q, e: cat0(stack(p * e), stack(q * e)), b, k, e_out)
    v_st = _each(stack, v)
    aa = _each(lambda p, q, s: _dot_nt(cat0(p, q), s), a_t, r_t, bk_st)

    strict = step > other
    incl = step >= other
    n_ab = _each(lambda x: jnp.where(strict, x[:c, :LANES], 0.0), aa)
    a_ak = _each(lambda x: jnp.where(strict, x[:c, LANES:], 0.0), aa)
    a_rb = _each(lambda x: jnp.where(incl, x[c:, :LANES], 0.0), aa)
    a_rk = _each(lambda x: jnp.where(incl, x[c:, LANES:], 0.0), aa)

    eye = jnp.where(step == other, 1.0, 0.0)
    t_inv = _each(lambda n: eye + n, n_ab)
    w1 = _each(dot, a_ak, v_st)
    n_pow = _each(lambda n: dot(n, stack(n)), n_ab)
    for _ in range(c.bit_length() - 3):
        pt = _each(lambda n, t: dot(n, cat1(stack(n), stack(t))), n_pow, t_inv)
        n_pow = _each(lambda x: x[:, :LANES], pt)
        t_inv = _each(lambda t, x: t + x[:, LANES:], t_inv, pt)
    t_inv = _each(lambda t, n: t + dot(n, stack(t)), t_inv, n_pow)

    tw = _each(lambda t, w, x: dot(t, cat1(stack(w), stack(x))), t_inv, w1, a_t)
    rhs = _each(lambda x, v_: cat0(cat1(stack(x[:, LANES:]), stack(x[:, :LANES])),
                                   cat1(jnp.zeros_like(v_), v_)), tw, v_st)
    out_p = _each(lambda p, q, s: dot(cat1(p, q), s), a_rb, a_rk, rhs)
    tail = _each(lambda w: jnp.exp(w[c - 1:c, :] - w), cw)
    bkp_st = _each(lambda p, q, e: cat0(stack(p * e), stack(q * e)), b, k, tail)
    upd = _each(lambda p, s: lax.dot_general(p, s, (((0,), (0,)), ((), ())),
                                             preferred_element_type=F32), bkp_st, rhs)
    upd_p = _each(lambda u: u[:WKV_HEAD] + u[WKV_HEAD:], upd)
    r_hat = _each(lambda x, o: x + o[:, :LANES], r_t, out_p)
    g = _each(lambda w, u: jnp.where(step == other, jnp.exp(w[c - 1:c, :]), 0.0) + u[:, :LANES],
              cw, upd_p)

    def apply_state(lhs, m):
        l_hi = lhs.astype(BF16)
        l_lo = (lhs - l_hi.astype(F32)).astype(BF16)
        m_hi = m.astype(BF16)
        m_lo = m - m_hi.astype(F32)
        both = jnp.dot(l_hi, cat1(stack(m_hi), stack(m_lo)), preferred_element_type=F32)
        return (both[:, :LANES] + both[:, LANES:]) + jnp.dot(l_lo, stack(m_hi),
                                                             preferred_element_type=F32)

    rm = _each(lambda p, q, m: apply_state(cat0(p, q), m), r_hat, g, m_state)
    y = _each(lambda x, o: x[:c] + o[:, LANES:], rm, out_p)
    m_new = _each(lambda x, u: x[c:] + u[:, LANES:], rm, upd_p)
    return y, m_new


def _wkv_kernel(r_ref, ld_ref, k_ref, v_ref, a_ref, b_ref, y_ref, m_ref):
    @pl.when(pl.program_id(2) == 0)
    def _():
        m_ref[...] = jnp.zeros_like(m_ref)

    lanes = [slice(p * LANES, (p + 1) * LANES) for p in range(WKV_PAIRS)]
    take = lambda ref: [ref[:, sl] for sl in lanes]
    y, m_new = _wkv_pairs(take(r_ref), take(ld_ref), take(k_ref), take(v_ref), take(a_ref),
                          take(b_ref), [m_ref[p] for p in range(WKV_PAIRS)])
    for p, sl in enumerate(lanes):
        y_ref[:, sl] = y[p]
        m_ref[p] = m_new[p]


def _wkv(r, ld, k, v, a, b, *, batch, seq):
    t, width = r.shape
    assert t == batch * seq and seq % WKV_CHUNK == 0
    group = WKV_PAIRS * LANES
    assert width % group == 0
    n_chunks = seq // WKV_CHUNK
    spec = pl.BlockSpec((WKV_CHUNK, group), lambda bi, gi, ci: (bi * n_chunks + ci, gi))
    return pl.pallas_call(
        _wkv_kernel,
        grid=(batch, width // group, n_chunks),
        in_specs=[spec] * 6,
        out_specs=spec,
        out_shape=jax.ShapeDtypeStruct((t, width), F32),
        scratch_shapes=[pltpu.VMEM((WKV_PAIRS, WKV_HEAD, LANES), F32)],
        compiler_params=pltpu.CompilerParams(
            dimension_semantics=("parallel", "parallel", "arbitrary")),
        name="wkv7_chunked",
    )(r, ld, k, v, a, b)


MOBA_BLOCK = 256
MOBA_TOPK = 3
MOBA_HEAD = 128


def _rope(t, cos, sin_signed):
    return t * cos + pltpu.roll(t, MOBA_HEAD // 2, axis=1) * sin_signed


def _moba_kernel(q_ref, k_ref, v_ref, cos_ref, sin_ref, o_ref, kr_ref, vb_ref, kmean_ref, *,
                 nb, scale):
    blk = MOBA_BLOCK
    kmean_ref[...] = jnp.zeros_like(kmean_ref)
    for n in range(nb):
        rows = slice(n * blk, (n + 1) * blk)
        kr = _rope(k_ref[rows, :], cos_ref[rows, :], sin_ref[rows, :])
        kr_ref[rows, :] = kr.astype(BF16)
        vb_ref[rows, :] = v_ref[rows, :].astype(BF16)
        kmean_ref[n:n + 1, :] = jnp.mean(kr, axis=0, keepdims=True)

    lane = lax.broadcasted_iota(jnp.int32, (blk, LANES), 1)
    row = lax.broadcasted_iota(jnp.int32, (blk, blk), 0)
    col = lax.broadcasted_iota(jnp.int32, (blk, blk), 1)
    causal = row >= col
    for i in range(nb):
        rows = slice(i * blk, (i + 1) * blk)
        q = _rope(q_ref[rows, :], cos_ref[rows, :], sin_ref[rows, :])
        width = (i + 1) * blk
        s = _dot_nt(q, kr_ref[:width, :]) * scale
        parts = []
        if i > MOBA_TOPK:
            gate = lax.dot_general(q, kmean_ref[...], (((1,), (1,)), ((), ())),
                                   precision=lax.Precision.HIGHEST, preferred_element_type=F32)
            gate = jnp.where(lane < i, gate, -jnp.inf)
            rank = jnp.zeros((blk, LANES), jnp.int32)
            for m in range(i):
                g_m = gate[:, m:m + 1]
                rank = rank + jnp.where(
                    g_m > gate, 1, jnp.where(g_m == gate, jnp.where(lane > m, 1, 0), 0))
            chosen = jnp.where(rank < MOBA_TOPK, 1.0, 0.0)
            for n in range(i):
                take = chosen[:, n:n + 1] > 0.5
                parts.append(jnp.where(take, s[:, n * blk:(n + 1) * blk], -jnp.inf))
        else:
            parts = [s[:, n * blk:(n + 1) * blk] for n in range(i)]
        parts.append(jnp.where(causal, s[:, i * blk:], -jnp.inf))
        s = jnp.concatenate(parts, axis=1) if i else parts[0]
        m_row = jnp.max(s, axis=1, keepdims=True)
        p = jnp.exp(s - m_row)
        l_row = jnp.sum(p, axis=1, keepdims=True)
        acc = jnp.dot(p.astype(BF16), vb_ref[:width, :], preferred_element_type=F32)
        o_ref[rows, :] = (acc / l_row).astype(o_ref.dtype)


def _moba(u_qkv, cos, sin_signed, *, batch, seq, heads):
    t = u_qkv.shape[0]
    assert seq % MOBA_BLOCK == 0
    nb = seq // MOBA_BLOCK
    assert nb <= LANES
    kernel = functools.partial(_moba_kernel, nb=nb, scale=MOBA_HEAD ** -0.5)
    full = lambda off: pl.BlockSpec((seq, MOBA_HEAD), lambda bi, hi: (bi, off + hi))
    table = pl.BlockSpec((seq, MOBA_HEAD), lambda bi, hi: (0, 0))
    return pl.pallas_call(
        kernel,
        grid=(batch, heads),
        in_specs=[full(0), full(heads), full(2 * heads), table, table],
        out_specs=full(0),
        out_shape=jax.ShapeDtypeStruct((t, heads * MOBA_HEAD), BF16),
        scratch_shapes=[pltpu.VMEM((seq, MOBA_HEAD), BF16), pltpu.VMEM((seq, MOBA_HEAD), BF16),
                        pltpu.VMEM((LANES, MOBA_HEAD), F32)],
        compiler_params=pltpu.CompilerParams(dimension_semantics=("parallel", "parallel")),
        name="moba_attention",
    )(u_qkv, u_qkv, u_qkv, cos, sin_signed)


VMEM_LIMIT = 56 * 1024 * 1024


def _matmul_kernel(x_ref, w_ref, o_ref):
    o_ref[...] = jnp.dot(x_ref[...], w_ref[...], preferred_element_type=F32).astype(o_ref.dtype)


def _matmul(x, w, *, tm, tn, out_dtype=F32):
    m, k = x.shape
    n = w.shape[1]
    assert m % tm == 0 and n % tn == 0
    return pl.pallas_call(
        _matmul_kernel,
        grid=(m // tm, n // tn),
        in_specs=[pl.BlockSpec((tm, k), lambda i, j: (i, 0)),
                  pl.BlockSpec((k, tn), lambda i, j: (0, j))],
        out_specs=pl.BlockSpec((tm, tn), lambda i, j: (i, j)),
        out_shape=jax.ShapeDtypeStruct((m, n), out_dtype),
        compiler_params=pltpu.CompilerParams(
            dimension_semantics=("parallel", "parallel"), vmem_limit_bytes=VMEM_LIMIT),
        name="matmul",
    )(x, w)


def _sigmoid(x):
    return 1.0 / (1.0 + jnp.exp(-x))


def _merge_kernel(a_ref, b_ref, pa_ref, pb_ref, ga_ref, gb_ref, o_ref):
    ya = jnp.dot(a_ref[...], pa_ref[...], preferred_element_type=F32)
    yb = jnp.dot(b_ref[...], pb_ref[...], preferred_element_type=F32)
    o_ref[...] = (_sigmoid(ga_ref[...]) * ya + _sigmoid(gb_ref[...]) * yb).astype(o_ref.dtype)


def _merge(o_a, o_b, p_a, p_b, gates, *, tm, tn):
    m, k = o_a.shape
    n = p_a.shape[1]
    nj = n // tn
    row = pl.BlockSpec((tm, k), lambda i, j: (i, 0))
    wcol = pl.BlockSpec((k, tn), lambda i, j: (0, j))
    return pl.pallas_call(
        _merge_kernel,
        grid=(m // tm, nj),
        in_specs=[row, row, wcol, wcol,
                  pl.BlockSpec((tm, tn), lambda i, j: (i, j)),
                  pl.BlockSpec((tm, tn), lambda i, j: (i, nj + j))],
        out_specs=pl.BlockSpec((tm, tn), lambda i, j: (i, j)),
        out_shape=jax.ShapeDtypeStruct((m, n), BF16),
        compiler_params=pltpu.CompilerParams(
            dimension_semantics=("parallel", "parallel"), vmem_limit_bytes=VMEM_LIMIT),
        name="gated_merge",
    )(o_a, o_b, p_a, p_b, gates, gates)


def _layer_norm(x, g, b, eps):
    mu = jnp.mean(x, axis=-1, keepdims=True)
    xc = x - mu
    var = jnp.mean(xc * xc, axis=-1, keepdims=True)
    return xc * lax.rsqrt(var + eps) * g + b


def _proj_ln_kernel(m_ref, w_ref, x_ref, g_ref, b_ref, o_ref, *, alpha, eps):
    y = jnp.dot(m_ref[...], w_ref[...], preferred_element_type=F32)
    o_ref[...] = _layer_norm(alpha * x_ref[...] + y, g_ref[...], b_ref[...], eps)


def _proj_ln(mixed, w, x, g, b, *, tm, alpha, eps):
    m, k = mixed.shape
    n = w.shape[1]
    vec = pl.BlockSpec((1, n), lambda i: (0, 0))
    return pl.pallas_call(
        functools.partial(_proj_ln_kernel, alpha=alpha, eps=eps),
        grid=(m // tm,),
        in_specs=[pl.BlockSpec((tm, k), lambda i: (i, 0)),
                  pl.BlockSpec((k, n), lambda i: (0, 0)),
                  pl.BlockSpec((tm, n), lambda i: (i, 0)), vec, vec],
        out_specs=pl.BlockSpec((tm, n), lambda i: (i, 0)),
        out_shape=jax.ShapeDtypeStruct((m, n), F32),
        compiler_params=pltpu.CompilerParams(
            dimension_semantics=("parallel",), vmem_limit_bytes=VMEM_LIMIT),
        name="out_proj_layernorm",
    )(mixed, w, x, g.reshape(1, n), b.reshape(1, n))


N_EXPERTS = 256
N_GROUPS = 8
TOPK_GROUPS = 4
TOP_K = 8
ROUTED_SCALE = 2.5


def _router_kernel(h_ref, wt_ref, bias_ref, idx_ref, wgt_ref):
    tm = h_ref.shape[0]
    per = N_EXPERTS // N_GROUPS
    logits = lax.dot_general(wt_ref[...], h_ref[...], (((1,), (1,)), ((), ())),
                             precision=lax.Precision.HIGHEST, preferred_element_type=F32)
    scores = _sigmoid(logits)
    biased = scores + bias_ref[...]
    neg = -jnp.inf

    grp = biased.reshape(N_GROUPS, per, tm)
    slot = lax.broadcasted_iota(jnp.int32, (N_GROUPS, per, tm), 1)
    top1 = jnp.max(grp, axis=1, keepdims=True)
    first = jnp.min(jnp.where(grp == top1, slot, per), axis=1, keepdims=True)
    top2 = jnp.max(jnp.where(slot == first, neg, grp), axis=1, keepdims=True)
    grp_score = (top1 + top2).reshape(N_GROUPS, tm)

    gid = lax.broadcasted_iota(jnp.int32, (N_GROUPS, tm), 0)
    rank = jnp.zeros((N_GROUPS, tm), jnp.int32)
    for m in range(N_GROUPS):
        s_m = grp_score[m:m + 1, :]
        rank = rank + jnp.where(s_m > grp_score, 1,
                                jnp.where(s_m == grp_score, jnp.where(gid > m, 1, 0), 0))
    keep = (rank < TOPK_GROUPS).reshape(N_GROUPS, 1, tm)
    masked = jnp.where(keep, grp, neg).reshape(N_EXPERTS, tm)

    eid = lax.broadcasted_iota(jnp.int32, (N_EXPERTS, tm), 0)
    idxs, wgts = [], []
    for _ in range(TOP_K):
        best = jnp.max(masked, axis=0, keepdims=True)
        pick = jnp.min(jnp.where(masked == best, eid, N_EXPERTS), axis=0, keepdims=True)
        hit = eid == pick
        idxs.append(pick)
        wgts.append(jnp.sum(jnp.where(hit, scores, 0.0), axis=0, keepdims=True))
        masked = jnp.where(hit, neg, masked)
    wgt = jnp.concatenate(wgts, axis=0)
    wgt = wgt / jnp.sum(wgt, axis=0, keepdims=True) * ROUTED_SCALE
    idx_ref[...] = jnp.concatenate(idxs, axis=0)
    wgt_ref[...] = wgt


def _router(h, w_router_t, bias, *, tm):
    t, d = h.shape
    out = pl.BlockSpec((TOP_K, tm), lambda i: (0, i))
    return pl.pallas_call(
        _router_kernel,
        grid=(t // tm,),
        in_specs=[pl.BlockSpec((tm, d), lambda i: (i, 0)),
                  pl.BlockSpec((N_EXPERTS, d), lambda i: (0, 0)),
                  pl.BlockSpec((N_EXPERTS, 1), lambda i: (0, 0))],
        out_specs=[out, out],
        out_shape=[jax.ShapeDtypeStruct((TOP_K, t), jnp.int32),
                   jax.ShapeDtypeStruct((TOP_K, t), F32)],
        compiler_params=pltpu.CompilerParams(
            dimension_semantics=("parallel",), vmem_limit_bytes=VMEM_LIMIT),
        name="moe_router",
    )(h, w_router_t, bias.reshape(N_EXPERTS, 1))


EXPERT_ROWS = 128


def _silu(x):
    return x * _sigmoid(x)


def _experts_kernel(blk_e_ref, blk_src_ref, n_used_ref, x_ref, rw_ref, wg_ref, wu_ref, wd_ref,
                    o_ref, wg_bf, wu_bf, wd_bf):
    i = pl.program_id(0)
    used = i < n_used_ref[0]
    prev = blk_e_ref[jnp.maximum(i - 1, 0)]
    fresh = jnp.logical_or(i == 0, blk_e_ref[i] != prev)

    @pl.when(jnp.logical_and(used, fresh))
    def _():
        wg_bf[...] = wg_ref[0].astype(BF16)
        wu_bf[...] = wu_ref[0].astype(BF16)
        wd_bf[...] = wd_ref[0].astype(BF16)

    @pl.when(used)
    def _():
        x = x_ref[...]
        hid = _silu(jnp.dot(x, wg_bf[...], preferred_element_type=F32)) * jnp.dot(
            x, wu_bf[...], preferred_element_type=F32)
        y = jnp.dot(hid.astype(BF16), wd_bf[...], preferred_element_type=F32)
        o_ref[...] = y * rw_ref[...]

    @pl.when(jnp.logical_not(used))
    def _():
        o_ref[...] = jnp.zeros_like(o_ref)


def _experts(xs, row_w, blk_e, blk_src, n_used, w_gate, w_up, w_down):
    rows, d = xs.shape
    ff = w_gate.shape[2]
    n_blk = rows // EXPERT_ROWS
    grid_spec = pltpu.PrefetchScalarGridSpec(
        num_scalar_prefetch=3,
        grid=(n_blk,),
        in_specs=[pl.BlockSpec((EXPERT_ROWS, d), lambda i, e, s, n: (s[i], 0)),
                  pl.BlockSpec((EXPERT_ROWS, 1), lambda i, e, s, n: (s[i], 0)),
                  pl.BlockSpec((1, d, ff), lambda i, e, s, n: (e[i], 0, 0)),
                  pl.BlockSpec((1, d, ff), lambda i, e, s, n: (e[i], 0, 0)),
                  pl.BlockSpec((1, ff, d), lambda i, e, s, n: (e[i], 0, 0))],
        out_specs=pl.BlockSpec((EXPERT_ROWS, d), lambda i, e, s, n: (i, 0)),
        scratch_shapes=[pltpu.VMEM((d, ff), BF16), pltpu.VMEM((d, ff), BF16),
                        pltpu.VMEM((ff, d), BF16)])
    return pl.pallas_call(
        _experts_kernel,
        grid_spec=grid_spec,
        out_shape=jax.ShapeDtypeStruct((rows, d), F32),
        compiler_params=pltpu.CompilerParams(
            dimension_semantics=("arbitrary",), vmem_limit_bytes=VMEM_LIMIT),
        name="routed_experts",
    )(blk_e, blk_src, n_used, xs, row_w, w_gate, w_up, w_down)


def _shared_ln_kernel(h_ref, r_ref, wg_ref, wu_ref, wd_ref, g_ref, b_ref, o_ref, *, alpha, eps):
    h = h_ref[...]
    hb = h.astype(BF16)
    hid = _silu(jnp.dot(hb, wg_ref[...], preferred_element_type=F32)) * jnp.dot(
        hb, wu_ref[...], preferred_element_type=F32)
    shared = jnp.dot(hid.astype(BF16), wd_ref[...], preferred_element_type=F32)
    o_ref[...] = _layer_norm(alpha * h + (r_ref[...] + shared), g_ref[...], b_ref[...], eps)


def _shared_ln(h, routed, w_gate, w_up, w_down, g, b, *, tm, alpha, eps):
    t, d = h.shape
    ff = w_gate.shape[1]
    row = pl.BlockSpec((tm, d), lambda i: (i, 0))
    vec = pl.BlockSpec((1, d), lambda i: (0, 0))
    return pl.pallas_call(
        functools.partial(_shared_ln_kernel, alpha=alpha, eps=eps),
        grid=(t // tm,),
        in_specs=[row, row,
                  pl.BlockSpec((d, ff), lambda i: (0, 0)), pl.BlockSpec((d, ff), lambda i: (0, 0)),
                  pl.BlockSpec((ff, d), lambda i: (0, 0)), vec, vec],
        out_specs=row,
        out_shape=jax.ShapeDtypeStruct((t, d), F32),
        compiler_params=pltpu.CompilerParams(
            dimension_semantics=("parallel",), vmem_limit_bytes=VMEM_LIMIT),
        name="shared_expert_layernorm",
    )(h, routed, w_gate, w_up, w_down, g.reshape(1, d), b.reshape(1, d))


RWKV_GN_EPS = 64e-5
LN_EPS = 1e-5
ROPE_THETA = 10000.0


def _rope_tables(seq, dim):
    inv = 1.0 / (ROPE_THETA ** (jnp.arange(0, dim, 2, dtype=F32) / dim))
    ang = jnp.arange(seq, dtype=F32)[:, None] * inv[None, :]
    ang = jnp.concatenate([ang, ang], axis=-1)
    sign = jnp.where(jnp.arange(dim) < dim // 2, -1.0, 1.0).astype(F32)
    return jnp.cos(ang), jnp.sin(ang) * sign


def _rwkv_branch(u_rkv, u_lora, shift_mu, w_lora_up, w0, a_lora_up, a0, g_lora_up, k_k, k_a, r_k,
                 ln_x_w, ln_x_b, *, batch, seq):
    c = u_rkv.shape[1] // 3
    heads = c // WKV_HEAD
    z = jnp.concatenate([u_rkv, u_lora], axis=1).reshape(batch, seq, -1)
    z_prev = jnp.pad(z, ((0, 0), (1, 0), (0, 0)))[:, :-1]
    z = (z + (z_prev - z) * shift_mu).reshape(batch * seq, -1)
    d_decay, d_aaa = w_lora_up.shape[0], a_lora_up.shape[0]
    cuts = [c, 2 * c, 3 * c, 3 * c + d_decay, 3 * c + d_decay + d_aaa]
    r, k, v, zw, za, zg = jnp.split(z, cuts, axis=-1)
    w = -jax.nn.softplus(-(w0 + jnp.tanh(zw) @ w_lora_up)) - 0.5
    a = jax.nn.sigmoid(a0 + za @ a_lora_up)
    g = jax.nn.sigmoid(zg) @ g_lora_up
    hd = lambda t: t.reshape(batch * seq, heads, WKV_HEAD)
    kk = hd(k * k_k)
    kk = (kk / jnp.maximum(jnp.linalg.norm(kk, axis=-1, keepdims=True), 1e-12)).reshape(-1, c)
    k = k * (1.0 + (a - 1.0) * k_a)
    y = _wkv(r, -jnp.exp(w), k, v, -kk, kk * a, batch=batch, seq=seq)
    y = hd(y)
    mu = y.mean(-1, keepdims=True)
    var = jnp.square(y - mu).mean(-1, keepdims=True)
    y = ((y - mu) * lax.rsqrt(var + RWKV_GN_EPS)).reshape(-1, c) * ln_x_w + ln_x_b
    bonus = ((hd(r) * hd(k) * r_k).sum(-1, keepdims=True) * hd(v)).reshape(-1, c)
    return ((y + bonus) * g).astype(BF16)


def _dispatch(top_idx, top_w, n_tokens):
    n_assign = n_tokens * TOP_K
    e_flat = top_idx.T.reshape(n_assign).astype(jnp.int32)
    w_flat = top_w.T.reshape(n_assign)
    order = jnp.argsort(e_flat).astype(jnp.int32)
    sorted_e = e_flat[order]
    counts = jax.ops.segment_sum(jnp.ones((n_assign,), jnp.int32), e_flat, num_segments=N_EXPERTS)
    padded = (counts + EXPERT_ROWS - 1) // EXPERT_ROWS * EXPERT_ROWS
    pad_end = jnp.cumsum(padded)
    pad_start = pad_end - padded
    grp_start = jnp.cumsum(counts) - counts
    dest = pad_start[sorted_e] + jnp.arange(n_assign, dtype=jnp.int32) - grp_start[sorted_e]
    n_rows = (-(-n_assign // EXPERT_ROWS) + N_EXPERTS) * EXPERT_ROWS
    n_blk = n_rows // EXPERT_ROWS
    row_tok = jnp.full((n_rows,), n_tokens, jnp.int32).at[dest].set(order // TOP_K)
    row_w = jnp.zeros((n_rows,), F32).at[dest].set(w_flat[order])
    pos = jnp.zeros((n_assign,), jnp.int32).at[order].set(dest)
    n_used = (pad_end[-1] // EXPERT_ROWS).astype(jnp.int32)
    blk_src = jnp.minimum(jnp.arange(n_blk, dtype=jnp.int32), n_used - 1)
    blk_e = jnp.minimum(
        jnp.searchsorted(pad_end, blk_src * EXPERT_ROWS, side='right'), N_EXPERTS - 1).astype(jnp.int32)
    return row_tok, row_w, pos, blk_e, blk_src, n_used.reshape(1)


def kernel(x, w_in, shift_mu, w_lora_up, w0, a_lora_up, a0, g_lora_up, k_k, k_a, r_k, ln_x_w, ln_x_b,
           w_o_rwkv, w_o_moba, w_out, ln1_g, ln1_b, w_router, router_bias, w_gate_e, w_up_e,
           w_down_e, w_gate_s, w_up_s, w_down_s, ln2_g, ln2_b):
    batch, seq, d = x.shape
    depth = w_in.shape[0]
    alpha = (2 * depth) ** 0.25
    t = batch * seq
    c = d
    shift_w = shift_mu.shape[1]
    moba_heads = d // MOBA_HEAD
    cos, sin_signed = _rope_tables(seq, MOBA_HEAD)

    h = x.reshape(t, d)
    for layer in range(depth):
        wl = w_in[layer].astype(BF16)
        hb = h.astype(BF16)
        n_lora = shift_w - 3 * c
        lora_pad = -n_lora % LANES
        w_lora = jnp.pad(wl[:, 3 * c:shift_w], ((0, 0), (0, lora_pad)))
        u_rkv = _matmul(hb, wl[:, :3 * c], tm=1024, tn=1024)
        u_lora = _matmul(hb, w_lora, tm=1024, tn=n_lora + lora_pad)[:, :n_lora]
        u_qkv = _matmul(hb, wl[:, shift_w:shift_w + 3 * d], tm=1024, tn=1024)
        u_gate = _matmul(hb, wl[:, shift_w + 3 * d:], tm=1024, tn=1024)

        o_rwkv = _rwkv_branch(u_rkv, u_lora, shift_mu[layer], w_lora_up[layer], w0[layer],
                              a_lora_up[layer], a0[layer], g_lora_up[layer], k_k[layer], k_a[layer],
                              r_k[layer], ln_x_w[layer], ln_x_b[layer],
                              batch=batch, seq=seq)
        o_moba = _moba(u_qkv, cos, sin_signed, batch=batch, seq=seq, heads=moba_heads)

        mixed = _merge(o_rwkv, o_moba, w_o_rwkv[layer].astype(BF16), w_o_moba[layer].astype(BF16),
                       u_gate, tm=512, tn=1024)
        h = _proj_ln(mixed, w_out[layer].astype(BF16), h, ln1_g[layer], ln1_b[layer],
                     tm=512, alpha=alpha, eps=LN_EPS)

        top_idx, top_w = _router(h, w_router[layer].T, router_bias[layer], tm=512)
        row_tok, row_w, pos, blk_e, blk_src, n_used = _dispatch(top_idx, top_w, t)
        h_pad = jnp.concatenate([h.astype(BF16), jnp.zeros((1, d), BF16)], axis=0)
        xs = h_pad[row_tok]
        ys = _experts(xs, row_w.reshape(-1, 1), blk_e, blk_src, n_used,
                      w_gate_e[layer], w_up_e[layer], w_down_e[layer])
        routed = ys[pos].reshape(t, TOP_K, d).sum(axis=1)
        h = _shared_ln(h, routed, w_gate_s[layer].astype(BF16), w_up_s[layer].astype(BF16),
                       w_down_s[layer].astype(BF16), ln2_g[layer], ln2_b[layer],
                       tm=512, alpha=alpha, eps=LN_EPS)
    return h.reshape(batch, seq, d)
```

```python
import functools

import jax
import jax.numpy as jnp
from jax import lax
from jax.experimental import pallas as pl
from jax.experimental.pallas import tpu as pltpu

F32 = jnp.float32
BF16 = jnp.bfloat16

LANES = 128
WKV_HEAD = 64
WKV_CHUNK = 64
WKV_PAIRS = 16


def _dot(a, b):
    return jnp.dot(a.astype(BF16), b.astype(BF16), preferred_element_type=F32)


def _dot_nt(a, b):
    return lax.dot_general(a.astype(BF16), b.astype(BF16), (((1,), (1,)), ((), ())),
                           preferred_element_type=F32)


assert WKV_CHUNK == WKV_HEAD
def _each(f, *lists):
    return [f(*xs) for xs in zip(*lists)]


def _wkv_pairs(r, ld, k, v, a, b, m_state):
    c = WKV_CHUNK
    cat0 = lambda *xs: jnp.concatenate(xs, axis=0)
    cat1 = lambda *xs: jnp.concatenate(xs, axis=1)
    lane = lax.broadcasted_iota(jnp.int32, (c, LANES), 1)
    step = lax.broadcasted_iota(jnp.int32, (c, LANES), 0)
    head0 = lane < WKV_HEAD
    other = jnp.where(head0, lane, lane - WKV_HEAD)

    def stack(x):
        x = x.astype(BF16)
        zero = jnp.zeros_like(x)
        return cat0(jnp.where(head0, x, zero), jnp.where(head0, zero, x))

    def cumsum_rows(x):
        shift = 1
        while shift < c:
            x = x + jnp.where(step >= shift, pltpu.roll(x, shift, axis=0), 0.0)
            shift *= 2
        return x

    dot = lambda p, q: jnp.dot(p.astype(BF16), q, preferred_element_type=F32)

    cw = _each(cumsum_rows, ld)
    e_out = _each(lambda x: jnp.exp(-x), cw)
    a_t = _each(lambda x, w, l: x * jnp.exp(w - l), a, cw, ld)
    r_t = _each(lambda x, w: x * jnp.exp(w), r, cw)
    bk_st = _each(lambda p, q, e: cat0(stack(p * e), stack(q * e)), b, k, e_out)
    v_st = _each(stack, v)
    aa = _each(lambda p, q, s: _dot_nt(cat0(p, q), s), a_t, r_t, bk_st)

    strict = step > other
    incl = step >= other
    n_ab = _each(lambda x: jnp.where(strict, x[:c, :LANES], 0.0), aa)
    a_ak = _each(lambda x: jnp.where(strict, x[:c, LANES:], 0.0), aa)
    a_rb = _each(lambda x: jnp.where(incl, x[c:, :LANES], 0.0), aa)
    a_rk = _each(lambda x: jnp.where(incl, x[c:, LANES:], 0.0), aa)

    eye = jnp.where(step == other, 1.0, 0.0)
    t_inv = _each(lambda n: eye + n, n_ab)
    w1 = _each(dot, a_ak, v_st)
    n_pow = _each(lambda n: dot(n, stack(n)), n_ab)
    for _ in range(c.bit_length() - 3):
        pt = _each(lambda n, t: dot(n, cat1(stack(n), stack(t))), n_pow, t_inv)
        n_pow = _each(lambda x: x[:, :LANES], pt)
        t_inv = _each(lambda t, x: t + x[:, LANES:], t_inv, pt)
    t_inv = _each(lambda t, n: t + dot(n, stack(t)), t_inv, n_pow)

    tw = _each(lambda t, w, x: dot(t, cat1(stack(w), stack(x))), t_inv, w1, a_t)
    rhs = _each(lambda x, v_: cat0(cat1(stack(x[:, LANES:]), stack(x[:, :LANES])),
                                   cat1(jnp.zeros_like(v_), v_)), tw, v_st)
    out_p = _each(lambda p, q, s: dot(cat1(p, q), s), a_rb, a_rk, rhs)
    tail = _each(lambda w: jnp.exp(w[c - 1:c, :] - w), cw)
    bkp_st = _each(lambda p, q, e: cat0(stack(p * e), stack(q * e)), b, k, tail)
    upd = _each(lambda p, s: lax.dot_general(p, s, (((0,), (0,)), ((), ())),
                                             preferred_element_type=F32), bkp_st, rhs)
    upd_p = _each(lambda u: u[:WKV_HEAD] + u[WKV_HEAD:], upd)
    r_hat = _each(lambda x, o: x + o[:, :LANES], r_t, out_p)
    g = _each(lambda w, u: jnp.where(step == other, jnp.exp(w[c - 1:c, :]), 0.0) + u[:, :LANES],
              cw, upd_p)

    def apply_state(lhs, m):
        l_hi = lhs.astype(BF16)
        l_lo = (lhs - l_hi.astype(F32)).astype(BF16)
        m_hi = m.astype(BF16)
        m_lo = m - m_hi.astype(F32)
        both = jnp.dot(l_hi, cat1(stack(m_hi), stack(m_lo)), preferred_element_type=F32)
        return (both[:, :LANES] + both[:, LANES:]) + jnp.dot(l_lo, stack(m_hi),
                                                             preferred_element_type=F32)

    rm = _each(lambda p, q, m: apply_state(cat0(p, q), m), r_hat, g, m_state)
    y = _each(lambda x, o: x[:c] + o[:, LANES:], rm, out_p)
    m_new = _each(lambda x, u: x[c:] + u[:, LANES:], rm, upd_p)
    return y, m_new


def _wkv_kernel(r_ref, ld_ref, k_ref, v_ref, a_ref, b_ref, y_ref, m_ref):
    @pl.when(pl.program_id(2) == 0)
    def _():
        m_ref[...] = jnp.zeros_like(m_ref)

    lanes = [slice(p * LANES, (p + 1) * LANES) for p in range(WKV_PAIRS)]
    take = lambda ref: [ref[:, sl] for sl in lanes]
    y, m_new = _wkv_pairs(take(r_ref), take(ld_ref), take(k_ref), take(v_ref), take(a_ref),
                          take(b_ref), [m_ref[p] for p in range(WKV_PAIRS)])
    for p, sl in enumerate(lanes):
        y_ref[:, sl] = y[p]
        m_ref[p] = m_new[p]


def _wkv(r, ld, k, v, a, b, *, batch, seq):
    t, width = r.shape
    assert t == batch * seq and seq % WKV_CHUNK == 0
    group = WKV_PAIRS * LANES
    assert width % group == 0
    n_chunks = seq // WKV_CHUNK
    spec = pl.BlockSpec((WKV_CHUNK, group), lambda bi, gi, ci: (bi * n_chunks + ci, gi))
    return pl.pallas_call(
        _wkv_kernel,
        grid=(batch, width // group, n_chunks),
        in_specs=[spec] * 6,
        out_specs=spec,
        out_shape=jax.ShapeDtypeStruct((t, width), F32),
        scratch_shapes=[pltpu.VMEM((WKV_PAIRS, WKV_HEAD, LANES), F32)],
        compiler_params=pltpu.CompilerParams(
            dimension_semantics=("parallel", "parallel", "arbitrary")),
        name="wkv7_chunked",
    )(r, ld, k, v, a, b)


MOBA_BLOCK = 256
MOBA_TOPK = 3
MOBA_HEAD = 128


def _rope(t, cos, sin_signed):
    return t * cos + pltpu.roll(t, MOBA_HEAD // 2, axis=1) * sin_signed


def _moba_kernel(q_ref, k_ref, v_ref, cos_ref, sin_ref, o_ref, kr_ref, vb_ref, kmean_ref, *,
                 nb, scale):
    blk = MOBA_BLOCK
    kmean_ref[...] = jnp.zeros_like(kmean_ref)
    for n in range(nb):
        rows = slice(n * blk, (n + 1) * blk)
        kr = _rope(k_ref[rows, :], cos_ref[rows, :], sin_ref[rows, :])
        kr_ref[rows, :] = kr.astype(BF16)
        vb_ref[rows, :] = v_ref[rows, :].astype(BF16)
        kmean_ref[n:n + 1, :] = jnp.mean(kr, axis=0, keepdims=True)

    lane = lax.broadcasted_iota(jnp.int32, (blk, LANES), 1)
    row = lax.broadcasted_iota(jnp.int32, (blk, blk), 0)
    col = lax.broadcasted_iota(jnp.int32, (blk, blk), 1)
    causal = row >= col
    for i in range(nb):
        rows = slice(i * blk, (i + 1) * blk)
        q = _rope(q_ref[rows, :], cos_ref[rows, :], sin_ref[rows, :])
        width = (i + 1) * blk
        s = _dot_nt(q, kr_ref[:width, :]) * scale
        parts = []
        if i > MOBA_TOPK:
            gate = lax.dot_general(q, kmean_ref[...], (((1,), (1,)), ((), ())),
                                   precision=lax.Precision.HIGHEST, preferred_element_type=F32)
            gate = jnp.where(lane < i, gate, -jnp.inf)
            rank = jnp.zeros((blk, LANES), jnp.int32)
            for m in range(i):
                g_m = gate[:, m:m + 1]
                rank = rank + jnp.where(
                    g_m > gate, 1, jnp.where(g_m == gate, jnp.where(lane > m, 1, 0), 0))
            chosen = jnp.where(rank < MOBA_TOPK, 1.0, 0.0)
            for n in range(i):
                take = chosen[:, n:n + 1] > 0.5
                parts.append(jnp.where(take, s[:, n * blk:(n + 1) * blk], -jnp.inf))
        else:
            parts = [s[:, n * blk:(n + 1) * blk] for n in range(i)]
        parts.append(jnp.where(causal, s[:, i * blk:], -jnp.inf))
        s = jnp.concatenate(parts, axis=1) if i else parts[0]
        m_row = jnp.max(s, axis=1, keepdims=True)
        p = jnp.exp(s - m_row)
        l_row = jnp.sum(p, axis=1, keepdims=True)
        acc = jnp.dot(p.astype(BF16), vb_ref[:width, :], preferred_element_type=F32)
        o_ref[rows, :] = (acc / l_row).astype(o_ref.dtype)


def _moba(u_qkv, cos, sin_signed, *, batch, seq, heads):
    t = u_qkv.shape[0]
    assert seq % MOBA_BLOCK == 0
    nb = seq // MOBA_BLOCK
    assert nb <= LANES
    kernel = functools.partial(_moba_kernel, nb=nb, scale=MOBA_HEAD ** -0.5)
    full = lambda off: pl.BlockSpec((seq, MOBA_HEAD), lambda bi, hi: (bi, off + hi))
    table = pl.BlockSpec((seq, MOBA_HEAD), lambda bi, hi: (0, 0))
    return pl.pallas_call(
        kernel,
        grid=(batch, heads),
        in_specs=[full(0), full(heads), full(2 * heads), table, table],
        out_specs=full(0),
        out_shape=jax.ShapeDtypeStruct((t, heads * MOBA_HEAD), BF16),
        scratch_shapes=[pltpu.VMEM((seq, MOBA_HEAD), BF16), pltpu.VMEM((seq, MOBA_HEAD), BF16),
                        pltpu.VMEM((LANES, MOBA_HEAD), F32)],
        compiler_params=pltpu.CompilerParams(dimension_semantics=("parallel", "parallel")),
        name="moba_attention",
    )(u_qkv, u_qkv, u_qkv, cos, sin_signed)


VMEM_LIMIT = 56 * 1024 * 1024


def _matmul_kernel(x_ref, w_ref, o_ref):
    o_ref[...] = jnp.dot(x_ref[...], w_ref[...], preferred_element_type=F32).astype(o_ref.dtype)


def _matmul(x, w, *, tm, tn, out_dtype=F32):
    m, k = x.shape
    n = w.shape[1]
    assert m % tm == 0 and n % tn == 0
    return pl.pallas_call(
        _matmul_kernel,
        grid=(m // tm, n // tn),
        in_specs=[pl.BlockSpec((tm, k), lambda i, j: (i, 0)),
                  pl.BlockSpec((k, tn), lambda i, j: (0, j))],
        out_specs=pl.BlockSpec((tm, tn), lambda i, j: (i, j)),
        out_shape=jax.ShapeDtypeStruct((m, n), out_dtype),
        compiler_params=pltpu.CompilerParams(
            dimension_semantics=("parallel", "parallel"), vmem_limit_bytes=VMEM_LIMIT),
        name="matmul",
    )(x, w)


def _sigmoid(x):
    return 1.0 / (1.0 + jnp.exp(-x))


def _merge_kernel(a_ref, b_ref, pa_ref, pb_ref, ga_ref, gb_ref, o_ref):
    ya = jnp.dot(a_ref[...], pa_ref[...], preferred_element_type=F32)
    yb = jnp.dot(b_ref[...], pb_ref[...], preferred_element_type=F32)
    o_ref[...] = (_sigmoid(ga_ref[...]) * ya + _sigmoid(gb_ref[...]) * yb).astype(o_ref.dtype)


def _merge(o_a, o_b, p_a, p_b, gates, *, tm, tn):
    m, k = o_a.shape
    n = p_a.shape[1]
    nj = n // tn
    row = pl.BlockSpec((tm, k), lambda i, j: (i, 0))
    wcol = pl.BlockSpec((k, tn), lambda i, j: (0, j))
    return pl.pallas_call(
        _merge_kernel,
        grid=(m // tm, nj),
        in_specs=[row, row, wcol, wcol,
                  pl.BlockSpec((tm, tn), lambda i, j: (i, j)),
                  pl.BlockSpec((tm, tn), lambda i, j: (i, nj + j))],
        out_specs=pl.BlockSpec((tm, tn), lambda i, j: (i, j)),
        out_shape=jax.ShapeDtypeStruct((m, n), BF16),
        compiler_params=pltpu.CompilerParams(
            dimension_semantics=("parallel", "parallel"), vmem_limit_bytes=VMEM_LIMIT),
        name="gated_merge",
    )(o_a, o_b, p_a, p_b, gates, gates)


def _layer_norm(x, g, b, eps):
    mu = jnp.mean(x, axis=-1, keepdims=True)
    xc = x - mu
    var = jnp.mean(xc * xc, axis=-1, keepdims=True)
    return xc * lax.rsqrt(var + eps) * g + b


def _proj_ln_kernel(m_ref, w_ref, x_ref, g_ref, b_ref, o_ref, *, alpha, eps):
    y = jnp.dot(m_ref[...], w_ref[...], preferred_element_type=F32)
    o_ref[...] = _layer_norm(alpha * x_ref[...] + y, g_ref[...], b_ref[...], eps)


def _proj_ln(mixed, w, x, g, b, *, tm, alpha, eps):
    m, k = mixed.shape
    n = w.shape[1]
    vec = pl.BlockSpec((1, n), lambda i: (0, 0))
    return pl.pallas_call(
        functools.partial(_proj_ln_kernel, alpha=alpha, eps=eps),
        grid=(m // tm,),
        in_specs=[pl.BlockSpec((tm, k), lambda i: (i, 0)),
                  pl.BlockSpec((k, n), lambda i: (0, 0)),
                  pl.BlockSpec((tm, n), lambda i: (i, 0)), vec, vec],
        out_specs=pl.BlockSpec((tm, n), lambda i: (i, 0)),
        out_shape=jax.ShapeDtypeStruct((m, n), F32),
        compiler_params=pltpu.CompilerParams(
            dimension_semantics=("parallel",), vmem_limit_bytes=VMEM_LIMIT),
        name="out_proj_layernorm",
    )(mixed, w, x, g.reshape(1, n), b.reshape(1, n))


N_EXPERTS = 256
N_GROUPS = 8
TOPK_GROUPS = 4
TOP_K = 8
ROUTED_SCALE = 2.5


def _router_kernel(h_ref, wt_ref, bias_ref, idx_ref, wgt_ref):
    tm = h_ref.shape[0]
    per = N_EXPERTS // N_GROUPS
    logits = lax.dot_general(wt_ref[...], h_ref[...], (((1,), (1,)), ((), ())),
                             precision=lax.Precision.HIGHEST, preferred_element_type=F32)
    scores = _sigmoid(logits)
    biased = scores + bias_ref[...]
    neg = -jnp.inf

    grp = biased.reshape(N_GROUPS, per, tm)
    slot = lax.broadcasted_iota(jnp.int32, (N_GROUPS, per, tm), 1)
    top1 = jnp.max(grp, axis=1, keepdims=True)
    first = jnp.min(jnp.where(grp == top1, slot, per), axis=1, keepdims=True)
    top2 = jnp.max(jnp.where(slot == first, neg, grp), axis=1, keepdims=True)
    grp_score = (top1 + top2).reshape(N_GROUPS, tm)

    gid = lax.broadcasted_iota(jnp.int32, (N_GROUPS, tm), 0)
    rank = jnp.zeros((N_GROUPS, tm), jnp.int32)
    for m in range(N_GROUPS):
        s_m = grp_score[m:m + 1, :]
        rank = rank + jnp.where(s_m > grp_score, 1,
                                jnp.where(s_m == grp_score, jnp.where(gid > m, 1, 0), 0))
    keep = (rank < TOPK_GROUPS).reshape(N_GROUPS, 1, tm)
    masked = jnp.where(keep, grp, neg).reshape(N_EXPERTS, tm)

    eid = lax.broadcasted_iota(jnp.int32, (N_EXPERTS, tm), 0)
    idxs, wgts = [], []
    for _ in range(TOP_K):
        best = jnp.max(masked, axis=0, keepdims=True)
        pick = jnp.min(jnp.where(masked == best, eid, N_EXPERTS), axis=0, keepdims=True)
        hit = eid == pick
        idxs.append(pick)
        wgts.append(jnp.sum(jnp.where(hit, scores, 0.0), axis=0, keepdims=True))
        masked = jnp.where(hit, neg, masked)
    wgt = jnp.concatenate(wgts, axis=0)
    wgt = wgt / jnp.sum(wgt, axis=0, keepdims=True) * ROUTED_SCALE
    idx_ref[...] = jnp.concatenate(idxs, axis=0)
    wgt_ref[...] = wgt


def _router(h, w_router_t, bias, *, tm):
    t, d = h.shape
    out = pl.BlockSpec((TOP_K, tm), lambda i: (0, i))
    return pl.pallas_call(
        _router_kernel,
        grid=(t // tm,),
        in_specs=[pl.BlockSpec((tm, d), lambda i: (i, 0)),
                  pl.BlockSpec((N_EXPERTS, d), lambda i: (0, 0)),
                  pl.BlockSpec((N_EXPERTS, 1), lambda i: (0, 0))],
        out_specs=[out, out],
        out_shape=[jax.ShapeDtypeStruct((TOP_K, t), jnp.int32),
                   jax.ShapeDtypeStruct((TOP_K, t), F32)],
        compiler_params=pltpu.CompilerParams(
            dimension_semantics=("parallel",), vmem_limit_bytes=VMEM_LIMIT),
        name="moe_router",
    )(h, w_router_t, bias.reshape(N_EXPERTS, 1))


EXPERT_ROWS = 128


def _silu(x):
    return x * _sigmoid(x)


IDX_SLOTS = 4


def _experts_kernel(blk_e_ref, nxt_e_ref, n_used_ref,
                    codes_hbm, h_hbm, wg_hbm, wu_hbm, wd_hbm, ys_hbm,
                    idx_ref, x_buf, y_buf, wg_st, wu_st, wd_st, wg_bf, wu_bf, wd_bf,
                    idx_sem, gat_sem, sct_sem, w_sem, *, n_tokens):
    i = pl.program_id(0)
    n_blk = pl.num_programs(0)
    n_used = n_used_ref[0]
    used = i < n_used
    rows = EXPERT_ROWS
    slot_bits = TOP_K.bit_length() - 1

    def idx_copy(k):
        slot = k & (IDX_SLOTS - 1)
        return pltpu.make_async_copy(codes_hbm.at[pl.ds(k, 1), :], idx_ref.at[pl.ds(slot, 1), :],
                                     idx_sem.at[slot])

    def start_rows(k, gather):
        window = k & (IDX_SLOTS - 1)
        buf = k & 1
        for j in range(rows):
            code = idx_ref[window, j]
            if gather:
                tok = jnp.minimum(code >> slot_bits, n_tokens - 1)
                pltpu.make_async_copy(h_hbm.at[pl.ds(tok, 1), :], x_buf.at[buf, pl.ds(j, 1), :],
                                      gat_sem.at[buf]).start()
            else:
                pltpu.make_async_copy(y_buf.at[buf, pl.ds(j, 1), :], ys_hbm.at[pl.ds(code, 1), :],
                                      sct_sem.at[buf]).start()

    def wait_gather(k):
        pltpu.make_async_copy(h_hbm.at[pl.ds(0, rows), :], x_buf.at[k & 1], gat_sem.at[k & 1]).wait()

    def wait_scatter(k):
        pltpu.make_async_copy(y_buf.at[k & 1], ys_hbm.at[pl.ds(0, rows), :], sct_sem.at[k & 1]).wait()

    def weight_copies(e):
        return (pltpu.make_async_copy(wg_hbm.at[e], wg_st, w_sem.at[0]),
                pltpu.make_async_copy(wu_hbm.at[e], wu_st, w_sem.at[1]),
                pltpu.make_async_copy(wd_hbm.at[e], wd_st, w_sem.at[2]))

    def compute(k):
        x = x_buf[k & 1].astype(BF16)
        hid = _silu(jnp.dot(x, wg_bf[...], preferred_element_type=F32)) * jnp.dot(
            x, wu_bf[...], preferred_element_type=F32)
        y_buf[k & 1] = jnp.dot(hid.astype(BF16), wd_bf[...], preferred_element_type=F32)

    @pl.when(i == 0)
    def _():
        for cp in weight_copies(blk_e_ref[0]):
            cp.start()
        idx_copy(0).start()
        idx_copy(0).wait()
        start_rows(0, gather=True)

        @pl.when(1 < n_used)
        def _():
            idx_copy(1).start()

    @pl.when(i + 2 < n_used)
    def _():
        idx_copy(i + 2).start()

    @pl.when(i + 1 < n_used)
    def _():
        idx_copy(i + 1).wait()

    prev_e = blk_e_ref[jnp.maximum(i - 1, 0)]
    fresh = jnp.logical_or(i == 0, blk_e_ref[i] != prev_e)

    @pl.when(jnp.logical_and(used, fresh))
    def _():
        for cp in weight_copies(blk_e_ref[i]):
            cp.wait()
        wg_bf[...] = wg_st[...].astype(BF16)
        wu_bf[...] = wu_st[...].astype(BF16)
        wd_bf[...] = wd_st[...].astype(BF16)

        @pl.when(nxt_e_ref[i] >= 0)
        def _():
            for cp in weight_copies(nxt_e_ref[i]):
                cp.start()

    @pl.when(jnp.logical_and(i >= 2, i - 2 < n_used))
    def _():
        wait_scatter(i - 2)

    has_next = i + 1 < n_used
    interior = jnp.logical_and(i >= 1, has_next)

    @pl.when(interior)
    def _():
        wait_gather(i)
        start_rows(i - 1, gather=False)
        start_rows(i + 1, gather=True)
        compute(i)

    @pl.when(jnp.logical_and(used, jnp.logical_not(interior)))
    def _():
        @pl.when(has_next)
        def _():
            start_rows(i + 1, gather=True)

        @pl.when(i >= 1)
        def _():
            start_rows(i - 1, gather=False)

        wait_gather(i)
        compute(i)

        @pl.when(jnp.logical_not(has_next))
        def _():
            start_rows(i, gather=False)

    @pl.when(i == n_blk - 1)
    def _():
        @pl.when(jnp.logical_and(i >= 1, i - 1 < n_used))
        def _():
            wait_scatter(i - 1)

        @pl.when(used)
        def _():
            wait_scatter(i)

        y_buf[...] = jnp.zeros_like(y_buf)
        for half in range(2):
            clear = pltpu.make_async_copy(
                y_buf.at[half], ys_hbm.at[pl.ds(n_tokens * TOP_K + half * rows, rows), :],
                sct_sem.at[half])
            clear.start()
            clear.wait()


def _experts(h, codes, blk_e, nxt_e, n_used, w_gate, w_up, w_down):
    t, d = h.shape
    ff = w_gate.shape[2]
    n_blk = blk_e.shape[0]
    any_spec = pl.BlockSpec(memory_space=pl.ANY)
    grid_spec = pltpu.PrefetchScalarGridSpec(
        num_scalar_prefetch=3,
        grid=(n_blk,),
        in_specs=[any_spec] * 5,
        out_specs=any_spec,
        scratch_shapes=[pltpu.SMEM((IDX_SLOTS, EXPERT_ROWS), jnp.int32),
                        pltpu.VMEM((2, EXPERT_ROWS, d), F32), pltpu.VMEM((2, EXPERT_ROWS, d), F32),
                        pltpu.VMEM((d, ff), F32), pltpu.VMEM((d, ff), F32), pltpu.VMEM((ff, d), F32),
                        pltpu.VMEM((d, ff), BF16), pltpu.VMEM((d, ff), BF16),
                        pltpu.VMEM((ff, d), BF16),
                        pltpu.SemaphoreType.DMA((IDX_SLOTS,)), pltpu.SemaphoreType.DMA((2,)),
                        pltpu.SemaphoreType.DMA((2,)), pltpu.SemaphoreType.DMA((3,))])
    return pl.pallas_call(
        functools.partial(_experts_kernel, n_tokens=t),
        grid_spec=grid_spec,
        out_shape=jax.ShapeDtypeStruct((t * TOP_K + 2 * EXPERT_ROWS, d), F32),
        compiler_params=pltpu.CompilerParams(
            dimension_semantics=("arbitrary",), vmem_limit_bytes=VMEM_LIMIT),
        name="routed_experts",
    )(blk_e, nxt_e, n_used, codes, h, w_gate, w_up, w_down)


def _shared_ln_kernel(h_ref, ys_ref, tw_ref, wg_ref, wu_ref, wd_ref, g_ref, b_ref, o_ref, *,
                      alpha, eps):
    h = h_ref[...]
    d = h.shape[1]
    hb = h.astype(BF16)
    hid = _silu(jnp.dot(hb, wg_ref[...], preferred_element_type=F32)) * jnp.dot(
        hb, wu_ref[...], preferred_element_type=F32)
    moe = jnp.dot(hid.astype(BF16), wd_ref[...], preferred_element_type=F32)
    for k in range(TOP_K):
        moe = moe + ys_ref[:, k * d:(k + 1) * d] * tw_ref[:, k:k + 1]
    o_ref[...] = _layer_norm(alpha * h + moe, g_ref[...], b_ref[...], eps)


def _shared_ln(h, ys, top_w, w_gate, w_up, w_down, g, b, *, tm, alpha, eps):
    t, d = h.shape
    ff = w_gate.shape[1]
    row = pl.BlockSpec((tm, d), lambda i: (i, 0))
    vec = pl.BlockSpec((1, d), lambda i: (0, 0))
    return pl.pallas_call(
        functools.partial(_shared_ln_kernel, alpha=alpha, eps=eps),
        grid=(t // tm,),
        in_specs=[row, pl.BlockSpec((tm, TOP_K * d), lambda i: (i, 0)),
                  pl.BlockSpec((tm, TOP_K), lambda i: (i, 0)),
                  pl.BlockSpec((d, ff), lambda i: (0, 0)), pl.BlockSpec((d, ff), lambda i: (0, 0)),
                  pl.BlockSpec((ff, d), lambda i: (0, 0)), vec, vec],
        out_specs=row,
        out_shape=jax.ShapeDtypeStruct((t, d), F32),
        compiler_params=pltpu.CompilerParams(
            dimension_semantics=("parallel",), vmem_limit_bytes=VMEM_LIMIT),
        name="shared_expert_layernorm",
    )(h, ys, top_w, w_gate, w_up, w_down, g.reshape(1, d), b.reshape(1, d))


RWKV_GN_EPS = 64e-5
LN_EPS = 1e-5
ROPE_THETA = 10000.0


def _rope_tables(seq, dim):
    inv = 1.0 / (ROPE_THETA ** (jnp.arange(0, dim, 2, dtype=F32) / dim))
    ang = jnp.arange(seq, dtype=F32)[:, None] * inv[None, :]
    ang = jnp.concatenate([ang, ang], axis=-1)
    sign = jnp.where(jnp.arange(dim) < dim // 2, -1.0, 1.0).astype(F32)
    return jnp.cos(ang), jnp.sin(ang) * sign


def _rwkv_branch(u_rkv, u_lora, shift_mu, w_lora_up, w0, a_lora_up, a0, g_lora_up, k_k, k_a, r_k,
                 ln_x_w, ln_x_b, *, batch, seq):
    c = u_rkv.shape[1] // 3
    heads = c // WKV_HEAD
    z = jnp.concatenate([u_rkv, u_lora], axis=1).reshape(batch, seq, -1)
    z_prev = jnp.pad(z, ((0, 0), (1, 0), (0, 0)))[:, :-1]
    z = (z + (z_prev - z) * shift_mu).reshape(batch * seq, -1)
    d_decay, d_aaa = w_lora_up.shape[0], a_lora_up.shape[0]
    cuts = [c, 2 * c, 3 * c, 3 * c + d_decay, 3 * c + d_decay + d_aaa]
    r, k, v, zw, za, zg = jnp.split(z, cuts, axis=-1)
    w = -jax.nn.softplus(-(w0 + jnp.tanh(zw) @ w_lora_up)) - 0.5
    a = jax.nn.sigmoid(a0 + za @ a_lora_up)
    g = jax.nn.sigmoid(zg) @ g_lora_up
    hd = lambda t: t.reshape(batch * seq, heads, WKV_HEAD)
    kk = hd(k * k_k)
    kk = (kk / jnp.maximum(jnp.linalg.norm(kk, axis=-1, keepdims=True), 1e-12)).reshape(-1, c)
    k = k * (1.0 + (a - 1.0) * k_a)
    y = _wkv(r, -jnp.exp(w), k, v, -kk, kk * a, batch=batch, seq=seq)
    y = hd(y)
    mu = y.mean(-1, keepdims=True)
    var = jnp.square(y - mu).mean(-1, keepdims=True)
    y = ((y - mu) * lax.rsqrt(var + RWKV_GN_EPS)).reshape(-1, c) * ln_x_w + ln_x_b
    bonus = ((hd(r) * hd(k) * r_k).sum(-1, keepdims=True) * hd(v)).reshape(-1, c)
    return ((y + bonus) * g).astype(BF16)


def _dispatch(top_idx, n_tokens):
    n_assign = n_tokens * TOP_K
    i32 = jnp.int32
    e_flat = top_idx.T.reshape(n_assign).astype(i32)
    sorted_e, order = lax.sort((e_flat, jnp.arange(n_assign, dtype=i32)), num_keys=1, is_stable=True)
    experts = jnp.arange(N_EXPERTS, dtype=i32)
    grp_start = jnp.searchsorted(sorted_e, experts, side='left').astype(i32)
    grp_end = jnp.searchsorted(sorted_e, experts, side='right').astype(i32)
    counts = grp_end - grp_start
    padded = (counts + EXPERT_ROWS - 1) // EXPERT_ROWS * EXPERT_ROWS
    pad_end = jnp.cumsum(padded)
    pad_start = pad_end - padded
    n_blk = -(-n_assign // EXPERT_ROWS) + N_EXPERTS
    n_used = (pad_end[-1] // EXPERT_ROWS).astype(i32)
    first_row = jnp.arange(n_blk, dtype=i32) * EXPERT_ROWS
    blk_e = jnp.minimum(jnp.searchsorted(pad_end, first_row, side='right'), N_EXPERTS - 1).astype(i32)
    into = first_row - pad_start[blk_e]
    blk_s0 = jnp.clip(grp_start[blk_e] + into, 0, n_assign)
    blk_nv = jnp.clip(counts[blk_e] - into, 0, EXPERT_ROWS)
    later = jnp.where(counts > 0, experts, N_EXPERTS)
    nxt = lax.cummin(jnp.concatenate([later[1:], jnp.full((1,), N_EXPERTS, i32)]), reverse=True)
    nxt_e = jnp.where(nxt < N_EXPERTS, nxt, -1)[blk_e]
    row = jnp.arange(EXPERT_ROWS, dtype=i32)[None, :]
    src = jnp.minimum(blk_s0[:, None] + row, n_assign - 1)
    pad = n_assign + (jnp.arange(n_blk, dtype=i32)[:, None] % 2) * EXPERT_ROWS + row
    codes = jnp.where(row < blk_nv[:, None], order[src], pad).astype(i32)
    return codes, blk_e, nxt_e.astype(i32), n_used.reshape(1)


def kernel(x, w_in, shift_mu, w_lora_up, w0, a_lora_up, a0, g_lora_up, k_k, k_a, r_k, ln_x_w, ln_x_b,
           w_o_rwkv, w_o_moba, w_out, ln1_g, ln1_b, w_router, router_bias, w_gate_e, w_up_e,
           w_down_e, w_gate_s, w_up_s, w_down_s, ln2_g, ln2_b):
    batch, seq, d = x.shape
    depth = w_in.shape[0]
    alpha = (2 * depth) ** 0.25
    t = batch * seq
    c = d
    shift_w = shift_mu.shape[1]
    moba_heads = d // MOBA_HEAD
    cos, sin_signed = _rope_tables(seq, MOBA_HEAD)

    h = x.reshape(t, d)
    for layer in range(depth):
        wl = w_in[layer].astype(BF16)
        hb = h.astype(BF16)
        n_lora = shift_w - 3 * c
        lora_pad = -n_lora % LANES
        w_lora = jnp.pad(wl[:, 3 * c:shift_w], ((0, 0), (0, lora_pad)))
        u_rkv = _matmul(hb, wl[:, :3 * c], tm=1024, tn=1024)
        u_lora = _matmul(hb, w_lora, tm=1024, tn=n_lora + lora_pad)[:, :n_lora]
        u_qkv = _matmul(hb, wl[:, shift_w:shift_w + 3 * d], tm=1024, tn=1024)
        u_gate = _matmul(hb, wl[:, shift_w + 3 * d:], tm=1024, tn=1024)

        o_rwkv = _rwkv_branch(u_rkv, u_lora, shift_mu[layer], w_lora_up[layer], w0[layer],
                              a_lora_up[layer], a0[layer], g_lora_up[layer], k_k[layer], k_a[layer],
                              r_k[layer], ln_x_w[layer], ln_x_b[layer],
                              batch=batch, seq=seq)
        o_moba = _moba(u_qkv, cos, sin_signed, batch=batch, seq=seq, heads=moba_heads)

        mixed = _merge(o_rwkv, o_moba, w_o_rwkv[layer].astype(BF16), w_o_moba[layer].astype(BF16),
                       u_gate, tm=512, tn=1024)
        h = _proj_ln(mixed, w_out[layer].astype(BF16), h, ln1_g[layer], ln1_b[layer],
                     tm=512, alpha=alpha, eps=LN_EPS)

        top_idx, top_w = _router(h, w_router[layer].T, router_bias[layer], tm=512)
        codes, blk_e, nxt_e, n_used = _dispatch(top_idx, t)
        ys = _experts(h, codes, blk_e, nxt_e, n_used,
                      w_gate_e[layer], w_up_e[layer], w_down_e[layer])
        h = _shared_ln(h, ys.reshape(-1, TOP_K * d), top_w.T, w_gate_s[layer].astype(BF16),
                       w_up_s[layer].astype(BF16), w_down_s[layer].astype(BF16), ln2_g[layer],
                       ln2_b[layer], tm=128, alpha=alpha, eps=LN_EPS)
    return h.reshape(batch, seq, d)
```

```python
import functools

import jax
import jax.numpy as jnp
from jax import lax
from jax.experimental import pallas as pl
from jax.experimental.pallas import tpu as pltpu

F32 = jnp.float32
BF16 = jnp.bfloat16

LANES = 128
WKV_HEAD = 64
WKV_CHUNK = 64


def _dot(a, b):
    return jnp.dot(a.astype(BF16), b.astype(BF16), preferred_element_type=F32)


def _dot_nt(a, b):
    return lax.dot_general(a.astype(BF16), b.astype(BF16), (((1,), (1,)), ((), ())),
                           preferred_element_type=F32)


assert WKV_CHUNK == WKV_HEAD
def _each(f, *lists):
    return [f(*xs) for xs in zip(*lists)]


def _wkv_pairs(r, ld, k, v, a, b, m_state):
    c = WKV_CHUNK
    cat0 = lambda *xs: jnp.concatenate(xs, axis=0)
    cat1 = lambda *xs: jnp.concatenate(xs, axis=1)
    lane = lax.broadcasted_iota(jnp.int32, (c, LANES), 1)
    step = lax.broadcasted_iota(jnp.int32, (c, LANES), 0)
    head0 = lane < WKV_HEAD
    other = jnp.where(head0, lane, lane - WKV_HEAD)

    def stack(x):
        x = x.astype(BF16)
        zero = jnp.zeros_like(x)
        return cat0(jnp.where(head0, x, zero), jnp.where(head0, zero, x))

    def cumsum_rows(x):
        shift = 1
        while shift < c:
            x = x + jnp.where(step >= shift, pltpu.roll(x, shift, axis=0), 0.0)
            shift *= 2
        return x

    dot = lambda p, q: jnp.dot(p.astype(BF16), q, preferred_element_type=F32)

    cw = _each(cumsum_rows, ld)
    e_out = _each(lambda x: jnp.exp(-x), cw)
    a_t = _each(lambda x, w, l: x * jnp.exp(w - l), a, cw, ld)
    r_t = _each(lambda x, w: x * jnp.exp(w), r, cw)
    bk_st = _each(lambda p, q, e: cat0(stack(p * e), stack(q * e)), b, k, e_out)
    v_st = _each(stack, v)
    aa = _each(lambda p, q, s: _dot_nt(cat0(p, q), s), a_t, r_t, bk_st)

    strict = step > other
    incl = step >= other
    n_ab = _each(lambda x: jnp.where(strict, x[:c, :LANES], 0.0), aa)
    a_ak = _each(lambda x: jnp.where(strict, x[:c, LANES:], 0.0), aa)
    a_rb = _each(lambda x: jnp.where(incl, x[c:, :LANES], 0.0), aa)
    a_rk = _each(lambda x: jnp.where(incl, x[c:, LANES:], 0.0), aa)

    eye = jnp.where(step == other, 1.0, 0.0)
    t_inv = _each(lambda n: eye + n, n_ab)
    w1 = _each(dot, a_ak, v_st)
    n_pow = _each(lambda n: dot(n, stack(n)), n_ab)
    for _ in range(c.bit_length() - 3):
        pt = _each(lambda n, t: dot(n, cat1(stack(n), stack(t))), n_pow, t_inv)
        n_pow = _each(lambda x: x[:, :LANES], pt)
        t_inv = _each(lambda t, x: t + x[:, LANES:], t_inv, pt)
    t_inv = _each(lambda t, n: t + dot(n, stack(t)), t_inv, n_pow)

    tw = _each(lambda t, w, x: dot(t, cat1(stack(w), stack(x))), t_inv, w1, a_t)
    rhs = _each(lambda x, v_: cat0(cat1(stack(x[:, LANES:]), stack(x[:, :LANES])),
                                   cat1(jnp.zeros_like(v_), v_)), tw, v_st)
    out_p = _each(lambda p, q, s: dot(cat1(p, q), s), a_rb, a_rk, rhs)
    tail = _each(lambda w: jnp.exp(w[c - 1:c, :] - w), cw)
    bkp_st = _each(lambda p, q, e: cat0(stack(p * e), stack(q * e)), b, k, tail)
    upd = _each(lambda p, s: lax.dot_general(p, s, (((0,), (0,)), ((), ())),
                                             preferred_element_type=F32), bkp_st, rhs)
    upd_p = _each(lambda u: u[:WKV_HEAD] + u[WKV_HEAD:], upd)
    r_hat = _each(lambda x, o: x + o[:, :LANES], r_t, out_p)
    g = _each(lambda w, u: jnp.where(step == other, jnp.exp(w[c - 1:c, :]), 0.0) + u[:, :LANES],
              cw, upd_p)

    def apply_state(lhs, m):
        l_hi = lhs.astype(BF16)
        l_lo = (lhs - l_hi.astype(F32)).astype(BF16)
        m_hi = m.astype(BF16)
        m_lo = m - m_hi.astype(F32)
        both = jnp.dot(l_hi, cat1(stack(m_hi), stack(m_lo)), preferred_element_type=F32)
        return (both[:, :LANES] + both[:, LANES:]) + jnp.dot(l_lo, stack(m_hi),
                                                             preferred_element_type=F32)

    rm = _each(lambda p, q, m: apply_state(cat0(p, q), m), r_hat, g, m_state)
    y = _each(lambda x, o: x[:c] + o[:, LANES:], rm, out_p)
    m_new = _each(lambda x, u: x[c:] + u[:, LANES:], rm, upd_p)
    return y, m_new


RWKV_GN_EPS = 64e-5
LORA_PAD = LANES


def _rwkv_kernel(u_ref, ul_ref, mu_ref, mul_ref, wl_ref, w0_ref, al_ref, a0_ref, gl_ref, kk_ref,
                 ka_ref, rk_ref, lnw_ref, lnb_ref, o_ref, m_ref, prev_ref, prevl_ref):
    c = WKV_CHUNK
    width = o_ref.shape[1]

    @pl.when(pl.program_id(1) == 0)
    def _():
        m_ref[...] = jnp.zeros_like(m_ref)
        prev_ref[...] = jnp.zeros_like(prev_ref)
        prevl_ref[...] = jnp.zeros_like(prevl_ref)

    def shifted(u, prev, mu):
        first = lax.broadcasted_iota(jnp.int32, u.shape, 0) == 0
        before = jnp.where(first, prev[7:8, :], pltpu.roll(u, 1, axis=0))
        return u + (before - u) * mu

    u = u_ref[...]
    ul = ul_ref[...]
    z = shifted(u, prev_ref[...], mu_ref[...])
    zl = shifted(ul, prevl_ref[...], mul_ref[...])
    prev_ref[...] = u[c - 8:, :]
    prevl_ref[...] = ul[c - 8:, :]

    r = z[:, :width]
    k_raw = z[:, width:2 * width]
    v = z[:, 2 * width:]
    w = w0_ref[...] + _dot(jnp.tanh(zl[:, :LORA_PAD]), wl_ref[...])
    ld = -jnp.exp(-(jnp.maximum(-w, 0.0) + jnp.log(1.0 + jnp.exp(-jnp.abs(w)))) - 0.5)
    a = _sigmoid(a0_ref[...] + _dot(zl[:, LORA_PAD:2 * LORA_PAD], al_ref[...]))
    g = _dot(_sigmoid(zl[:, 2 * LORA_PAD:]), gl_ref[...])
    kk_raw = k_raw * kk_ref[...]
    k = k_raw * (1.0 + (a - 1.0) * ka_ref[...])

    lanes = [slice(p * LANES, (p + 1) * LANES) for p in range(width // LANES)]
    split = lambda x: [x[:, sl] for sl in lanes]
    row = lax.broadcasted_iota(jnp.int32, (LANES, LANES), 0)
    col = lax.broadcasted_iota(jnp.int32, (LANES, LANES), 1)
    same_head = jnp.where((row < WKV_HEAD) == (col < WKV_HEAD), 1.0, 0.0).astype(BF16)

    def head_sums(x):
        stacked = jnp.concatenate(split(x), axis=0).astype(BF16)
        s = jnp.dot(stacked, same_head, preferred_element_type=F32)
        return [s[p * c:(p + 1) * c] for p in range(len(lanes))]

    norm2 = head_sums(kk_raw * kk_raw)
    kk = _each(lambda x, s: x * lax.rsqrt(jnp.maximum(s, 1e-24)), split(kk_raw), norm2)
    bonus = head_sums(r * k * rk_ref[...])

    m_state = [m_ref[p] for p in range(len(lanes))]
    y, m_new = _wkv_pairs(split(r), split(ld), split(k), split(v), _each(lambda x: -x, kk),
                          _each(lambda x, a_: x * a_, kk, split(a)), m_state)
    for p in range(len(lanes)):
        m_ref[p] = m_new[p]

    inv_n = 1.0 / WKV_HEAD
    y = jnp.concatenate(y, axis=1)
    mean = jnp.concatenate(head_sums(y), axis=1) * inv_n
    yc = y - mean
    var = jnp.concatenate(head_sums(yc * yc), axis=1) * inv_n
    yn = yc * lax.rsqrt(var + RWKV_GN_EPS) * lnw_ref[...] + lnb_ref[...]
    out = (yn + jnp.concatenate(bonus, axis=1) * v) * g
    o_ref[...] = out.astype(o_ref.dtype)


def _rwkv_mix(u_rkv, u_lora, mu_rkv, mu_lora, w_lora, w0, a_lora, a0, g_lora, k_k, k_a, r_k,
              ln_w, ln_b, *, batch, seq):
    t = u_rkv.shape[0]
    width = u_rkv.shape[1] // 3
    n_lora = u_lora.shape[1]
    assert t == batch * seq and seq % WKV_CHUNK == 0 and width % LANES == 0
    n_chunks = seq // WKV_CHUNK
    rows = lambda n: pl.BlockSpec((WKV_CHUNK, n), lambda bi, ci: (bi * n_chunks + ci, 0))
    whole = lambda x: pl.BlockSpec(x.shape, lambda bi, ci: (0,) * x.ndim)
    vec = lambda x: x.reshape(1, -1)
    params = [vec(mu_rkv), vec(mu_lora), w_lora.astype(BF16), vec(w0), a_lora.astype(BF16), vec(a0),
              g_lora.astype(BF16), vec(k_k), vec(k_a), vec(r_k), vec(ln_w), vec(ln_b)]
    return pl.pallas_call(
        _rwkv_kernel,
        grid=(batch, n_chunks),
        in_specs=[rows(3 * width), rows(n_lora)] + [whole(p) for p in params],
        out_specs=rows(width),
        out_shape=jax.ShapeDtypeStruct((t, width), BF16),
        scratch_shapes=[pltpu.VMEM((width // LANES, WKV_HEAD, LANES), F32),
                        pltpu.VMEM((8, 3 * width), F32), pltpu.VMEM((8, n_lora), F32)],
        compiler_params=pltpu.CompilerParams(
            dimension_semantics=("parallel", "arbitrary"), vmem_limit_bytes=VMEM_LIMIT),
        name="rwkv7_time_mix",
    )(u_rkv, u_lora, *params)


MOBA_BLOCK = 256
MOBA_TOPK = 3
MOBA_HEAD = 128


def _rope(t, cos, sin_signed):
    return t * cos + pltpu.roll(t, MOBA_HEAD // 2, axis=1) * sin_signed


def _moba_kernel(q_ref, k_ref, v_ref, cos_ref, sin_ref, o_ref, kr_ref, vb_ref, kmean_ref, *,
                 nb, scale):
    blk = MOBA_BLOCK
    kmean_ref[...] = jnp.zeros_like(kmean_ref)
    for n in range(nb):
        rows = slice(n * blk, (n + 1) * blk)
        kr = _rope(k_ref[rows, :], cos_ref[rows, :], sin_ref[rows, :])
        kr_ref[rows, :] = kr.astype(BF16)
        vb_ref[rows, :] = v_ref[rows, :].astype(BF16)
        kmean_ref[n:n + 1, :] = jnp.mean(kr, axis=0, keepdims=True)

    lane = lax.broadcasted_iota(jnp.int32, (blk, LANES), 1)
    row = lax.broadcasted_iota(jnp.int32, (blk, blk), 0)
    col = lax.broadcasted_iota(jnp.int32, (blk, blk), 1)
    causal = row >= col
    for i in range(nb):
        rows = slice(i * blk, (i + 1) * blk)
        q = _rope(q_ref[rows, :], cos_ref[rows, :], sin_ref[rows, :])
        width = (i + 1) * blk
        s = _dot_nt(q, kr_ref[:width, :]) * scale
        parts = []
        if i > MOBA_TOPK:
            gate = lax.dot_general(q, kmean_ref[...], (((1,), (1,)), ((), ())),
                                   precision=lax.Precision.HIGHEST, preferred_element_type=F32)
            gate = jnp.where(lane < i, gate, -jnp.inf)
            rank = jnp.zeros((blk, LANES), jnp.int32)
            for m in range(i):
                g_m = gate[:, m:m + 1]
                rank = rank + jnp.where(
                    g_m > gate, 1, jnp.where(g_m == gate, jnp.where(lane > m, 1, 0), 0))
            chosen = jnp.where(rank < MOBA_TOPK, 1.0, 0.0)
            for n in range(i):
                take = chosen[:, n:n + 1] > 0.5
                parts.append(jnp.where(take, s[:, n * blk:(n + 1) * blk], -jnp.inf))
        else:
            parts = [s[:, n * blk:(n + 1) * blk] for n in range(i)]
        parts.append(jnp.where(causal, s[:, i * blk:], -jnp.inf))
        s = jnp.concatenate(parts, axis=1) if i else parts[0]
        m_row = jnp.max(s, axis=1, keepdims=True)
        p = jnp.exp(s - m_row)
        l_row = jnp.sum(p, axis=1, keepdims=True)
        acc = jnp.dot(p.astype(BF16), vb_ref[:width, :], preferred_element_type=F32)
        o_ref[rows, :] = (acc / l_row).astype(o_ref.dtype)


def _moba(u_qkv, cos, sin_signed, *, batch, seq, heads):
    t = u_qkv.shape[0]
    assert seq % MOBA_BLOCK == 0
    nb = seq // MOBA_BLOCK
    assert nb <= LANES
    kernel = functools.partial(_moba_kernel, nb=nb, scale=MOBA_HEAD ** -0.5)
    full = lambda off: pl.BlockSpec((seq, MOBA_HEAD), lambda bi, hi: (bi, off + hi))
    table = pl.BlockSpec((seq, MOBA_HEAD), lambda bi, hi: (0, 0))
    return pl.pallas_call(
        kernel,
        grid=(batch, heads),
        in_specs=[full(0), full(heads), full(2 * heads), table, table],
        out_specs=full(0),
        out_shape=jax.ShapeDtypeStruct((t, heads * MOBA_HEAD), BF16),
        scratch_shapes=[pltpu.VMEM((seq, MOBA_HEAD), BF16), pltpu.VMEM((seq, MOBA_HEAD), BF16),
                        pltpu.VMEM((LANES, MOBA_HEAD), F32)],
        compiler_params=pltpu.CompilerParams(dimension_semantics=("parallel", "parallel")),
        name="moba_attention",
    )(u_qkv, u_qkv, u_qkv, cos, sin_signed)


VMEM_LIMIT = 56 * 1024 * 1024


def _matmul_kernel(x_ref, w_ref, o_ref):
    o_ref[...] = jnp.dot(x_ref[...], w_ref[...], preferred_element_type=F32).astype(o_ref.dtype)


def _matmul(x, w, *, tm, tn, out_dtype=F32):
    m, k = x.shape
    n = w.shape[1]
    assert m % tm == 0 and n % tn == 0
    return pl.pallas_call(
        _matmul_kernel,
        grid=(m // tm, n // tn),
        in_specs=[pl.BlockSpec((tm, k), lambda i, j: (i, 0)),
                  pl.BlockSpec((k, tn), lambda i, j: (0, j))],
        out_specs=pl.BlockSpec((tm, tn), lambda i, j: (i, j)),
        out_shape=jax.ShapeDtypeStruct((m, n), out_dtype),
        compiler_params=pltpu.CompilerParams(
            dimension_semantics=("parallel", "parallel"), vmem_limit_bytes=VMEM_LIMIT),
        name="matmul",
    )(x, w)


def _sigmoid(x):
    return 1.0 / (1.0 + jnp.exp(-x))


def _merge_kernel(a_ref, b_ref, pa_ref, pb_ref, ga_ref, gb_ref, o_ref):
    ya = jnp.dot(a_ref[...], pa_ref[...], preferred_element_type=F32)
    yb = jnp.dot(b_ref[...], pb_ref[...], preferred_element_type=F32)
    o_ref[...] = (_sigmoid(ga_ref[...]) * ya + _sigmoid(gb_ref[...]) * yb).astype(o_ref.dtype)


def _merge(o_a, o_b, p_a, p_b, gates, *, tm, tn):
    m, k = o_a.shape
    n = p_a.shape[1]
    nj = n // tn
    row = pl.BlockSpec((tm, k), lambda i, j: (i, 0))
    wcol = pl.BlockSpec((k, tn), lambda i, j: (0, j))
    return pl.pallas_call(
        _merge_kernel,
        grid=(m // tm, nj),
        in_specs=[row, row, wcol, wcol,
                  pl.BlockSpec((tm, tn), lambda i, j: (i, j)),
                  pl.BlockSpec((tm, tn), lambda i, j: (i, nj + j))],
        out_specs=pl.BlockSpec((tm, tn), lambda i, j: (i, j)),
        out_shape=jax.ShapeDtypeStruct((m, n), BF16),
        compiler_params=pltpu.CompilerParams(
            dimension_semantics=("parallel", "parallel"), vmem_limit_bytes=VMEM_LIMIT),
        name="gated_merge",
    )(o_a, o_b, p_a, p_b, gates, gates)


def _layer_norm(x, g, b, eps):
    mu = jnp.mean(x, axis=-1, keepdims=True)
    xc = x - mu
    var = jnp.mean(xc * xc, axis=-1, keepdims=True)
    return xc * lax.rsqrt(var + eps) * g + b


def _proj_ln_kernel(m_ref, w_ref, x_ref, g_ref, b_ref, o_ref, *, alpha, eps):
    y = jnp.dot(m_ref[...], w_ref[...], preferred_element_type=F32)
    o_ref[...] = _layer_norm(alpha * x_ref[...] + y, g_ref[...], b_ref[...], eps)


def _proj_ln(mixed, w, x, g, b, *, tm, alpha, eps):
    m, k = mixed.shape
    n = w.shape[1]
    vec = pl.BlockSpec((1, n), lambda i: (0, 0))
    return pl.pallas_call(
        functools.partial(_proj_ln_kernel, alpha=alpha, eps=eps),
        grid=(m // tm,),
        in_specs=[pl.BlockSpec((tm, k), lambda i: (i, 0)),
                  pl.BlockSpec((k, n), lambda i: (0, 0)),
                  pl.BlockSpec((tm, n), lambda i: (i, 0)), vec, vec],
        out_specs=pl.BlockSpec((tm, n), lambda i: (i, 0)),
        out_shape=jax.ShapeDtypeStruct((m, n), F32),
        compiler_params=pltpu.CompilerParams(
            dimension_semantics=("parallel",), vmem_limit_bytes=VMEM_LIMIT),
        name="out_proj_layernorm",
    )(mixed, w, x, g.reshape(1, n), b.reshape(1, n))


N_EXPERTS = 256
N_GROUPS = 8
TOPK_GROUPS = 4
TOP_K = 8
ROUTED_SCALE = 2.5


def _router_kernel(h_ref, wt_ref, bias_ref, idx_ref, wgt_ref):
    tm = h_ref.shape[0]
    per = N_EXPERTS // N_GROUPS
    logits = lax.dot_general(wt_ref[...], h_ref[...], (((1,), (1,)), ((), ())),
                             precision=lax.Precision.HIGHEST, preferred_element_type=F32)
    scores = _sigmoid(logits)
    biased = scores + bias_ref[...]
    neg = -jnp.inf

    grp = biased.reshape(N_GROUPS, per, tm)
    slot = lax.broadcasted_iota(jnp.int32, (N_GROUPS, per, tm), 1)
    top1 = jnp.max(grp, axis=1, keepdims=True)
    first = jnp.min(jnp.where(grp == top1, slot, per), axis=1, keepdims=True)
    top2 = jnp.max(jnp.where(slot == first, neg, grp), axis=1, keepdims=True)
    grp_score = (top1 + top2).reshape(N_GROUPS, tm)

    gid = lax.broadcasted_iota(jnp.int32, (N_GROUPS, tm), 0)
    rank = jnp.zeros((N_GROUPS, tm), jnp.int32)
    for m in range(N_GROUPS):
        s_m = grp_score[m:m + 1, :]
        rank = rank + jnp.where(s_m > grp_score, 1,
                                jnp.where(s_m == grp_score, jnp.where(gid > m, 1, 0), 0))
    keep = (rank < TOPK_GROUPS).reshape(N_GROUPS, 1, tm)
    masked = jnp.where(keep, grp, neg).reshape(N_EXPERTS, tm)

    eid = lax.broadcasted_iota(jnp.int32, (N_EXPERTS, tm), 0)
    idxs, wgts = [], []
    for _ in range(TOP_K):
        best = jnp.max(masked, axis=0, keepdims=True)
        pick = jnp.min(jnp.where(masked == best, eid, N_EXPERTS), axis=0, keepdims=True)
        hit = eid == pick
        idxs.append(pick)
        wgts.append(jnp.sum(jnp.where(hit, scores, 0.0), axis=0, keepdims=True))
        masked = jnp.where(hit, neg, masked)
    wgt = jnp.concatenate(wgts, axis=0)
    wgt = wgt / jnp.sum(wgt, axis=0, keepdims=True) * ROUTED_SCALE
    idx_ref[...] = jnp.concatenate(idxs, axis=0)
    wgt_ref[...] = wgt


def _router(h, w_router_t, bias, *, tm):
    t, d = h.shape
    out = pl.BlockSpec((TOP_K, tm), lambda i: (0, i))
    return pl.pallas_call(
        _router_kernel,
        grid=(t // tm,),
        in_specs=[pl.BlockSpec((tm, d), lambda i: (i, 0)),
                  pl.BlockSpec((N_EXPERTS, d), lambda i: (0, 0)),
                  pl.BlockSpec((N_EXPERTS, 1), lambda i: (0, 0))],
        out_specs=[out, out],
        out_shape=[jax.ShapeDtypeStruct((TOP_K, t), jnp.int32),
                   jax.ShapeDtypeStruct((TOP_K, t), F32)],
        compiler_params=pltpu.CompilerParams(
            dimension_semantics=("parallel",), vmem_limit_bytes=VMEM_LIMIT),
        name="moe_router",
    )(h, w_router_t, bias.reshape(N_EXPERTS, 1))


EXPERT_ROWS = 128


def _silu(x):
    return x * _sigmoid(x)


IDX_SLOTS = 4


def _experts_kernel(blk_e_ref, nxt_e_ref, n_used_ref,
                    codes_hbm, h_hbm, wg_hbm, wu_hbm, wd_hbm, ys_hbm,
                    idx_ref, x_buf, y_buf, wg_st, wu_st, wd_st, wg_bf, wu_bf, wd_bf,
                    idx_sem, gat_sem, sct_sem, w_sem, *, n_tokens):
    i = pl.program_id(0)
    n_blk = pl.num_programs(0)
    n_used = n_used_ref[0]
    used = i < n_used
    rows = EXPERT_ROWS

    def idx_copy(k):
        slot = k & (IDX_SLOTS - 1)
        return pltpu.make_async_copy(codes_hbm.at[pl.ds(k, 1), :], idx_ref.at[pl.ds(slot, 1), :],
                                     idx_sem.at[slot])

    def start_rows(k, gather):
        window = k & (IDX_SLOTS - 1)
        buf = k & 1
        for j in range(rows):
            if gather:
                tok = idx_ref[window, j]
                pltpu.make_async_copy(h_hbm.at[pl.ds(tok, 1), :], x_buf.at[buf, pl.ds(j, 1), :],
                                      gat_sem.at[buf]).start(priority=j % 2)
            else:
                dst = idx_ref[window, rows + j]
                pltpu.make_async_copy(y_buf.at[buf, pl.ds(j, 1), :], ys_hbm.at[pl.ds(dst, 1), :],
                                      sct_sem.at[buf]).start(priority=j % 2)

    def wait_gather(k):
        pltpu.make_async_copy(h_hbm.at[pl.ds(0, rows), :], x_buf.at[k & 1], gat_sem.at[k & 1]).wait()

    def wait_scatter(k):
        pltpu.make_async_copy(y_buf.at[k & 1], ys_hbm.at[pl.ds(0, rows), :], sct_sem.at[k & 1]).wait()

    def weight_copies(e):
        return (pltpu.make_async_copy(wg_hbm.at[e], wg_st, w_sem.at[0]),
                pltpu.make_async_copy(wu_hbm.at[e], wu_st, w_sem.at[1]),
                pltpu.make_async_copy(wd_hbm.at[e], wd_st, w_sem.at[2]))

    def compute(k):
        x = x_buf[k & 1].astype(BF16)
        hid = _silu(jnp.dot(x, wg_bf[...], preferred_element_type=F32)) * jnp.dot(
            x, wu_bf[...], preferred_element_type=F32)
        y_buf[k & 1] = jnp.dot(hid.astype(BF16), wd_bf[...], preferred_element_type=F32)

    @pl.when(i == 0)
    def _():
        for cp in weight_copies(blk_e_ref[0]):
            cp.start()
        idx_copy(0).start()
        idx_copy(0).wait()
        start_rows(0, gather=True)

        @pl.when(1 < n_used)
        def _():
            idx_copy(1).start()

    @pl.when(i + 2 < n_used)
    def _():
        idx_copy(i + 2).start()

    @pl.when(i + 1 < n_used)
    def _():
        idx_copy(i + 1).wait()

    prev_e = blk_e_ref[jnp.maximum(i - 1, 0)]
    fresh = jnp.logical_or(i == 0, blk_e_ref[i] != prev_e)

    @pl.when(jnp.logical_and(used, fresh))
    def _():
        for cp in weight_copies(blk_e_ref[i]):
            cp.wait()
        wg_bf[...] = wg_st[...].astype(BF16)
        wu_bf[...] = wu_st[...].astype(BF16)
        wd_bf[...] = wd_st[...].astype(BF16)

        @pl.when(nxt_e_ref[i] >= 0)
        def _():
            for cp in weight_copies(nxt_e_ref[i]):
                cp.start()

    @pl.when(jnp.logical_and(i >= 2, i - 2 < n_used))
    def _():
        wait_scatter(i - 2)

    has_next = i + 1 < n_used
    interior = jnp.logical_and(i >= 1, has_next)

    @pl.when(interior)
    def _():
        wait_gather(i)
        start_rows(i - 1, gather=False)
        start_rows(i + 1, gather=True)
        compute(i)

    @pl.when(jnp.logical_and(used, jnp.logical_not(interior)))
    def _():
        @pl.when(has_next)
        def _():
            start_rows(i + 1, gather=True)

        @pl.when(i >= 1)
        def _():
            start_rows(i - 1, gather=False)

        wait_gather(i)
        compute(i)

        @pl.when(jnp.logical_not(has_next))
        def _():
            start_rows(i, gather=False)

    @pl.when(i == n_blk - 1)
    def _():
        @pl.when(jnp.logical_and(i >= 1, i - 1 < n_used))
        def _():
            wait_scatter(i - 1)

        @pl.when(used)
        def _():
            wait_scatter(i)

        y_buf[...] = jnp.zeros_like(y_buf)
        for half in range(2):
            clear = pltpu.make_async_copy(
                y_buf.at[half], ys_hbm.at[pl.ds(n_tokens * TOP_K + half * rows, rows), :],
                sct_sem.at[half])
            clear.start()
            clear.wait()


def _experts(h, codes, blk_e, nxt_e, n_used, w_gate, w_up, w_down):
    t, d = h.shape
    ff = w_gate.shape[2]
    n_blk = blk_e.shape[0]
    any_spec = pl.BlockSpec(memory_space=pl.ANY)
    grid_spec = pltpu.PrefetchScalarGridSpec(
        num_scalar_prefetch=3,
        grid=(n_blk,),
        in_specs=[any_spec] * 5,
        out_specs=any_spec,
        scratch_shapes=[pltpu.SMEM((IDX_SLOTS, 2 * EXPERT_ROWS), jnp.int32),
                        pltpu.VMEM((2, EXPERT_ROWS, d), F32), pltpu.VMEM((2, EXPERT_ROWS, d), F32),
                        pltpu.VMEM((d, ff), F32), pltpu.VMEM((d, ff), F32), pltpu.VMEM((ff, d), F32),
                        pltpu.VMEM((d, ff), BF16), pltpu.VMEM((d, ff), BF16),
                        pltpu.VMEM((ff, d), BF16),
                        pltpu.SemaphoreType.DMA((IDX_SLOTS,)), pltpu.SemaphoreType.DMA((2,)),
                        pltpu.SemaphoreType.DMA((2,)), pltpu.SemaphoreType.DMA((3,))])
    return pl.pallas_call(
        functools.partial(_experts_kernel, n_tokens=t),
        grid_spec=grid_spec,
        out_shape=jax.ShapeDtypeStruct((t * TOP_K + 2 * EXPERT_ROWS, d), F32),
        compiler_params=pltpu.CompilerParams(
            dimension_semantics=("arbitrary",), vmem_limit_bytes=VMEM_LIMIT),
        name="routed_experts",
    )(blk_e, nxt_e, n_used, codes, h, w_gate, w_up, w_down)


def _shared_ln_kernel(h_ref, *refs, alpha, eps):
    ys_refs = refs[:TOP_K]
    tw_ref, wg_ref, wu_ref, wd_ref, g_ref, b_ref, o_ref = refs[TOP_K:]
    h = h_ref[...]
    hb = h.astype(BF16)
    hid = _silu(jnp.dot(hb, wg_ref[...], preferred_element_type=F32)) * jnp.dot(
        hb, wu_ref[...], preferred_element_type=F32)
    moe = jnp.dot(hid.astype(BF16), wd_ref[...], preferred_element_type=F32)
    for k in range(TOP_K):
        moe = moe + ys_refs[k][...] * tw_ref[:, k:k + 1]
    o_ref[...] = _layer_norm(alpha * h + moe, g_ref[...], b_ref[...], eps)


def _shared_ln(h, ys, top_w, w_gate, w_up, w_down, g, b, *, tm, alpha, eps):
    t, d = h.shape
    ff = w_gate.shape[1]
    n_tiles = t // tm
    row = pl.BlockSpec((tm, d), lambda i: (i, 0))
    vec = pl.BlockSpec((1, d), lambda i: (0, 0))
    band = lambda k: pl.BlockSpec((tm, d), lambda i: (k * n_tiles + i, 0))
    return pl.pallas_call(
        functools.partial(_shared_ln_kernel, alpha=alpha, eps=eps),
        grid=(n_tiles,),
        in_specs=[row] + [band(k) for k in range(TOP_K)] + [
            pl.BlockSpec((tm, TOP_K), lambda i: (i, 0)),
            pl.BlockSpec((d, ff), lambda i: (0, 0)), pl.BlockSpec((d, ff), lambda i: (0, 0)),
            pl.BlockSpec((ff, d), lambda i: (0, 0)), vec, vec],
        out_specs=row,
        out_shape=jax.ShapeDtypeStruct((t, d), F32),
        compiler_params=pltpu.CompilerParams(
            dimension_semantics=("parallel",), vmem_limit_bytes=VMEM_LIMIT),
        name="shared_expert_layernorm",
    )(h, *([ys] * TOP_K), top_w, w_gate, w_up, w_down, g.reshape(1, d), b.reshape(1, d))


LN_EPS = 1e-5
ROPE_THETA = 10000.0


def _rope_tables(seq, dim):
    inv = 1.0 / (ROPE_THETA ** (jnp.arange(0, dim, 2, dtype=F32) / dim))
    ang = jnp.arange(seq, dtype=F32)[:, None] * inv[None, :]
    ang = jnp.concatenate([ang, ang], axis=-1)
    sign = jnp.where(jnp.arange(dim) < dim // 2, -1.0, 1.0).astype(F32)
    return jnp.cos(ang), jnp.sin(ang) * sign


def _pad_lora(x, d_decay, d_aaa, axis):
    assert d_decay <= LORA_PAD and d_aaa <= LORA_PAD
    decay, aaa, gate = jnp.split(x, [d_decay, d_decay + d_aaa], axis=axis)
    pad = lambda p: jnp.pad(p, [(0, LORA_PAD - p.shape[a]) if a == axis % x.ndim else (0, 0)
                                for a in range(x.ndim)])
    return jnp.concatenate([pad(decay), pad(aaa), gate], axis=axis)


def _dispatch(top_idx, n_tokens):
    n_assign = n_tokens * TOP_K
    i32 = jnp.int32
    e_flat = top_idx.reshape(n_assign).astype(i32)
    sorted_e, order = lax.sort((e_flat, jnp.arange(n_assign, dtype=i32)), num_keys=1, is_stable=True)
    experts = jnp.arange(N_EXPERTS, dtype=i32)
    grp_start = jnp.searchsorted(sorted_e, experts, side='left').astype(i32)
    grp_end = jnp.searchsorted(sorted_e, experts, side='right').astype(i32)
    counts = grp_end - grp_start
    padded = (counts + EXPERT_ROWS - 1) // EXPERT_ROWS * EXPERT_ROWS
    pad_end = jnp.cumsum(padded)
    pad_start = pad_end - padded
    n_blk = -(-n_assign // EXPERT_ROWS) + N_EXPERTS
    n_used = (pad_end[-1] // EXPERT_ROWS).astype(i32)
    first_row = jnp.arange(n_blk, dtype=i32) * EXPERT_ROWS
    blk_e = jnp.minimum(jnp.searchsorted(pad_end, first_row, side='right'), N_EXPERTS - 1).astype(i32)
    into = first_row - pad_start[blk_e]
    blk_s0 = jnp.clip(grp_start[blk_e] + into, 0, n_assign)
    blk_nv = jnp.clip(counts[blk_e] - into, 0, EXPERT_ROWS)
    later = jnp.where(counts > 0, experts, N_EXPERTS)
    nxt = lax.cummin(jnp.concatenate([later[1:], jnp.full((1,), N_EXPERTS, i32)]), reverse=True)
    nxt_e = jnp.where(nxt < N_EXPERTS, nxt, -1)[blk_e]
    row = jnp.arange(EXPERT_ROWS, dtype=i32)[None, :]
    src = jnp.minimum(blk_s0[:, None] + row, n_assign - 1)
    pad = n_assign + (jnp.arange(n_blk, dtype=i32)[:, None] % 2) * EXPERT_ROWS + row
    valid = row < blk_nv[:, None]
    ids = order[src]
    codes = jnp.concatenate([jnp.where(valid, ids % n_tokens, 0), jnp.where(valid, ids, pad)], axis=1)
    return codes.astype(i32), blk_e, nxt_e.astype(i32), n_used.reshape(1)


def kernel(x, w_in, shift_mu, w_lora_up, w0, a_lora_up, a0, g_lora_up, k_k, k_a, r_k, ln_x_w, ln_x_b,
           w_o_rwkv, w_o_moba, w_out, ln1_g, ln1_b, w_router, router_bias, w_gate_e, w_up_e,
           w_down_e, w_gate_s, w_up_s, w_down_s, ln2_g, ln2_b):
    batch, seq, d = x.shape
    depth = w_in.shape[0]
    alpha = (2 * depth) ** 0.25
    t = batch * seq
    c = d
    shift_w = shift_mu.shape[1]
    moba_heads = d // MOBA_HEAD
    cos, sin_signed = _rope_tables(seq, MOBA_HEAD)

    h = x.reshape(t, d)
    for layer in range(depth):
        wl = w_in[layer].astype(BF16)
        hb = h.astype(BF16)
        d_decay, d_aaa = w_lora_up.shape[1], a_lora_up.shape[1]
        w_lora = _pad_lora(wl[:, 3 * c:shift_w], d_decay, d_aaa, axis=1)
        assert w_lora.shape[1] % LANES == 0
        u_rkv = _matmul(hb, wl[:, :3 * c], tm=1024, tn=1024)
        u_lora = _matmul(hb, w_lora, tm=1024, tn=w_lora.shape[1])
        u_qkv = _matmul(hb, wl[:, shift_w:shift_w + 3 * d], tm=1024, tn=1024)
        u_gate = _matmul(hb, wl[:, shift_w + 3 * d:], tm=1024, tn=1024)

        pad_rows = lambda p: jnp.pad(p, ((0, LORA_PAD - p.shape[0]), (0, 0)))
        o_rwkv = _rwkv_mix(u_rkv, u_lora, shift_mu[layer][:3 * c],
                           _pad_lora(shift_mu[layer][3 * c:], d_decay, d_aaa, axis=0),
                           pad_rows(w_lora_up[layer]), w0[layer], pad_rows(a_lora_up[layer]),
                           a0[layer], g_lora_up[layer], k_k[layer], k_a[layer], r_k[layer],
                           ln_x_w[layer], ln_x_b[layer], batch=batch, seq=seq)
        o_moba = _moba(u_qkv, cos, sin_signed, batch=batch, seq=seq, heads=moba_heads)

        mixed = _merge(o_rwkv, o_moba, w_o_rwkv[layer].astype(BF16), w_o_moba[layer].astype(BF16),
                       u_gate, tm=512, tn=1024)
        h = _proj_ln(mixed, w_out[layer].astype(BF16), h, ln1_g[layer], ln1_b[layer],
                     tm=512, alpha=alpha, eps=LN_EPS)

        top_idx, top_w = _router(h, w_router[layer].T, router_bias[layer], tm=512)
        codes, blk_e, nxt_e, n_used = _dispatch(top_idx, t)
        ys = _experts(h, codes, blk_e, nxt_e, n_used,
                      w_gate_e[layer], w_up_e[layer], w_down_e[layer])
        h = _shared_ln(h, ys, top_w.T, w_gate_s[layer].astype(BF16),
                       w_up_s[layer].astype(BF16), w_down_s[layer].astype(BF16), ln2_g[layer],
                       ln2_b[layer], tm=128, alpha=alpha, eps=LN_EPS)
    return h.reshape(batch, seq, d)
```

```python
import functools

import jax
import jax.numpy as jnp
from jax import lax
from jax.experimental import pallas as pl
from jax.experimental.pallas import tpu as pltpu

F32 = jnp.float32
BF16 = jnp.bfloat16

LANES = 128
WKV_HEAD = 64
WKV_CHUNK = 64


def _dot(a, b):
    return jnp.dot(a.astype(BF16), b.astype(BF16), preferred_element_type=F32)


def _dot_nt(a, b):
    return lax.dot_general(a.astype(BF16), b.astype(BF16), (((1,), (1,)), ((), ())),
                           preferred_element_type=F32)


assert WKV_CHUNK == WKV_HEAD
def _each(f, *lists):
    return [f(*xs) for xs in zip(*lists)]


def _wkv_pairs(r, ld, k, v, a, b, m_state):
    c = WKV_CHUNK
    cat0 = lambda *xs: jnp.concatenate(xs, axis=0)
    cat1 = lambda *xs: jnp.concatenate(xs, axis=1)
    lane = lax.broadcasted_iota(jnp.int32, (c, LANES), 1)
    step = lax.broadcasted_iota(jnp.int32, (c, LANES), 0)
    head0 = lane < WKV_HEAD
    other = jnp.where(head0, lane, lane - WKV_HEAD)

    def stack(x):
        x = x.astype(BF16)
        zero = jnp.zeros_like(x)
        return cat0(jnp.where(head0, x, zero), jnp.where(head0, zero, x))

    def cumsum_rows(x):
        shift = 1
        while shift < c:
            x = x + jnp.where(step >= shift, pltpu.roll(x, shift, axis=0), 0.0)
            shift *= 2
        return x

    dot = lambda p, q: jnp.dot(p.astype(BF16), q, preferred_element_type=F32)

    cw = _each(cumsum_rows, ld)
    e_out = _each(lambda x: jnp.exp(-x), cw)
    a_t = _each(lambda x, w, l: x * jnp.exp(w - l), a, cw, ld)
    r_t = _each(lambda x, w: x * jnp.exp(w), r, cw)
    bk_st = _each(lambda p, q, e: cat0(stack(p * e), stack(q * e)), b, k, e_out)
    v_st = _each(stack, v)
    aa = _each(lambda p, q, s: _dot_nt(cat0(p, q), s), a_t, r_t, bk_st)

    strict = step > other
    incl = step >= other
    n_ab = _each(lambda x: jnp.where(strict, x[:c, :LANES], 0.0), aa)
    a_ak = _each(lambda x: jnp.where(strict, x[:c, LANES:], 0.0), aa)
    a_rb = _each(lambda x: jnp.where(incl, x[c:, :LANES], 0.0), aa)
    a_rk = _each(lambda x: jnp.where(incl, x[c:, LANES:], 0.0), aa)

    eye = jnp.where(step == other, 1.0, 0.0)
    t_inv = _each(lambda n: eye + n, n_ab)
    w1 = _each(dot, a_ak, v_st)
    n_pow = _each(lambda n: dot(n, stack(n)), n_ab)
    for _ in range(c.bit_length() - 3):
        pt = _each(lambda n, t: dot(n, cat1(stack(n), stack(t))), n_pow, t_inv)
        n_pow = _each(lambda x: x[:, :LANES], pt)
        t_inv = _each(lambda t, x: t + x[:, LANES:], t_inv, pt)
    t_inv = _each(lambda t, n: t + dot(n, stack(t)), t_inv, n_pow)

    tw = _each(lambda t, w, x: dot(t, cat1(stack(w), stack(x))), t_inv, w1, a_t)
    rhs = _each(lambda x, v_: cat0(cat1(stack(x[:, LANES:]), stack(x[:, :LANES])),
                                   cat1(jnp.zeros_like(v_), v_)), tw, v_st)
    out_p = _each(lambda p, q, s: dot(cat1(p, q), s), a_rb, a_rk, rhs)
    tail = _each(lambda w: jnp.exp(w[c - 1:c, :] - w), cw)
    bkp_st = _each(lambda p, q, e: cat0(stack(p * e), stack(q * e)), b, k, tail)
    upd = _each(lambda p, s: lax.dot_general(p, s, (((0,), (0,)), ((), ())),
                                             preferred_element_type=F32), bkp_st, rhs)
    upd_p = _each(lambda u: u[:WKV_HEAD] + u[WKV_HEAD:], upd)
    r_hat = _each(lambda x, o: x + o[:, :LANES], r_t, out_p)
    g = _each(lambda w, u: jnp.where(step == other, jnp.exp(w[c - 1:c, :]), 0.0) + u[:, :LANES],
              cw, upd_p)

    def apply_state(lhs, m):
        l_hi = lhs.astype(BF16)
        l_lo = (lhs - l_hi.astype(F32)).astype(BF16)
        m_hi = m.astype(BF16)
        m_lo = m - m_hi.astype(F32)
        both = jnp.dot(l_hi, cat1(stack(m_hi), stack(m_lo)), preferred_element_type=F32)
        return (both[:, :LANES] + both[:, LANES:]) + jnp.dot(l_lo, stack(m_hi),
                                                             preferred_element_type=F32)

    rm = _each(lambda p, q, m: apply_state(cat0(p, q), m), r_hat, g, m_state)
    y = _each(lambda x, o: x[:c] + o[:, LANES:], rm, out_p)
    m_new = _each(lambda x, u: x[c:] + u[:, LANES:], rm, upd_p)
    return y, m_new


RWKV_GN_EPS = 64e-5
LORA_PAD = LANES


def _rwkv_kernel(u_ref, ul_ref, mu_ref, mul_ref, wl_ref, w0_ref, al_ref, a0_ref, gl_ref, kk_ref,
                 ka_ref, rk_ref, lnw_ref, lnb_ref, o_ref, m_ref, prev_ref, prevl_ref):
    c = WKV_CHUNK
    width = o_ref.shape[1]

    @pl.when(pl.program_id(1) == 0)
    def _():
        m_ref[...] = jnp.zeros_like(m_ref)
        prev_ref[...] = jnp.zeros_like(prev_ref)
        prevl_ref[...] = jnp.zeros_like(prevl_ref)

    def shifted(u, prev, mu):
        first = lax.broadcasted_iota(jnp.int32, u.shape, 0) == 0
        before = jnp.where(first, prev[7:8, :], pltpu.roll(u, 1, axis=0))
        return u + (before - u) * mu

    u = u_ref[...]
    ul = ul_ref[...]
    z = shifted(u, prev_ref[...], mu_ref[...])
    zl = shifted(ul, prevl_ref[...], mul_ref[...])
    prev_ref[...] = u[c - 8:, :]
    prevl_ref[...] = ul[c - 8:, :]

    r = z[:, :width]
    k_raw = z[:, width:2 * width]
    v = z[:, 2 * width:]
    w = w0_ref[...] + _dot(jnp.tanh(zl[:, :LORA_PAD]), wl_ref[...])
    ld = -jnp.exp(-(jnp.maximum(-w, 0.0) + jnp.log(1.0 + jnp.exp(-jnp.abs(w)))) - 0.5)
    a = _sigmoid(a0_ref[...] + _dot(zl[:, LORA_PAD:2 * LORA_PAD], al_ref[...]))
    g = _dot(_sigmoid(zl[:, 2 * LORA_PAD:]), gl_ref[...])
    kk_raw = k_raw * kk_ref[...]
    k = k_raw * (1.0 + (a - 1.0) * ka_ref[...])

    lanes = [slice(p * LANES, (p + 1) * LANES) for p in range(width // LANES)]
    split = lambda x: [x[:, sl] for sl in lanes]
    row = lax.broadcasted_iota(jnp.int32, (LANES, LANES), 0)
    col = lax.broadcasted_iota(jnp.int32, (LANES, LANES), 1)
    same_head = jnp.where((row < WKV_HEAD) == (col < WKV_HEAD), 1.0, 0.0).astype(BF16)

    def head_sums(x):
        stacked = jnp.concatenate(split(x), axis=0).astype(BF16)
        s = jnp.dot(stacked, same_head, preferred_element_type=F32)
        return [s[p * c:(p + 1) * c] for p in range(len(lanes))]

    norm2 = head_sums(kk_raw * kk_raw)
    kk = _each(lambda x, s: x * lax.rsqrt(jnp.maximum(s, 1e-24)), split(kk_raw), norm2)
    bonus = head_sums(r * k * rk_ref[...])

    m_state = [m_ref[p] for p in range(len(lanes))]
    y, m_new = _wkv_pairs(split(r), split(ld), split(k), split(v), _each(lambda x: -x, kk),
                          _each(lambda x, a_: x * a_, kk, split(a)), m_state)
    for p in range(len(lanes)):
        m_ref[p] = m_new[p]

    inv_n = 1.0 / WKV_HEAD
    y = jnp.concatenate(y, axis=1)
    mean = jnp.concatenate(head_sums(y), axis=1) * inv_n
    yc = y - mean
    var = jnp.concatenate(head_sums(yc * yc), axis=1) * inv_n
    yn = yc * lax.rsqrt(var + RWKV_GN_EPS) * lnw_ref[...] + lnb_ref[...]
    out = (yn + jnp.concatenate(bonus, axis=1) * v) * g
    o_ref[...] = out.astype(o_ref.dtype)


def _rwkv_mix(u_rkv, u_lora, mu_rkv, mu_lora, w_lora, w0, a_lora, a0, g_lora, k_k, k_a, r_k,
              ln_w, ln_b, *, batch, seq):
    t = u_rkv.shape[0]
    width = u_rkv.shape[1] // 3
    n_lora = u_lora.shape[1]
    assert t == batch * seq and seq % WKV_CHUNK == 0 and width % LANES == 0
    n_chunks = seq // WKV_CHUNK
    rows = lambda n: pl.BlockSpec((WKV_CHUNK, n), lambda bi, ci: (bi * n_chunks + ci, 0))
    whole = lambda x: pl.BlockSpec(x.shape, lambda bi, ci: (0,) * x.ndim)
    vec = lambda x: x.reshape(1, -1)
    params = [vec(mu_rkv), vec(mu_lora), w_lora.astype(BF16), vec(w0), a_lora.astype(BF16), vec(a0),
              g_lora.astype(BF16), vec(k_k), vec(k_a), vec(r_k), vec(ln_w), vec(ln_b)]
    return pl.pallas_call(
        _rwkv_kernel,
        grid=(batch, n_chunks),
        in_specs=[rows(3 * width), rows(n_lora)] + [whole(p) for p in params],
        out_specs=rows(width),
        out_shape=jax.ShapeDtypeStruct((t, width), BF16),
        scratch_shapes=[pltpu.VMEM((width // LANES, WKV_HEAD, LANES), F32),
                        pltpu.VMEM((8, 3 * width), F32), pltpu.VMEM((8, n_lora), F32)],
        compiler_params=pltpu.CompilerParams(
            dimension_semantics=("parallel", "arbitrary"), vmem_limit_bytes=VMEM_LIMIT),
        name="rwkv7_time_mix",
    )(u_rkv, u_lora, *params)


MOBA_BLOCK = 256
MOBA_TOPK = 3
MOBA_HEAD = 128


def _rope(t, cos, sin_signed):
    return t * cos + pltpu.roll(t, MOBA_HEAD // 2, axis=1) * sin_signed


def _moba_kernel(q_ref, k_ref, v_ref, cos_ref, sin_ref, o_ref, kr_ref, vb_ref, kmean_ref, *,
                 nb, scale):
    blk = MOBA_BLOCK
    kmean_ref[...] = jnp.zeros_like(kmean_ref)
    for n in range(nb):
        rows = slice(n * blk, (n + 1) * blk)
        kr = _rope(k_ref[rows, :], cos_ref[rows, :], sin_ref[rows, :])
        kr_ref[rows, :] = kr.astype(BF16)
        vb_ref[rows, :] = v_ref[rows, :].astype(BF16)
        kmean_ref[n:n + 1, :] = jnp.mean(kr, axis=0, keepdims=True)

    lane = lax.broadcasted_iota(jnp.int32, (blk, LANES), 1)
    row = lax.broadcasted_iota(jnp.int32, (blk, blk), 0)
    col = lax.broadcasted_iota(jnp.int32, (blk, blk), 1)
    causal = row >= col
    for i in range(nb):
        rows = slice(i * blk, (i + 1) * blk)
        q = _rope(q_ref[rows, :], cos_ref[rows, :], sin_ref[rows, :])
        width = (i + 1) * blk
        s = _dot_nt(q, kr_ref[:width, :]) * scale
        parts = []
        if i > MOBA_TOPK:
            gate = lax.dot_general(q, kmean_ref[...], (((1,), (1,)), ((), ())),
                                   precision=lax.Precision.HIGHEST, preferred_element_type=F32)
            gate = jnp.where(lane < i, gate, -jnp.inf)
            rank = jnp.zeros((blk, LANES), jnp.int32)
            for m in range(i):
                g_m = gate[:, m:m + 1]
                rank = rank + jnp.where(
                    g_m > gate, 1, jnp.where(g_m == gate, jnp.where(lane > m, 1, 0), 0))
            chosen = jnp.where(rank < MOBA_TOPK, 1.0, 0.0)
            for n in range(i):
                take = chosen[:, n:n + 1] > 0.5
                parts.append(jnp.where(take, s[:, n * blk:(n + 1) * blk], -jnp.inf))
        else:
            parts = [s[:, n * blk:(n + 1) * blk] for n in range(i)]
        parts.append(jnp.where(causal, s[:, i * blk:], -jnp.inf))
        s = jnp.concatenate(parts, axis=1) if i else parts[0]
        m_row = jnp.max(s, axis=1, keepdims=True)
        p = jnp.exp(s - m_row)
        l_row = jnp.sum(p, axis=1, keepdims=True)
        acc = jnp.dot(p.astype(BF16), vb_ref[:width, :], preferred_element_type=F32)
        o_ref[rows, :] = (acc / l_row).astype(o_ref.dtype)


def _moba(u_qkv, cos, sin_signed, *, batch, seq, heads):
    t = u_qkv.shape[0]
    assert seq % MOBA_BLOCK == 0
    nb = seq // MOBA_BLOCK
    assert nb <= LANES
    kernel = functools.partial(_moba_kernel, nb=nb, scale=MOBA_HEAD ** -0.5)
    full = lambda off: pl.BlockSpec((seq, MOBA_HEAD), lambda bi, hi: (bi, off + hi))
    table = pl.BlockSpec((seq, MOBA_HEAD), lambda bi, hi: (0, 0))
    return pl.pallas_call(
        kernel,
        grid=(batch, heads),
        in_specs=[full(0), full(heads), full(2 * heads), table, table],
        out_specs=full(0),
        out_shape=jax.ShapeDtypeStruct((t, heads * MOBA_HEAD), BF16),
        scratch_shapes=[pltpu.VMEM((seq, MOBA_HEAD), BF16), pltpu.VMEM((seq, MOBA_HEAD), BF16),
                        pltpu.VMEM((LANES, MOBA_HEAD), F32)],
        compiler_params=pltpu.CompilerParams(dimension_semantics=("parallel", "parallel")),
        name="moba_attention",
    )(u_qkv, u_qkv, u_qkv, cos, sin_signed)


VMEM_LIMIT = 56 * 1024 * 1024


def _matmul_kernel(x_ref, w_ref, o_ref):
    o_ref[...] = jnp.dot(x_ref[...], w_ref[...], preferred_element_type=F32).astype(o_ref.dtype)


def _matmul(x, w, *, tm, tn, out_dtype=F32):
    m, k = x.shape
    n = w.shape[1]
    assert m % tm == 0 and n % tn == 0
    return pl.pallas_call(
        _matmul_kernel,
        grid=(m // tm, n // tn),
        in_specs=[pl.BlockSpec((tm, k), lambda i, j: (i, 0)),
                  pl.BlockSpec((k, tn), lambda i, j: (0, j))],
        out_specs=pl.BlockSpec((tm, tn), lambda i, j: (i, j)),
        out_shape=jax.ShapeDtypeStruct((m, n), out_dtype),
        compiler_params=pltpu.CompilerParams(
            dimension_semantics=("parallel", "parallel"), vmem_limit_bytes=VMEM_LIMIT),
        name="matmul",
    )(x, w)


def _sigmoid(x):
    return 1.0 / (1.0 + jnp.exp(-x))


def _merge_kernel(a_ref, b_ref, pa_ref, pb_ref, ga_ref, gb_ref, o_ref):
    ya = jnp.dot(a_ref[...], pa_ref[...], preferred_element_type=F32)
    yb = jnp.dot(b_ref[...], pb_ref[...], preferred_element_type=F32)
    o_ref[...] = (_sigmoid(ga_ref[...]) * ya + _sigmoid(gb_ref[...]) * yb).astype(o_ref.dtype)


def _merge(o_a, o_b, p_a, p_b, gates, *, tm, tn):
    m, k = o_a.shape
    n = p_a.shape[1]
    nj = n // tn
    row = pl.BlockSpec((tm, k), lambda i, j: (i, 0))
    wcol = pl.BlockSpec((k, tn), lambda i, j: (0, j))
    return pl.pallas_call(
        _merge_kernel,
        grid=(m // tm, nj),
        in_specs=[row, row, wcol, wcol,
                  pl.BlockSpec((tm, tn), lambda i, j: (i, j)),
                  pl.BlockSpec((tm, tn), lambda i, j: (i, nj + j))],
        out_specs=pl.BlockSpec((tm, tn), lambda i, j: (i, j)),
        out_shape=jax.ShapeDtypeStruct((m, n), BF16),
        compiler_params=pltpu.CompilerParams(
            dimension_semantics=("parallel", "parallel"), vmem_limit_bytes=VMEM_LIMIT),
        name="gated_merge",
    )(o_a, o_b, p_a, p_b, gates, gates)


def _layer_norm(x, g, b, eps):
    mu = jnp.mean(x, axis=-1, keepdims=True)
    xc = x - mu
    var = jnp.mean(xc * xc, axis=-1, keepdims=True)
    return xc * lax.rsqrt(var + eps) * g + b


def _proj_ln_kernel(m_ref, w_ref, x_ref, g_ref, b_ref, o_ref, *, alpha, eps):
    y = jnp.dot(m_ref[...], w_ref[...], preferred_element_type=F32)
    o_ref[...] = _layer_norm(alpha * x_ref[...] + y, g_ref[...], b_ref[...], eps)


def _proj_ln(mixed, w, x, g, b, *, tm, alpha, eps):
    m, k = mixed.shape
    n = w.shape[1]
    vec = pl.BlockSpec((1, n), lambda i: (0, 0))
    return pl.pallas_call(
        functools.partial(_proj_ln_kernel, alpha=alpha, eps=eps),
        grid=(m // tm,),
        in_specs=[pl.BlockSpec((tm, k), lambda i: (i, 0)),
                  pl.BlockSpec((k, n), lambda i: (0, 0)),
                  pl.BlockSpec((tm, n), lambda i: (i, 0)), vec, vec],
        out_specs=pl.BlockSpec((tm, n), lambda i: (i, 0)),
        out_shape=jax.ShapeDtypeStruct((m, n), F32),
        compiler_params=pltpu.CompilerParams(
            dimension_semantics=("parallel",), vmem_limit_bytes=VMEM_LIMIT),
        name="out_proj_layernorm",
    )(mixed, w, x, g.reshape(1, n), b.reshape(1, n))


N_EXPERTS = 256
N_GROUPS = 8
TOPK_GROUPS = 4
TOP_K = 8
ROUTED_SCALE = 2.5


def _router_kernel(h_ref, wt_ref, bias_ref, idx_ref, wgt_ref):
    tm = h_ref.shape[0]
    per = N_EXPERTS // N_GROUPS
    logits = lax.dot_general(wt_ref[...], h_ref[...], (((1,), (1,)), ((), ())),
                             precision=lax.Precision.HIGHEST, preferred_element_type=F32)
    scores = _sigmoid(logits)
    biased = scores + bias_ref[...]
    neg = -jnp.inf

    grp = biased.reshape(N_GROUPS, per, tm)
    slot = lax.broadcasted_iota(jnp.int32, (N_GROUPS, per, tm), 1)
    top1 = jnp.max(grp, axis=1, keepdims=True)
    first = jnp.min(jnp.where(grp == top1, slot, per), axis=1, keepdims=True)
    top2 = jnp.max(jnp.where(slot == first, neg, grp), axis=1, keepdims=True)
    grp_score = (top1 + top2).reshape(N_GROUPS, tm)

    gid = lax.broadcasted_iota(jnp.int32, (N_GROUPS, tm), 0)
    rank = jnp.zeros((N_GROUPS, tm), jnp.int32)
    for m in range(N_GROUPS):
        s_m = grp_score[m:m + 1, :]
        rank = rank + jnp.where(s_m > grp_score, 1,
                                jnp.where(s_m == grp_score, jnp.where(gid > m, 1, 0), 0))
    keep = (rank < TOPK_GROUPS).reshape(N_GROUPS, 1, tm)
    masked = jnp.where(keep, grp, neg).reshape(N_EXPERTS, tm)

    eid = lax.broadcasted_iota(jnp.int32, (N_EXPERTS, tm), 0)
    idxs, wgts = [], []
    for _ in range(TOP_K):
        best = jnp.max(masked, axis=0, keepdims=True)
        pick = jnp.min(jnp.where(masked == best, eid, N_EXPERTS), axis=0, keepdims=True)
        hit = eid == pick
        idxs.append(pick)
        wgts.append(jnp.sum(jnp.where(hit, scores, 0.0), axis=0, keepdims=True))
        masked = jnp.where(hit, neg, masked)
    wgt = jnp.concatenate(wgts, axis=0)
    wgt = wgt / jnp.sum(wgt, axis=0, keepdims=True) * ROUTED_SCALE
    idx_ref[...] = jnp.concatenate(idxs, axis=0)
    wgt_ref[...] = wgt


def _router(h, w_router_t, bias, *, tm):
    t, d = h.shape
    out = pl.BlockSpec((TOP_K, tm), lambda i: (0, i))
    return pl.pallas_call(
        _router_kernel,
        grid=(t // tm,),
        in_specs=[pl.BlockSpec((tm, d), lambda i: (i, 0)),
                  pl.BlockSpec((N_EXPERTS, d), lambda i: (0, 0)),
                  pl.BlockSpec((N_EXPERTS, 1), lambda i: (0, 0))],
        out_specs=[out, out],
        out_shape=[jax.ShapeDtypeStruct((TOP_K, t), jnp.int32),
                   jax.ShapeDtypeStruct((TOP_K, t), F32)],
        compiler_params=pltpu.CompilerParams(
            dimension_semantics=("parallel",), vmem_limit_bytes=VMEM_LIMIT),
        name="moe_router",
    )(h, w_router_t, bias.reshape(N_EXPERTS, 1))


EXPERT_ROWS = 128


def _silu(x):
    return x * _sigmoid(x)


IDX_SLOTS = 4


def _experts_kernel(blk_e_ref, nxt_e_ref, n_used_ref,
                    codes_hbm, h_hbm, wg_hbm, wu_hbm, wd_hbm, ys_hbm,
                    idx_ref, x_buf, y_buf, wg_st, wu_st, wd_st, wg_bf, wu_bf, wd_bf,
                    idx_sem, gat_sem, sct_sem, w_sem, *, n_tokens):
    i = pl.program_id(0)
    n_blk = pl.num_programs(0)
    n_used = n_used_ref[0]
    used = i < n_used
    rows = EXPERT_ROWS
    chunks = x_buf.shape[1]

    def idx_copy(k):
        slot = k & (IDX_SLOTS - 1)
        return pltpu.make_async_copy(codes_hbm.at[pl.ds(k, 1), :], idx_ref.at[pl.ds(slot, 1), :],
                                     idx_sem.at[slot])

    def start_rows(k, gather):
        window = k & (IDX_SLOTS - 1)
        buf = k & 1
        for j in range(rows):
            if gather:
                src = idx_ref[window, j]
                pltpu.make_async_copy(h_hbm.at[pl.ds(src, chunks), :], x_buf.at[buf, :, j, :],
                                      gat_sem.at[buf]).start()
            else:
                dst = idx_ref[window, rows + j]
                pltpu.make_async_copy(y_buf.at[buf, :, j, :], ys_hbm.at[pl.ds(dst, chunks), :],
                                      sct_sem.at[buf]).start()

    def wait_gather(k):
        pltpu.make_async_copy(x_buf.at[k & 1], x_buf.at[k & 1], gat_sem.at[k & 1]).wait()

    def wait_scatter(k):
        pltpu.make_async_copy(y_buf.at[k & 1], y_buf.at[k & 1], sct_sem.at[k & 1]).wait()

    def weight_copies(e):
        return (pltpu.make_async_copy(wg_hbm.at[e], wg_st, w_sem.at[0]),
                pltpu.make_async_copy(wu_hbm.at[e], wu_st, w_sem.at[1]),
                pltpu.make_async_copy(wd_hbm.at[e], wd_st, w_sem.at[2]))

    def compute(k):
        x = jnp.concatenate([x_buf[k & 1, c] for c in range(chunks)], axis=1).astype(BF16)
        hid = _silu(jnp.dot(x, wg_bf[...], preferred_element_type=F32)) * jnp.dot(
            x, wu_bf[...], preferred_element_type=F32)
        y = jnp.dot(hid.astype(BF16), wd_bf[...], preferred_element_type=F32)
        for c in range(chunks):
            y_buf[k & 1, c] = y[:, c * LANES:(c + 1) * LANES]

    @pl.when(i == 0)
    def _():
        for cp in weight_copies(blk_e_ref[0]):
            cp.start()
        idx_copy(0).start()
        idx_copy(0).wait()
        start_rows(0, gather=True)

        @pl.when(1 < n_used)
        def _():
            idx_copy(1).start()

    @pl.when(i + 2 < n_used)
    def _():
        idx_copy(i + 2).start()

    @pl.when(i + 1 < n_used)
    def _():
        idx_copy(i + 1).wait()

    prev_e = blk_e_ref[jnp.maximum(i - 1, 0)]
    fresh = jnp.logical_or(i == 0, blk_e_ref[i] != prev_e)

    @pl.when(jnp.logical_and(used, fresh))
    def _():
        for cp in weight_copies(blk_e_ref[i]):
            cp.wait()
        wg_bf[...] = wg_st[...].astype(BF16)
        wu_bf[...] = wu_st[...].astype(BF16)
        wd_bf[...] = wd_st[...].astype(BF16)

        @pl.when(nxt_e_ref[i] >= 0)
        def _():
            for cp in weight_copies(nxt_e_ref[i]):
                cp.start()

    @pl.when(jnp.logical_and(i >= 2, i - 2 < n_used))
    def _():
        wait_scatter(i - 2)

    has_next = i + 1 < n_used
    interior = jnp.logical_and(i >= 1, has_next)

    @pl.when(interior)
    def _():
        wait_gather(i)
        start_rows(i - 1, gather=False)
        start_rows(i + 1, gather=True)
        compute(i)

    @pl.when(jnp.logical_and(used, jnp.logical_not(interior)))
    def _():
        @pl.when(has_next)
        def _():
            start_rows(i + 1, gather=True)

        @pl.when(i >= 1)
        def _():
            start_rows(i - 1, gather=False)

        wait_gather(i)
        compute(i)

        @pl.when(jnp.logical_not(has_next))
        def _():
            start_rows(i, gather=False)

    @pl.when(i == n_blk - 1)
    def _():
        @pl.when(jnp.logical_and(i >= 1, i - 1 < n_used))
        def _():
            wait_scatter(i - 1)

        @pl.when(used)
        def _():
            wait_scatter(i)

        y_buf[...] = jnp.zeros_like(y_buf)
        for half in range(2):
            for j in range(rows):
                first = (n_tokens * TOP_K + half * rows + j) * chunks
                pltpu.make_async_copy(y_buf.at[half, :, j, :], ys_hbm.at[pl.ds(first, chunks), :],
                                      sct_sem.at[half]).start()
            wait_scatter(half)


def _experts(h, codes, blk_e, nxt_e, n_used, w_gate, w_up, w_down):
    d, ff = w_gate.shape[1], w_gate.shape[2]
    chunks = d // LANES
    t = h.shape[0] // chunks
    n_blk = blk_e.shape[0]
    any_spec = pl.BlockSpec(memory_space=pl.ANY)
    grid_spec = pltpu.PrefetchScalarGridSpec(
        num_scalar_prefetch=3,
        grid=(n_blk,),
        in_specs=[any_spec] * 5,
        out_specs=any_spec,
        scratch_shapes=[pltpu.SMEM((IDX_SLOTS, 2 * EXPERT_ROWS), jnp.int32),
                        pltpu.VMEM((2, chunks, EXPERT_ROWS, LANES), F32),
                        pltpu.VMEM((2, chunks, EXPERT_ROWS, LANES), F32),
                        pltpu.VMEM((d, ff), F32), pltpu.VMEM((d, ff), F32), pltpu.VMEM((ff, d), F32),
                        pltpu.VMEM((d, ff), BF16), pltpu.VMEM((d, ff), BF16),
                        pltpu.VMEM((ff, d), BF16),
                        pltpu.SemaphoreType.DMA((IDX_SLOTS,)), pltpu.SemaphoreType.DMA((2,)),
                        pltpu.SemaphoreType.DMA((2,)), pltpu.SemaphoreType.DMA((3,))])
    return pl.pallas_call(
        functools.partial(_experts_kernel, n_tokens=t),
        grid_spec=grid_spec,
        out_shape=jax.ShapeDtypeStruct(((t * TOP_K + 2 * EXPERT_ROWS) * chunks, LANES), F32),
        compiler_params=pltpu.CompilerParams(
            dimension_semantics=("arbitrary",), vmem_limit_bytes=VMEM_LIMIT),
        name="routed_experts",
    )(blk_e, nxt_e, n_used, codes, h, w_gate, w_up, w_down)


def _shared_ln_kernel(h_ref, *refs, alpha, eps):
    ys_refs = refs[:TOP_K]
    tw_ref, wg_ref, wu_ref, wd_ref, g_ref, b_ref, o_ref, routed_ref = refs[TOP_K:]
    h = h_ref[...]
    tm, d = h.shape
    chunks = d // LANES
    hb = h.astype(BF16)
    hid = _silu(jnp.dot(hb, wg_ref[...], preferred_element_type=F32)) * jnp.dot(
        hb, wu_ref[...], preferred_element_type=F32)
    moe = jnp.dot(hid.astype(BF16), wd_ref[...], preferred_element_type=F32)
    routed = ys_refs[0][...] * tw_ref[:, 0:1]
    for k in range(1, TOP_K):
        routed = routed + ys_refs[k][...] * tw_ref[:, k:k + 1]
    routed_ref[...] = routed
    moe = moe + jnp.concatenate(
        [routed_ref[pl.ds(c, tm, stride=chunks), :] for c in range(chunks)], axis=1)
    o_ref[...] = _layer_norm(alpha * h + moe, g_ref[...], b_ref[...], eps)


def _shared_ln(h, ys, top_w, w_gate, w_up, w_down, g, b, *, tm, alpha, eps):
    t, d = h.shape
    ff = w_gate.shape[1]
    chunks = d // LANES
    n_tiles = t // tm
    row = pl.BlockSpec((tm, d), lambda i: (i, 0))
    vec = pl.BlockSpec((1, d), lambda i: (0, 0))
    band = lambda k: pl.BlockSpec((tm * chunks, LANES), lambda i: (k * n_tiles + i, 0))
    return pl.pallas_call(
        functools.partial(_shared_ln_kernel, alpha=alpha, eps=eps),
        grid=(n_tiles,),
        in_specs=[row] + [band(k) for k in range(TOP_K)] + [
            pl.BlockSpec((tm * chunks, TOP_K), lambda i: (i, 0)),
            pl.BlockSpec((d, ff), lambda i: (0, 0)), pl.BlockSpec((d, ff), lambda i: (0, 0)),
            pl.BlockSpec((ff, d), lambda i: (0, 0)), vec, vec],
        out_specs=row,
        out_shape=jax.ShapeDtypeStruct((t, d), F32),
        scratch_shapes=[pltpu.VMEM((tm * chunks, LANES), F32)],
        compiler_params=pltpu.CompilerParams(
            dimension_semantics=("parallel",), vmem_limit_bytes=VMEM_LIMIT),
        name="shared_expert_layernorm",
    )(h, *([ys] * TOP_K), top_w, w_gate, w_up, w_down, g.reshape(1, d), b.reshape(1, d))


LN_EPS = 1e-5
ROPE_THETA = 10000.0


def _rope_tables(seq, dim):
    inv = 1.0 / (ROPE_THETA ** (jnp.arange(0, dim, 2, dtype=F32) / dim))
    ang = jnp.arange(seq, dtype=F32)[:, None] * inv[None, :]
    ang = jnp.concatenate([ang, ang], axis=-1)
    sign = jnp.where(jnp.arange(dim) < dim // 2, -1.0, 1.0).astype(F32)
    return jnp.cos(ang), jnp.sin(ang) * sign


def _pad_lora(x, d_decay, d_aaa, axis):
    assert d_decay <= LORA_PAD and d_aaa <= LORA_PAD
    decay, aaa, gate = jnp.split(x, [d_decay, d_decay + d_aaa], axis=axis)
    pad = lambda p: jnp.pad(p, [(0, LORA_PAD - p.shape[a]) if a == axis % x.ndim else (0, 0)
                                for a in range(x.ndim)])
    return jnp.concatenate([pad(decay), pad(aaa), gate], axis=axis)


def _dispatch(top_idx, n_tokens, row_stride):
    n_assign = n_tokens * TOP_K
    i32 = jnp.int32
    e_flat = top_idx.reshape(n_assign).astype(i32)
    sorted_e, order = lax.sort((e_flat, jnp.arange(n_assign, dtype=i32)), num_keys=1, is_stable=True)
    experts = jnp.arange(N_EXPERTS, dtype=i32)
    grp_start = jnp.searchsorted(sorted_e, experts, side='left').astype(i32)
    grp_end = jnp.searchsorted(sorted_e, experts, side='right').astype(i32)
    counts = grp_end - grp_start
    padded = (counts + EXPERT_ROWS - 1) // EXPERT_ROWS * EXPERT_ROWS
    pad_end = jnp.cumsum(padded)
    pad_start = pad_end - padded
    n_blk = -(-n_assign // EXPERT_ROWS) + N_EXPERTS
    n_used = (pad_end[-1] // EXPERT_ROWS).astype(i32)
    first_row = jnp.arange(n_blk, dtype=i32) * EXPERT_ROWS
    blk_e = jnp.minimum(jnp.searchsorted(pad_end, first_row, side='right'), N_EXPERTS - 1).astype(i32)
    into = first_row - pad_start[blk_e]
    blk_s0 = jnp.clip(grp_start[blk_e] + into, 0, n_assign)
    blk_nv = jnp.clip(counts[blk_e] - into, 0, EXPERT_ROWS)
    later = jnp.where(counts > 0, experts, N_EXPERTS)
    nxt = lax.cummin(jnp.concatenate([later[1:], jnp.full((1,), N_EXPERTS, i32)]), reverse=True)
    nxt_e = jnp.where(nxt < N_EXPERTS, nxt, -1)[blk_e]
    row = jnp.arange(EXPERT_ROWS, dtype=i32)[None, :]
    src = jnp.minimum(blk_s0[:, None] + row, n_assign - 1)
    pad = n_assign + (jnp.arange(n_blk, dtype=i32)[:, None] % 2) * EXPERT_ROWS + row
    valid = row < blk_nv[:, None]
    ids = order[src]
    codes = jnp.concatenate([jnp.where(valid, ids % n_tokens, 0), jnp.where(valid, ids, pad)], axis=1)
    return (codes * row_stride).astype(i32), blk_e, nxt_e.astype(i32), n_used.reshape(1)


def kernel(x, w_in, shift_mu, w_lora_up, w0, a_lora_up, a0, g_lora_up, k_k, k_a, r_k, ln_x_w, ln_x_b,
           w_o_rwkv, w_o_moba, w_out, ln1_g, ln1_b, w_router, router_bias, w_gate_e, w_up_e,
           w_down_e, w_gate_s, w_up_s, w_down_s, ln2_g, ln2_b):
    batch, seq, d = x.shape
    depth = w_in.shape[0]
    alpha = (2 * depth) ** 0.25
    t = batch * seq
    c = d
    shift_w = shift_mu.shape[1]
    moba_heads = d // MOBA_HEAD
    cos, sin_signed = _rope_tables(seq, MOBA_HEAD)

    h = x.reshape(t, d)
    for layer in range(depth):
        wl = w_in[layer].astype(BF16)
        hb = h.astype(BF16)
        d_decay, d_aaa = w_lora_up.shape[1], a_lora_up.shape[1]
        w_lora = _pad_lora(wl[:, 3 * c:shift_w], d_decay, d_aaa, axis=1)
        assert w_lora.shape[1] % LANES == 0
        u_rkv = _matmul(hb, wl[:, :3 * c], tm=1024, tn=1024)
        u_lora = _matmul(hb, w_lora, tm=1024, tn=w_lora.shape[1])
        u_qkv = _matmul(hb, wl[:, shift_w:shift_w + 3 * d], tm=1024, tn=1024)
        u_gate = _matmul(hb, wl[:, shift_w + 3 * d:], tm=1024, tn=1024)

        pad_rows = lambda p: jnp.pad(p, ((0, LORA_PAD - p.shape[0]), (0, 0)))
        o_rwkv = _rwkv_mix(u_rkv, u_lora, shift_mu[layer][:3 * c],
                           _pad_lora(shift_mu[layer][3 * c:], d_decay, d_aaa, axis=0),
                           pad_rows(w_lora_up[layer]), w0[layer], pad_rows(a_lora_up[layer]),
                           a0[layer], g_lora_up[layer], k_k[layer], k_a[layer], r_k[layer],
                           ln_x_w[layer], ln_x_b[layer], batch=batch, seq=seq)
        o_moba = _moba(u_qkv, cos, sin_signed, batch=batch, seq=seq, heads=moba_heads)

        mixed = _merge(o_rwkv, o_moba, w_o_rwkv[layer].astype(BF16), w_o_moba[layer].astype(BF16),
                       u_gate, tm=512, tn=1024)
        h = _proj_ln(mixed, w_out[layer].astype(BF16), h, ln1_g[layer], ln1_b[layer],
                     tm=512, alpha=alpha, eps=LN_EPS)

        top_idx, top_w = _router(h, w_router[layer].T, router_bias[layer], tm=512)
        chunks = d // LANES
        codes, blk_e, nxt_e, n_used = _dispatch(top_idx, t, chunks)
        ys = _experts(h.reshape(t * chunks, LANES), codes, blk_e, nxt_e, n_used,
                      w_gate_e[layer], w_up_e[layer], w_down_e[layer])
        h = _shared_ln(h, ys, jnp.repeat(top_w.T, chunks, axis=0), w_gate_s[layer].astype(BF16),
                       w_up_s[layer].astype(BF16), w_down_s[layer].astype(BF16), ln2_g[layer],
                       ln2_b[layer], tm=128, alpha=alpha, eps=LN_EPS)
    return h.reshape(batch, seq, d)
```

```python
import functools

import jax
import jax.numpy as jnp
from jax import lax
from jax.experimental import pallas as pl
from jax.experimental.pallas import tpu as pltpu

F32 = jnp.float32
BF16 = jnp.bfloat16

LANES = 128
WKV_HEAD = 64
WKV_CHUNK = 64


def _dot(a, b):
    return jnp.dot(a.astype(BF16), b.astype(BF16), preferred_element_type=F32)


def _dot_nt(a, b):
    return lax.dot_general(a.astype(BF16), b.astype(BF16), (((1,), (1,)), ((), ())),
                           preferred_element_type=F32)


assert WKV_CHUNK == WKV_HEAD
def _each(f, *lists):
    return [f(*xs) for xs in zip(*lists)]


def _wkv_pairs(r, ld, k, v, a, b, m_state):
    c = WKV_CHUNK
    cat0 = lambda *xs: jnp.concatenate(xs, axis=0)
    cat1 = lambda *xs: jnp.concatenate(xs, axis=1)
    lane = lax.broadcasted_iota(jnp.int32, (c, LANES), 1)
    step = lax.broadcasted_iota(jnp.int32, (c, LANES), 0)
    head0 = lane < WKV_HEAD
    other = jnp.where(head0, lane, lane - WKV_HEAD)

    def stack(x):
        x = x.astype(BF16)
        zero = jnp.zeros_like(x)
        return cat0(jnp.where(head0, x, zero), jnp.where(head0, zero, x))

    def cumsum_rows(x):
        shift = 1
        while shift < c:
            x = x + jnp.where(step >= shift, pltpu.roll(x, shift, axis=0), 0.0)
            shift *= 2
        return x

    dot = lambda p, q: jnp.dot(p.astype(BF16), q, preferred_element_type=F32)

    cw = _each(cumsum_rows, ld)
    e_out = _each(lambda x: jnp.exp(-x), cw)
    a_t = _each(lambda x, w, l: x * jnp.exp(w - l), a, cw, ld)
    r_t = _each(lambda x, w: x * jnp.exp(w), r, cw)
    bk_st = _each(lambda p, q, e: cat0(stack(p * e), stack(q * e)), b, k, e_out)
    v_st = _each(stack, v)
    aa = _each(lambda p, q, s: _dot_nt(cat0(p, q), s), a_t, r_t, bk_st)

    strict = step > other
    incl = step >= other
    n_ab = _each(lambda x: jnp.where(strict, x[:c, :LANES], 0.0), aa)
    a_ak = _each(lambda x: jnp.where(strict, x[:c, LANES:], 0.0), aa)
    a_rb = _each(lambda x: jnp.where(incl, x[c:, :LANES], 0.0), aa)
    a_rk = _each(lambda x: jnp.where(incl, x[c:, LANES:], 0.0), aa)

    eye = jnp.where(step == other, 1.0, 0.0)
    t_inv = _each(lambda n: eye + n, n_ab)
    w1 = _each(dot, a_ak, v_st)
    n_pow = _each(lambda n: dot(n, stack(n)), n_ab)
    for _ in range(c.bit_length() - 3):
        pt = _each(lambda n, t: dot(n, cat1(stack(n), stack(t))), n_pow, t_inv)
        n_pow = _each(lambda x: x[:, :LANES], pt)
        t_inv = _each(lambda t, x: t + x[:, LANES:], t_inv, pt)
    t_inv = _each(lambda t, n: t + dot(n, stack(t)), t_inv, n_pow)

    tw = _each(lambda t, w, x: dot(t, cat1(stack(w), stack(x))), t_inv, w1, a_t)
    rhs = _each(lambda x, v_: cat0(cat1(stack(x[:, LANES:]), stack(x[:, :LANES])),
                                   cat1(jnp.zeros_like(v_), v_)), tw, v_st)
    out_p = _each(lambda p, q, s: dot(cat1(p, q), s), a_rb, a_rk, rhs)
    tail = _each(lambda w: jnp.exp(w[c - 1:c, :] - w), cw)
    bkp_st = _each(lambda p, q, e: cat0(stack(p * e), stack(q * e)), b, k, tail)
    upd = _each(lambda p, s: lax.dot_general(p, s, (((0,), (0,)), ((), ())),
                                             preferred_element_type=F32), bkp_st, rhs)
    upd_p = _each(lambda u: u[:WKV_HEAD] + u[WKV_HEAD:], upd)
    r_hat = _each(lambda x, o: x + o[:, :LANES], r_t, out_p)
    g = _each(lambda w, u: jnp.where(step == other, jnp.exp(w[c - 1:c, :]), 0.0) + u[:, :LANES],
              cw, upd_p)

    def apply_state(lhs, m):
        l_hi = lhs.astype(BF16)
        l_lo = (lhs - l_hi.astype(F32)).astype(BF16)
        m_hi = m.astype(BF16)
        m_lo = m - m_hi.astype(F32)
        both = jnp.dot(l_hi, cat1(stack(m_hi), stack(m_lo)), preferred_element_type=F32)
        return (both[:, :LANES] + both[:, LANES:]) + jnp.dot(l_lo, stack(m_hi),
                                                             preferred_element_type=F32)

    rm = _each(lambda p, q, m: apply_state(cat0(p, q), m), r_hat, g, m_state)
    y = _each(lambda x, o: x[:c] + o[:, LANES:], rm, out_p)
    m_new = _each(lambda x, u: x[c:] + u[:, LANES:], rm, upd_p)
    return y, m_new


RWKV_GN_EPS = 64e-5
LORA_PAD = LANES


def _rwkv_kernel(u_ref, ul_ref, mu_ref, mul_ref, wl_ref, w0_ref, al_ref, a0_ref, gl_ref, kk_ref,
                 ka_ref, rk_ref, lnw_ref, lnb_ref, o_ref, m_ref, prev_ref, prevl_ref):
    c = WKV_CHUNK
    width = o_ref.shape[1]

    @pl.when(pl.program_id(1) == 0)
    def _():
        m_ref[...] = jnp.zeros_like(m_ref)
        prev_ref[...] = jnp.zeros_like(prev_ref)
        prevl_ref[...] = jnp.zeros_like(prevl_ref)

    def shifted(u, prev, mu):
        first = lax.broadcasted_iota(jnp.int32, u.shape, 0) == 0
        before = jnp.where(first, prev[7:8, :], pltpu.roll(u, 1, axis=0))
        return u + (before - u) * mu

    u = u_ref[...]
    ul = ul_ref[...]
    z = shifted(u, prev_ref[...], mu_ref[...])
    zl = shifted(ul, prevl_ref[...], mul_ref[...])
    prev_ref[...] = u[c - 8:, :]
    prevl_ref[...] = ul[c - 8:, :]

    r = z[:, :width]
    k_raw = z[:, width:2 * width]
    v = z[:, 2 * width:]
    w = w0_ref[...] + _dot(jnp.tanh(zl[:, :LORA_PAD]), wl_ref[...])
    ld = -jnp.exp(-(jnp.maximum(-w, 0.0) + jnp.log(1.0 + jnp.exp(-jnp.abs(w)))) - 0.5)
    a = _sigmoid(a0_ref[...] + _dot(zl[:, LORA_PAD:2 * LORA_PAD], al_ref[...]))
    g = _dot(_sigmoid(zl[:, 2 * LORA_PAD:]), gl_ref[...])
    kk_raw = k_raw * kk_ref[...]
    k = k_raw * (1.0 + (a - 1.0) * ka_ref[...])

    lanes = [slice(p * LANES, (p + 1) * LANES) for p in range(width // LANES)]
    split = lambda x: [x[:, sl] for sl in lanes]
    row = lax.broadcasted_iota(jnp.int32, (LANES, LANES), 0)
    col = lax.broadcasted_iota(jnp.int32, (LANES, LANES), 1)
    same_head = jnp.where((row < WKV_HEAD) == (col < WKV_HEAD), 1.0, 0.0).astype(BF16)

    def head_sums(x):
        stacked = jnp.concatenate(split(x), axis=0).astype(BF16)
        s = jnp.dot(stacked, same_head, preferred_element_type=F32)
        return [s[p * c:(p + 1) * c] for p in range(len(lanes))]

    norm2 = head_sums(kk_raw * kk_raw)
    kk = _each(lambda x, s: x * lax.rsqrt(jnp.maximum(s, 1e-24)), split(kk_raw), norm2)
    bonus = head_sums(r * k * rk_ref[...])

    m_state = [m_ref[p] for p in range(len(lanes))]
    y, m_new = _wkv_pairs(split(r), split(ld), split(k), split(v), _each(lambda x: -x, kk),
                          _each(lambda x, a_: x * a_, kk, split(a)), m_state)
    for p in range(len(lanes)):
        m_ref[p] = m_new[p]

    inv_n = 1.0 / WKV_HEAD
    y = jnp.concatenate(y, axis=1)
    mean = jnp.concatenate(head_sums(y), axis=1) * inv_n
    yc = y - mean
    var = jnp.concatenate(head_sums(yc * yc), axis=1) * inv_n
    yn = yc * lax.rsqrt(var + RWKV_GN_EPS) * lnw_ref[...] + lnb_ref[...]
    out = (yn + jnp.concatenate(bonus, axis=1) * v) * g
    o_ref[...] = out.astype(o_ref.dtype)


def _rwkv_mix(u_rkv, u_lora, mu_rkv, mu_lora, w_lora, w0, a_lora, a0, g_lora, k_k, k_a, r_k,
              ln_w, ln_b, *, batch, seq):
    t = u_rkv.shape[0]
    width = u_rkv.shape[1] // 3
    n_lora = u_lora.shape[1]
    assert t == batch * seq and seq % WKV_CHUNK == 0 and width % LANES == 0
    n_chunks = seq // WKV_CHUNK
    rows = lambda n: pl.BlockSpec((WKV_CHUNK, n), lambda bi, ci: (bi * n_chunks + ci, 0))
    whole = lambda x: pl.BlockSpec(x.shape, lambda bi, ci: (0,) * x.ndim)
    vec = lambda x: x.reshape(1, -1)
    params = [vec(mu_rkv), vec(mu_lora), w_lora.astype(BF16), vec(w0), a_lora.astype(BF16), vec(a0),
              g_lora.astype(BF16), vec(k_k), vec(k_a), vec(r_k), vec(ln_w), vec(ln_b)]
    return pl.pallas_call(
        _rwkv_kernel,
        grid=(batch, n_chunks),
        in_specs=[rows(3 * width), rows(n_lora)] + [whole(p) for p in params],
        out_specs=rows(width),
        out_shape=jax.ShapeDtypeStruct((t, width), BF16),
        scratch_shapes=[pltpu.VMEM((width // LANES, WKV_HEAD, LANES), F32),
                        pltpu.VMEM((8, 3 * width), F32), pltpu.VMEM((8, n_lora), F32)],
        compiler_params=pltpu.CompilerParams(
            dimension_semantics=("parallel", "arbitrary"), vmem_limit_bytes=VMEM_LIMIT),
        name="rwkv7_time_mix",
    )(u_rkv, u_lora, *params)


MOBA_BLOCK = 256
MOBA_TOPK = 3
MOBA_HEAD = 128


def _rope(t, cos, sin_signed):
    return t * cos + pltpu.roll(t, MOBA_HEAD // 2, axis=1) * sin_signed


def _moba_kernel(q_ref, k_ref, v_ref, cos_ref, sin_ref, o_ref, kr_ref, vb_ref, kmean_ref, *,
                 nb, scale):
    blk = MOBA_BLOCK
    kmean_ref[...] = jnp.zeros_like(kmean_ref)
    for n in range(nb):
        rows = slice(n * blk, (n + 1) * blk)
        kr = _rope(k_ref[rows, :], cos_ref[rows, :], sin_ref[rows, :])
        kr_ref[rows, :] = kr.astype(BF16)
        vb_ref[rows, :] = v_ref[rows, :].astype(BF16)
        kmean_ref[n:n + 1, :] = jnp.mean(kr, axis=0, keepdims=True)

    lane = lax.broadcasted_iota(jnp.int32, (blk, LANES), 1)
    row = lax.broadcasted_iota(jnp.int32, (blk, blk), 0)
    col = lax.broadcasted_iota(jnp.int32, (blk, blk), 1)
    causal = row >= col
    for i in range(nb):
        rows = slice(i * blk, (i + 1) * blk)
        q = _rope(q_ref[rows, :], cos_ref[rows, :], sin_ref[rows, :])
        width = (i + 1) * blk
        s = _dot_nt(q, kr_ref[:width, :]) * scale
        parts = []
        if i > MOBA_TOPK:
            gate = lax.dot_general(q, kmean_ref[...], (((1,), (1,)), ((), ())),
                                   precision=lax.Precision.HIGHEST, preferred_element_type=F32)
            gate = jnp.where(lane < i, gate, -jnp.inf)
            rank = jnp.zeros((blk, LANES), jnp.int32)
            for m in range(i):
                g_m = gate[:, m:m + 1]
                rank = rank + jnp.where(
                    g_m > gate, 1, jnp.where(g_m == gate, jnp.where(lane > m, 1, 0), 0))
            chosen = jnp.where(rank < MOBA_TOPK, 1.0, 0.0)
            for n in range(i):
                take = chosen[:, n:n + 1] > 0.5
                parts.append(jnp.where(take, s[:, n * blk:(n + 1) * blk], -jnp.inf))
        else:
            parts = [s[:, n * blk:(n + 1) * blk] for n in range(i)]
        parts.append(jnp.where(causal, s[:, i * blk:], -jnp.inf))
        s = jnp.concatenate(parts, axis=1) if i else parts[0]
        m_row = jnp.max(s, axis=1, keepdims=True)
        p = jnp.exp(s - m_row)
        l_row = jnp.sum(p, axis=1, keepdims=True)
        acc = jnp.dot(p.astype(BF16), vb_ref[:width, :], preferred_element_type=F32)
        o_ref[rows, :] = (acc / l_row).astype(o_ref.dtype)


def _moba(u_qkv, cos, sin_signed, *, batch, seq, heads):
    t = u_qkv.shape[0]
    assert seq % MOBA_BLOCK == 0
    nb = seq // MOBA_BLOCK
    assert nb <= LANES
    kernel = functools.partial(_moba_kernel, nb=nb, scale=MOBA_HEAD ** -0.5)
    full = lambda off: pl.BlockSpec((seq, MOBA_HEAD), lambda bi, hi: (bi, off + hi))
    table = pl.BlockSpec((seq, MOBA_HEAD), lambda bi, hi: (0, 0))
    return pl.pallas_call(
        kernel,
        grid=(batch, heads),
        in_specs=[full(0), full(heads), full(2 * heads), table, table],
        out_specs=full(0),
        out_shape=jax.ShapeDtypeStruct((t, heads * MOBA_HEAD), BF16),
        scratch_shapes=[pltpu.VMEM((seq, MOBA_HEAD), BF16), pltpu.VMEM((seq, MOBA_HEAD), BF16),
                        pltpu.VMEM((LANES, MOBA_HEAD), F32)],
        compiler_params=pltpu.CompilerParams(dimension_semantics=("parallel", "parallel")),
        name="moba_attention",
    )(u_qkv, u_qkv, u_qkv, cos, sin_signed)


VMEM_LIMIT = 56 * 1024 * 1024


def _matmul_kernel(x_ref, w_ref, o_ref):
    o_ref[...] = jnp.dot(x_ref[...], w_ref[...], preferred_element_type=F32).astype(o_ref.dtype)


def _matmul(x, w, *, tm, tn, out_dtype=F32):
    m, k = x.shape
    n = w.shape[1]
    assert m % tm == 0 and n % tn == 0
    return pl.pallas_call(
        _matmul_kernel,
        grid=(m // tm, n // tn),
        in_specs=[pl.BlockSpec((tm, k), lambda i, j: (i, 0)),
                  pl.BlockSpec((k, tn), lambda i, j: (0, j))],
        out_specs=pl.BlockSpec((tm, tn), lambda i, j: (i, j)),
        out_shape=jax.ShapeDtypeStruct((m, n), out_dtype),
        compiler_params=pltpu.CompilerParams(
            dimension_semantics=("parallel", "parallel"), vmem_limit_bytes=VMEM_LIMIT),
        name="matmul",
    )(x, w)


def _sigmoid(x):
    return 1.0 / (1.0 + jnp.exp(-x))


def _merge_kernel(a_ref, b_ref, pa_ref, pb_ref, ga_ref, gb_ref, o_ref):
    ya = jnp.dot(a_ref[...], pa_ref[...], preferred_element_type=F32)
    yb = jnp.dot(b_ref[...], pb_ref[...], preferred_element_type=F32)
    o_ref[...] = (_sigmoid(ga_ref[...]) * ya + _sigmoid(gb_ref[...]) * yb).astype(o_ref.dtype)


def _merge(o_a, o_b, p_a, p_b, gates, *, tm, tn):
    m, k = o_a.shape
    n = p_a.shape[1]
    nj = n // tn
    row = pl.BlockSpec((tm, k), lambda i, j: (i, 0))
    wcol = pl.BlockSpec((k, tn), lambda i, j: (0, j))
    return pl.pallas_call(
        _merge_kernel,
        grid=(m // tm, nj),
        in_specs=[row, row, wcol, wcol,
                  pl.BlockSpec((tm, tn), lambda i, j: (i, j)),
                  pl.BlockSpec((tm, tn), lambda i, j: (i, nj + j))],
        out_specs=pl.BlockSpec((tm, tn), lambda i, j: (i, j)),
        out_shape=jax.ShapeDtypeStruct((m, n), BF16),
        compiler_params=pltpu.CompilerParams(
            dimension_semantics=("parallel", "parallel"), vmem_limit_bytes=VMEM_LIMIT),
        name="gated_merge",
    )(o_a, o_b, p_a, p_b, gates, gates)


def _layer_norm(x, g, b, eps):
    mu = jnp.mean(x, axis=-1, keepdims=True)
    xc = x - mu
    var = jnp.mean(xc * xc, axis=-1, keepdims=True)
    return xc * lax.rsqrt(var + eps) * g + b


def _proj_ln_kernel(m_ref, w_ref, x_ref, g_ref, b_ref, o_ref, *, alpha, eps):
    y = jnp.dot(m_ref[...], w_ref[...], preferred_element_type=F32)
    o_ref[...] = _layer_norm(alpha * x_ref[...] + y, g_ref[...], b_ref[...], eps)


def _proj_ln(mixed, w, x, g, b, *, tm, alpha, eps):
    m, k = mixed.shape
    n = w.shape[1]
    vec = pl.BlockSpec((1, n), lambda i: (0, 0))
    return pl.pallas_call(
        functools.partial(_proj_ln_kernel, alpha=alpha, eps=eps),
        grid=(m // tm,),
        in_specs=[pl.BlockSpec((tm, k), lambda i: (i, 0)),
                  pl.BlockSpec((k, n), lambda i: (0, 0)),
                  pl.BlockSpec((tm, n), lambda i: (i, 0)), vec, vec],
        out_specs=pl.BlockSpec((tm, n), lambda i: (i, 0)),
        out_shape=jax.ShapeDtypeStruct((m, n), F32),
        compiler_params=pltpu.CompilerParams(
            dimension_semantics=("parallel",), vmem_limit_bytes=VMEM_LIMIT),
        name="out_proj_layernorm",
    )(mixed, w, x, g.reshape(1, n), b.reshape(1, n))


N_EXPERTS = 256
N_GROUPS = 8
TOPK_GROUPS = 4
TOP_K = 8
ROUTED_SCALE = 2.5


def _router_kernel(h_ref, wt_ref, bias_ref, idx_ref, wgt_ref):
    tm = h_ref.shape[0]
    per = N_EXPERTS // N_GROUPS
    logits = lax.dot_general(wt_ref[...], h_ref[...], (((1,), (1,)), ((), ())),
                             precision=lax.Precision.HIGHEST, preferred_element_type=F32)
    scores = _sigmoid(logits)
    biased = scores + bias_ref[...]
    neg = -jnp.inf

    grp = biased.reshape(N_GROUPS, per, tm)
    slot = lax.broadcasted_iota(jnp.int32, (N_GROUPS, per, tm), 1)
    top1 = jnp.max(grp, axis=1, keepdims=True)
    first = jnp.min(jnp.where(grp == top1, slot, per), axis=1, keepdims=True)
    top2 = jnp.max(jnp.where(slot == first, neg, grp), axis=1, keepdims=True)
    grp_score = (top1 + top2).reshape(N_GROUPS, tm)

    gid = lax.broadcasted_iota(jnp.int32, (N_GROUPS, tm), 0)
    rank = jnp.zeros((N_GROUPS, tm), jnp.int32)
    for m in range(N_GROUPS):
        s_m = grp_score[m:m + 1, :]
        rank = rank + jnp.where(s_m > grp_score, 1,
                                jnp.where(s_m == grp_score, jnp.where(gid > m, 1, 0), 0))
    keep = (rank < TOPK_GROUPS).reshape(N_GROUPS, 1, tm)
    masked = jnp.where(keep, grp, neg).reshape(N_EXPERTS, tm)

    eid = lax.broadcasted_iota(jnp.int32, (N_EXPERTS, tm), 0)
    idxs, wgts = [], []
    for _ in range(TOP_K):
        best = jnp.max(masked, axis=0, keepdims=True)
        pick = jnp.min(jnp.where(masked == best, eid, N_EXPERTS), axis=0, keepdims=True)
        hit = eid == pick
        idxs.append(pick)
        wgts.append(jnp.sum(jnp.where(hit, scores, 0.0), axis=0, keepdims=True))
        masked = jnp.where(hit, neg, masked)
    wgt = jnp.concatenate(wgts, axis=0)
    wgt = wgt / jnp.sum(wgt, axis=0, keepdims=True) * ROUTED_SCALE
    idx_ref[...] = jnp.concatenate(idxs, axis=0)
    wgt_ref[...] = wgt


def _router(h, w_router_t, bias, *, tm):
    t, d = h.shape
    out = pl.BlockSpec((TOP_K, tm), lambda i: (0, i))
    return pl.pallas_call(
        _router_kernel,
        grid=(t // tm,),
        in_specs=[pl.BlockSpec((tm, d), lambda i: (i, 0)),
                  pl.BlockSpec((N_EXPERTS, d), lambda i: (0, 0)),
                  pl.BlockSpec((N_EXPERTS, 1), lambda i: (0, 0))],
        out_specs=[out, out],
        out_shape=[jax.ShapeDtypeStruct((TOP_K, t), jnp.int32),
                   jax.ShapeDtypeStruct((TOP_K, t), F32)],
        compiler_params=pltpu.CompilerParams(
            dimension_semantics=("parallel",), vmem_limit_bytes=VMEM_LIMIT),
        name="moe_router",
    )(h, w_router_t, bias.reshape(N_EXPERTS, 1))


EXPERT_ROWS = 128


def _silu(x):
    return x * _sigmoid(x)


GATHER_AHEAD = 3
X_SLOTS = 4
Y_SLOTS = 4
SCATTER_LAG = 3
IDX_SLOTS = 8


def _experts_kernel(blk_e_ref, nxt_e_ref, n_used_ref,
                    codes_hbm, h_hbm, wg_hbm, wu_hbm, wd_hbm, ys_hbm,
                    idx_ref, x_buf, y_buf, wg_st, wu_st, wd_st, wg_bf, wu_bf, wd_bf,
                    idx_sem, gat_sem, sct_sem, w_sem, *, n_tokens):
    i = pl.program_id(0)
    n_blk = pl.num_programs(0)
    n_used = n_used_ref[0]
    used = i < n_used
    rows = EXPERT_ROWS
    x_chunks = x_buf.shape[2]
    chunks = y_buf.shape[2]

    def idx_copy(k):
        slot = k & (IDX_SLOTS - 1)
        return pltpu.make_async_copy(codes_hbm.at[pl.ds(k, 1), :], idx_ref.at[pl.ds(slot, 1), :],
                                     idx_sem.at[slot])

    def start_rows(k, gather):
        window = k & (IDX_SLOTS - 1)
        for j in range(rows):
            if gather:
                buf = k & (X_SLOTS - 1)
                src = idx_ref[window, j]
                pltpu.make_async_copy(h_hbm.at[pl.ds(src, x_chunks), :], x_buf.at[buf, j],
                                      gat_sem.at[buf]).start()
            else:
                buf = k & (Y_SLOTS - 1)
                dst = idx_ref[window, rows + j]
                pltpu.make_async_copy(y_buf.at[buf, j], ys_hbm.at[pl.ds(dst, chunks), :],
                                      sct_sem.at[buf]).start()

    def wait_gather(k):
        buf = k & (X_SLOTS - 1)
        pltpu.make_async_copy(x_buf.at[buf], x_buf.at[buf], gat_sem.at[buf]).wait()

    def wait_scatter(k):
        buf = k & (Y_SLOTS - 1)
        pltpu.make_async_copy(y_buf.at[buf], y_buf.at[buf], sct_sem.at[buf]).wait()

    def weight_copies(e):
        return (pltpu.make_async_copy(wg_hbm.at[e], wg_st, w_sem.at[0]),
                pltpu.make_async_copy(wu_hbm.at[e], wu_st, w_sem.at[1]),
                pltpu.make_async_copy(wd_hbm.at[e], wd_st, w_sem.at[2]))

    def compute(k):
        u = pltpu.einshape("jcl->cjl", x_buf[k & (X_SLOTS - 1)])
        u = jnp.concatenate([u[c] for c in range(x_chunks)], axis=1)
        lo = lax.bitcast_convert_type(u << 16, F32)
        hi = lax.bitcast_convert_type(u & jnp.uint32(0xFFFF0000), F32)
        x = jnp.concatenate([lo, hi], axis=1).astype(BF16)
        hid = _silu(jnp.dot(x, wg_bf[...], preferred_element_type=F32)) * jnp.dot(
            x, wu_bf[...], preferred_element_type=F32)
        y = jnp.dot(hid.astype(BF16), wd_bf[...], preferred_element_type=F32)
        y = jnp.stack([y[:, c * LANES:(c + 1) * LANES] for c in range(chunks)], axis=0)
        y_buf[k & (Y_SLOTS - 1)] = pltpu.einshape("cjl->jcl", y)

    @pl.when(i == 0)
    def _():
        for cp in weight_copies(blk_e_ref[0]):
            cp.start(priority=1)
        for k in range(GATHER_AHEAD):
            @pl.when(k < n_used)
            def _():
                idx_copy(k).start()
                idx_copy(k).wait()
                start_rows(k, gather=True)

        @pl.when(GATHER_AHEAD < n_used)
        def _():
            idx_copy(GATHER_AHEAD).start()

    @pl.when(i + GATHER_AHEAD + 1 < n_used)
    def _():
        idx_copy(i + GATHER_AHEAD + 1).start()

    has_ahead = i + GATHER_AHEAD < n_used

    @pl.when(has_ahead)
    def _():
        idx_copy(i + GATHER_AHEAD).wait()

    prev_e = blk_e_ref[jnp.maximum(i - 1, 0)]
    fresh = jnp.logical_or(i == 0, blk_e_ref[i] != prev_e)

    @pl.when(jnp.logical_and(used, fresh))
    def _():
        for cp in weight_copies(blk_e_ref[i]):
            cp.wait()
        wg_bf[...] = wg_st[...].astype(BF16)
        wu_bf[...] = wu_st[...].astype(BF16)
        wd_bf[...] = wd_st[...].astype(BF16)

        @pl.when(nxt_e_ref[i] >= 0)
        def _():
            for cp in weight_copies(nxt_e_ref[i]):
                cp.start(priority=1)

    @pl.when(jnp.logical_and(i >= SCATTER_LAG, i - SCATTER_LAG < n_used))
    def _():
        wait_scatter(i - SCATTER_LAG)

    interior = jnp.logical_and(i >= 1, has_ahead)

    @pl.when(interior)
    def _():
        wait_gather(i)
        start_rows(i - 1, gather=False)
        start_rows(i + GATHER_AHEAD, gather=True)
        compute(i)

    @pl.when(jnp.logical_and(used, jnp.logical_not(interior)))
    def _():
        @pl.when(has_ahead)
        def _():
            start_rows(i + GATHER_AHEAD, gather=True)

        @pl.when(i >= 1)
        def _():
            start_rows(i - 1, gather=False)

        wait_gather(i)
        compute(i)

        @pl.when(i + 1 == n_used)
        def _():
            start_rows(i, gather=False)

    @pl.when(i == n_blk - 1)
    def _():
        for back in range(SCATTER_LAG - 1, -1, -1):
            @pl.when(jnp.logical_and(i >= back, i - back < n_used))
            def _():
                wait_scatter(i - back)

        y_buf[...] = jnp.zeros_like(y_buf)
        for half in range(2):
            for j in range(rows):
                first = (n_tokens * TOP_K + half * rows + j) * chunks
                pltpu.make_async_copy(y_buf.at[half, j], ys_hbm.at[pl.ds(first, chunks), :],
                                      sct_sem.at[half]).start()
            wait_scatter(half)


def _pack_rows(h):
    t, d = h.shape
    bits = lax.bitcast_convert_type(h.astype(BF16), jnp.uint16).astype(jnp.uint32)
    words = bits[:, :d // 2] | (bits[:, d // 2:] << 16)
    return words.reshape(t * d // (2 * LANES), LANES)


def _experts(h, codes, blk_e, nxt_e, n_used, w_gate, w_up, w_down):
    d, ff = w_gate.shape[1], w_gate.shape[2]
    chunks = d // LANES
    x_chunks = d // (2 * LANES)
    t = h.shape[0] // x_chunks
    n_blk = blk_e.shape[0]
    any_spec = pl.BlockSpec(memory_space=pl.ANY)
    grid_spec = pltpu.PrefetchScalarGridSpec(
        num_scalar_prefetch=3,
        grid=(n_blk,),
        in_specs=[any_spec] * 5,
        out_specs=any_spec,
        scratch_shapes=[pltpu.SMEM((IDX_SLOTS, 2 * EXPERT_ROWS), jnp.int32),
                        pltpu.VMEM((X_SLOTS, EXPERT_ROWS, x_chunks, LANES), jnp.uint32),
                        pltpu.VMEM((Y_SLOTS, EXPERT_ROWS, chunks, LANES), F32),
                        pltpu.VMEM((d, ff), F32), pltpu.VMEM((d, ff), F32), pltpu.VMEM((ff, d), F32),
                        pltpu.VMEM((d, ff), BF16), pltpu.VMEM((d, ff), BF16),
                        pltpu.VMEM((ff, d), BF16),
                        pltpu.SemaphoreType.DMA((IDX_SLOTS,)), pltpu.SemaphoreType.DMA((X_SLOTS,)),
                        pltpu.SemaphoreType.DMA((Y_SLOTS,)), pltpu.SemaphoreType.DMA((3,))])
    return pl.pallas_call(
        functools.partial(_experts_kernel, n_tokens=t),
        grid_spec=grid_spec,
        out_shape=jax.ShapeDtypeStruct(((t * TOP_K + 2 * EXPERT_ROWS) * chunks, LANES), F32),
        compiler_params=pltpu.CompilerParams(
            dimension_semantics=("arbitrary",), vmem_limit_bytes=VMEM_LIMIT),
        name="routed_experts",
    )(blk_e, nxt_e, n_used, codes, h, w_gate, w_up, w_down)


def _shared_ln_kernel(h_ref, *refs, alpha, eps):
    ys_refs = refs[:TOP_K]
    tw_ref, wg_ref, wu_ref, wd_ref, g_ref, b_ref, o_ref, routed_ref = refs[TOP_K:]
    h = h_ref[...]
    tm, d = h.shape
    chunks = d // LANES
    hb = h.astype(BF16)
    hid = _silu(jnp.dot(hb, wg_ref[...], preferred_element_type=F32)) * jnp.dot(
        hb, wu_ref[...], preferred_element_type=F32)
    moe = jnp.dot(hid.astype(BF16), wd_ref[...], preferred_element_type=F32)
    routed = ys_refs[0][...] * tw_ref[:, 0:1]
    for k in range(1, TOP_K):
        routed = routed + ys_refs[k][...] * tw_ref[:, k:k + 1]
    routed_ref[...] = routed
    moe = moe + jnp.concatenate(
        [routed_ref[pl.ds(c, tm, stride=chunks), :] for c in range(chunks)], axis=1)
    o_ref[...] = _layer_norm(alpha * h + moe, g_ref[...], b_ref[...], eps)


def _shared_ln(h, ys, top_w, w_gate, w_up, w_down, g, b, *, tm, alpha, eps):
    t, d = h.shape
    ff = w_gate.shape[1]
    chunks = d // LANES
    n_tiles = t // tm
    row = pl.BlockSpec((tm, d), lambda i: (i, 0))
    vec = pl.BlockSpec((1, d), lambda i: (0, 0))
    band = lambda k: pl.BlockSpec((tm * chunks, LANES), lambda i: (k * n_tiles + i, 0))
    return pl.pallas_call(
        functools.partial(_shared_ln_kernel, alpha=alpha, eps=eps),
        grid=(n_tiles,),
        in_specs=[row] + [band(k) for k in range(TOP_K)] + [
            pl.BlockSpec((tm * chunks, TOP_K), lambda i: (i, 0)),
            pl.BlockSpec((d, ff), lambda i: (0, 0)), pl.BlockSpec((d, ff), lambda i: (0, 0)),
            pl.BlockSpec((ff, d), lambda i: (0, 0)), vec, vec],
        out_specs=row,
        out_shape=jax.ShapeDtypeStruct((t, d), F32),
        scratch_shapes=[pltpu.VMEM((tm * chunks, LANES), F32)],
        compiler_params=pltpu.CompilerParams(
            dimension_semantics=("parallel",), vmem_limit_bytes=VMEM_LIMIT),
        name="shared_expert_layernorm",
    )(h, *([ys] * TOP_K), top_w, w_gate, w_up, w_down, g.reshape(1, d), b.reshape(1, d))


LN_EPS = 1e-5
ROPE_THETA = 10000.0


def _rope_tables(seq, dim):
    inv = 1.0 / (ROPE_THETA ** (jnp.arange(0, dim, 2, dtype=F32) / dim))
    ang = jnp.arange(seq, dtype=F32)[:, None] * inv[None, :]
    ang = jnp.concatenate([ang, ang], axis=-1)
    sign = jnp.where(jnp.arange(dim) < dim // 2, -1.0, 1.0).astype(F32)
    return jnp.cos(ang), jnp.sin(ang) * sign


def _pad_lora(x, d_decay, d_aaa, axis):
    assert d_decay <= LORA_PAD and d_aaa <= LORA_PAD
    decay, aaa, gate = jnp.split(x, [d_decay, d_decay + d_aaa], axis=axis)
    pad = lambda p: jnp.pad(p, [(0, LORA_PAD - p.shape[a]) if a == axis % x.ndim else (0, 0)
                                for a in range(x.ndim)])
    return jnp.concatenate([pad(decay), pad(aaa), gate], axis=axis)


def _dispatch(top_idx, n_tokens, src_stride, dst_stride):
    n_assign = n_tokens * TOP_K
    i32 = jnp.int32
    e_flat = top_idx.reshape(n_assign).astype(i32)
    sorted_e, order = lax.sort((e_flat, jnp.arange(n_assign, dtype=i32)), num_keys=1, is_stable=True)
    experts = jnp.arange(N_EXPERTS, dtype=i32)
    grp_start = jnp.searchsorted(sorted_e, experts, side='left').astype(i32)
    grp_end = jnp.searchsorted(sorted_e, experts, side='right').astype(i32)
    counts = grp_end - grp_start
    padded = (counts + EXPERT_ROWS - 1) // EXPERT_ROWS * EXPERT_ROWS
    pad_end = jnp.cumsum(padded)
    pad_start = pad_end - padded
    n_blk = -(-n_assign // EXPERT_ROWS) + N_EXPERTS
    n_used = (pad_end[-1] // EXPERT_ROWS).astype(i32)
    first_row = jnp.arange(n_blk, dtype=i32) * EXPERT_ROWS
    blk_e = jnp.minimum(jnp.searchsorted(pad_end, first_row, side='right'), N_EXPERTS - 1).astype(i32)
    into = first_row - pad_start[blk_e]
    blk_s0 = jnp.clip(grp_start[blk_e] + into, 0, n_assign)
    blk_nv = jnp.clip(counts[blk_e] - into, 0, EXPERT_ROWS)
    later = jnp.where(counts > 0, experts, N_EXPERTS)
    nxt = lax.cummin(jnp.concatenate([later[1:], jnp.full((1,), N_EXPERTS, i32)]), reverse=True)
    nxt_e = jnp.where(nxt < N_EXPERTS, nxt, -1)[blk_e]
    row = jnp.arange(EXPERT_ROWS, dtype=i32)[None, :]
    src = jnp.minimum(blk_s0[:, None] + row, n_assign - 1)
    pad = n_assign + (jnp.arange(n_blk, dtype=i32)[:, None] % 2) * EXPERT_ROWS + row
    valid = row < blk_nv[:, None]
    ids = order[src]
    codes = jnp.concatenate([jnp.where(valid, ids % n_tokens, 0) * src_stride,
                             jnp.where(valid, ids, pad) * dst_stride], axis=1)
    return codes.astype(i32), blk_e, nxt_e.astype(i32), n_used.reshape(1)


def kernel(x, w_in, shift_mu, w_lora_up, w0, a_lora_up, a0, g_lora_up, k_k, k_a, r_k, ln_x_w, ln_x_b,
           w_o_rwkv, w_o_moba, w_out, ln1_g, ln1_b, w_router, router_bias, w_gate_e, w_up_e,
           w_down_e, w_gate_s, w_up_s, w_down_s, ln2_g, ln2_b):
    batch, seq, d = x.shape
    depth = w_in.shape[0]
    alpha = (2 * depth) ** 0.25
    t = batch * seq
    c = d
    shift_w = shift_mu.shape[1]
    moba_heads = d // MOBA_HEAD
    cos, sin_signed = _rope_tables(seq, MOBA_HEAD)

    h = x.reshape(t, d)
    for layer in range(depth):
        wl = w_in[layer].astype(BF16)
        hb = h.astype(BF16)
        d_decay, d_aaa = w_lora_up.shape[1], a_lora_up.shape[1]
        w_lora = _pad_lora(wl[:, 3 * c:shift_w], d_decay, d_aaa, axis=1)
        assert w_lora.shape[1] % LANES == 0
        u_rkv = _matmul(hb, wl[:, :3 * c], tm=1024, tn=1024)
        u_lora = _matmul(hb, w_lora, tm=1024, tn=w_lora.shape[1])
        u_qkv = _matmul(hb, wl[:, shift_w:shift_w + 3 * d], tm=1024, tn=1024)
        u_gate = _matmul(hb, wl[:, shift_w + 3 * d:], tm=1024, tn=1024)

        pad_rows = lambda p: jnp.pad(p, ((0, LORA_PAD - p.shape[0]), (0, 0)))
        o_rwkv = _rwkv_mix(u_rkv, u_lora, shift_mu[layer][:3 * c],
                           _pad_lora(shift_mu[layer][3 * c:], d_decay, d_aaa, axis=0),
                           pad_rows(w_lora_up[layer]), w0[layer], pad_rows(a_lora_up[layer]),
                           a0[layer], g_lora_up[layer], k_k[layer], k_a[layer], r_k[layer],
                           ln_x_w[layer], ln_x_b[layer], batch=batch, seq=seq)
        o_moba = _moba(u_qkv, cos, sin_signed, batch=batch, seq=seq, heads=moba_heads)

        mixed = _merge(o_rwkv, o_moba, w_o_rwkv[layer].astype(BF16), w_o_moba[layer].astype(BF16),
                       u_gate, tm=512, tn=1024)
        h = _proj_ln(mixed, w_out[layer].astype(BF16), h, ln1_g[layer], ln1_b[layer],
                     tm=512, alpha=alpha, eps=LN_EPS)

        top_idx, top_w = _router(h, w_router[layer].T, router_bias[layer], tm=512)
        chunks = d // LANES
        codes, blk_e, nxt_e, n_used = _dispatch(top_idx, t, chunks // 2, chunks)
        ys = _experts(_pack_rows(h), codes, blk_e, nxt_e, n_used,
                      w_gate_e[layer], w_up_e[layer], w_down_e[layer])
        h = _shared_ln(h, ys, jnp.repeat(top_w.T, chunks, axis=0), w_gate_s[layer].astype(BF16),
                       w_up_s[layer].astype(BF16), w_down_s[layer].astype(BF16), ln2_g[layer],
                       ln2_b[layer], tm=128, alpha=alpha, eps=LN_EPS)
    return h.reshape(batch, seq, d)
```

```python
import functools

import jax
import jax.numpy as jnp
from jax import lax
from jax.experimental import pallas as pl
from jax.experimental.pallas import tpu as pltpu

F32 = jnp.float32
BF16 = jnp.bfloat16

LANES = 128
WKV_HEAD = 64
WKV_CHUNK = 64


def _dot(a, b):
    return jnp.dot(a.astype(BF16), b.astype(BF16), preferred_element_type=F32)


def _dot_nt(a, b):
    return lax.dot_general(a.astype(BF16), b.astype(BF16), (((1,), (1,)), ((), ())),
                           preferred_element_type=F32)


assert WKV_CHUNK == WKV_HEAD
def _each(f, *lists):
    return [f(*xs) for xs in zip(*lists)]


def _wkv_pairs(r, ld, k, v, a, b, m_state):
    c = WKV_CHUNK
    cat0 = lambda *xs: jnp.concatenate(xs, axis=0)
    cat1 = lambda *xs: jnp.concatenate(xs, axis=1)
    lane = lax.broadcasted_iota(jnp.int32, (c, LANES), 1)
    step = lax.broadcasted_iota(jnp.int32, (c, LANES), 0)
    head0 = lane < WKV_HEAD
    other = jnp.where(head0, lane, lane - WKV_HEAD)

    def stack(x):
        x = x.astype(BF16)
        zero = jnp.zeros_like(x)
        return cat0(jnp.where(head0, x, zero), jnp.where(head0, zero, x))

    def cumsum_rows(x):
        shift = 1
        while shift < c:
            x = x + jnp.where(step >= shift, pltpu.roll(x, shift, axis=0), 0.0)
            shift *= 2
        return x

    dot = lambda p, q: jnp.dot(p.astype(BF16), q, preferred_element_type=F32)

    cw = _each(cumsum_rows, ld)
    e_out = _each(lambda x: jnp.exp(-x), cw)
    a_t = _each(lambda x, w, l: x * jnp.exp(w - l), a, cw, ld)
    r_t = _each(lambda x, w: x * jnp.exp(w), r, cw)
    bk_st = _each(lambda p, q, e: cat0(stack(p * e), stack(q * e)), b, k, e_out)
    v_st = _each(stack, v)
    aa = _each(lambda p, q, s: _dot_nt(cat0(p, q), s), a_t, r_t, bk_st)

    strict = step > other
    incl = step >= other
    n_ab = _each(lambda x: jnp.where(strict, x[:c, :LANES], 0.0), aa)
    a_ak = _each(lambda x: jnp.where(strict, x[:c, LANES:], 0.0), aa)
    a_rb = _each(lambda x: jnp.where(incl, x[c:, :LANES], 0.0), aa)
    a_rk = _each(lambda x: jnp.where(incl, x[c:, LANES:], 0.0), aa)

    eye = jnp.where(step == other, 1.0, 0.0)
    t_inv = _each(lambda n: eye + n, n_ab)
    w1 = _each(dot, a_ak, v_st)
    n_pow = _each(lambda n: dot(n, stack(n)), n_ab)
    for _ in range(c.bit_length() - 3):
        pt = _each(lambda n, t: dot(n, cat1(stack(n), stack(t))), n_pow, t_inv)
        n_pow = _each(lambda x: x[:, :LANES], pt)
        t_inv = _each(lambda t, x: t + x[:, LANES:], t_inv, pt)
    t_inv = _each(lambda t, n: t + dot(n, stack(t)), t_inv, n_pow)

    tw = _each(lambda t, w, x: dot(t, cat1(stack(w), stack(x))), t_inv, w1, a_t)
    rhs = _each(lambda x, v_: cat0(cat1(stack(x[:, LANES:]), stack(x[:, :LANES])),
                                   cat1(jnp.zeros_like(v_), v_)), tw, v_st)
    out_p = _each(lambda p, q, s: dot(cat1(p, q), s), a_rb, a_rk, rhs)
    tail = _each(lambda w: jnp.exp(w[c - 1:c, :] - w), cw)
    bkp_st = _each(lambda p, q, e: cat0(stack(p * e), stack(q * e)), b, k, tail)
    upd = _each(lambda p, s: lax.dot_general(p, s, (((0,), (0,)), ((), ())),
                                             preferred_element_type=F32), bkp_st, rhs)
    upd_p = _each(lambda u: u[:WKV_HEAD] + u[WKV_HEAD:], upd)
    r_hat = _each(lambda x, o: x + o[:, :LANES], r_t, out_p)
    g = _each(lambda w, u: jnp.where(step == other, jnp.exp(w[c - 1:c, :]), 0.0) + u[:, :LANES],
              cw, upd_p)

    def apply_state(lhs, m):
        l_hi = lhs.astype(BF16)
        l_lo = (lhs - l_hi.astype(F32)).astype(BF16)
        m_hi = m.astype(BF16)
        m_lo = m - m_hi.astype(F32)
        both = jnp.dot(l_hi, cat1(stack(m_hi), stack(m_lo)), preferred_element_type=F32)
        return (both[:, :LANES] + both[:, LANES:]) + jnp.dot(l_lo, stack(m_hi),
                                                             preferred_element_type=F32)

    rm = _each(lambda p, q, m: apply_state(cat0(p, q), m), r_hat, g, m_state)
    y = _each(lambda x, o: x[:c] + o[:, LANES:], rm, out_p)
    m_new = _each(lambda x, u: x[c:] + u[:, LANES:], rm, upd_p)
    return y, m_new


RWKV_GN_EPS = 64e-5
LORA_PAD = LANES


def _rwkv_kernel(u_ref, ul_ref, mu_ref, mul_ref, wl_ref, w0_ref, al_ref, a0_ref, gl_ref, kk_ref,
                 ka_ref, rk_ref, lnw_ref, lnb_ref, o_ref, m_ref, prev_ref, prevl_ref):
    c = WKV_CHUNK
    width = o_ref.shape[1]

    @pl.when(pl.program_id(1) == 0)
    def _():
        m_ref[...] = jnp.zeros_like(m_ref)
        prev_ref[...] = jnp.zeros_like(prev_ref)
        prevl_ref[...] = jnp.zeros_like(prevl_ref)

    def shifted(u, prev, mu):
        first = lax.broadcasted_iota(jnp.int32, u.shape, 0) == 0
        before = jnp.where(first, prev[7:8, :], pltpu.roll(u, 1, axis=0))
        return u + (before - u) * mu

    u = u_ref[...]
    ul = ul_ref[...]
    z = shifted(u, prev_ref[...], mu_ref[...])
    zl = shifted(ul, prevl_ref[...], mul_ref[...])
    prev_ref[...] = u[c - 8:, :]
    prevl_ref[...] = ul[c - 8:, :]

    r = z[:, :width]
    k_raw = z[:, width:2 * width]
    v = z[:, 2 * width:]
    w = w0_ref[...] + _dot(jnp.tanh(zl[:, :LORA_PAD]), wl_ref[...])
    ld = -jnp.exp(-(jnp.maximum(-w, 0.0) + jnp.log(1.0 + jnp.exp(-jnp.abs(w)))) - 0.5)
    a = _sigmoid(a0_ref[...] + _dot(zl[:, LORA_PAD:2 * LORA_PAD], al_ref[...]))
    g = _dot(_sigmoid(zl[:, 2 * LORA_PAD:]), gl_ref[...])
    kk_raw = k_raw * kk_ref[...]
    k = k_raw * (1.0 + (a - 1.0) * ka_ref[...])

    lanes = [slice(p * LANES, (p + 1) * LANES) for p in range(width // LANES)]
    split = lambda x: [x[:, sl] for sl in lanes]
    row = lax.broadcasted_iota(jnp.int32, (LANES, LANES), 0)
    col = lax.broadcasted_iota(jnp.int32, (LANES, LANES), 1)
    same_head = jnp.where((row < WKV_HEAD) == (col < WKV_HEAD), 1.0, 0.0).astype(BF16)

    def head_sums(x):
        stacked = jnp.concatenate(split(x), axis=0).astype(BF16)
        s = jnp.dot(stacked, same_head, preferred_element_type=F32)
        return [s[p * c:(p + 1) * c] for p in range(len(lanes))]

    norm2 = head_sums(kk_raw * kk_raw)
    kk = _each(lambda x, s: x * lax.rsqrt(jnp.maximum(s, 1e-24)), split(kk_raw), norm2)
    bonus = head_sums(r * k * rk_ref[...])

    m_state = [m_ref[p] for p in range(len(lanes))]
    y, m_new = _wkv_pairs(split(r), split(ld), split(k), split(v), _each(lambda x: -x, kk),
                          _each(lambda x, a_: x * a_, kk, split(a)), m_state)
    for p in range(len(lanes)):
        m_ref[p] = m_new[p]

    inv_n = 1.0 / WKV_HEAD
    y = jnp.concatenate(y, axis=1)
    mean = jnp.concatenate(head_sums(y), axis=1) * inv_n
    yc = y - mean
    var = jnp.concatenate(head_sums(yc * yc), axis=1) * inv_n
    yn = yc * lax.rsqrt(var + RWKV_GN_EPS) * lnw_ref[...] + lnb_ref[...]
    out = (yn + jnp.concatenate(bonus, axis=1) * v) * g
    o_ref[...] = out.astype(o_ref.dtype)


def _rwkv_mix(u_rkv, u_lora, mu_rkv, mu_lora, w_lora, w0, a_lora, a0, g_lora, k_k, k_a, r_k,
              ln_w, ln_b, *, batch, seq):
    t = u_rkv.shape[0]
    width = u_rkv.shape[1] // 3
    n_lora = u_lora.shape[1]
    assert t == batch * seq and seq % WKV_CHUNK == 0 and width % LANES == 0
    n_chunks = seq // WKV_CHUNK
    rows = lambda n: pl.BlockSpec((WKV_CHUNK, n), lambda bi, ci: (bi * n_chunks + ci, 0))
    whole = lambda x: pl.BlockSpec(x.shape, lambda bi, ci: (0,) * x.ndim)
    vec = lambda x: x.reshape(1, -1)
    params = [vec(mu_rkv), vec(mu_lora), w_lora.astype(BF16), vec(w0), a_lora.astype(BF16), vec(a0),
              g_lora.astype(BF16), vec(k_k), vec(k_a), vec(r_k), vec(ln_w), vec(ln_b)]
    return pl.pallas_call(
        _rwkv_kernel,
        grid=(batch, n_chunks),
        in_specs=[rows(3 * width), rows(n_lora)] + [whole(p) for p in params],
        out_specs=rows(width),
        out_shape=jax.ShapeDtypeStruct((t, width), BF16),
        scratch_shapes=[pltpu.VMEM((width // LANES, WKV_HEAD, LANES), F32),
                        pltpu.VMEM((8, 3 * width), F32), pltpu.VMEM((8, n_lora), F32)],
        compiler_params=pltpu.CompilerParams(
            dimension_semantics=("parallel", "arbitrary"), vmem_limit_bytes=VMEM_LIMIT),
        name="rwkv7_time_mix",
    )(u_rkv, u_lora, *params)


MOBA_BLOCK = 256
MOBA_TOPK = 3
MOBA_HEAD = 128


def _rope(t, cos, sin_signed):
    return t * cos + pltpu.roll(t, MOBA_HEAD // 2, axis=1) * sin_signed


MOBA_HEADS_PER_STEP = 2


def _moba_kernel(q_ref, k_ref, v_ref, cos_ref, sin_ref, o_ref, kr_ref, vb_ref, kmean_ref, *,
                 nb, scale):
    blk = MOBA_BLOCK
    heads = [slice(g * MOBA_HEAD, (g + 1) * MOBA_HEAD) for g in range(MOBA_HEADS_PER_STEP)]
    kmean_ref[...] = jnp.zeros_like(kmean_ref)
    for n in range(nb):
        rows = slice(n * blk, (n + 1) * blk)
        for g, hd in enumerate(heads):
            kr = _rope(k_ref[rows, hd], cos_ref[rows, :], sin_ref[rows, :])
            kr_ref[g, rows, :] = kr.astype(BF16)
            vb_ref[g, rows, :] = v_ref[rows, hd].astype(BF16)
            kmean_ref[g, n:n + 1, :] = jnp.mean(kr, axis=0, keepdims=True)

    lane = lax.broadcasted_iota(jnp.int32, (blk, LANES), 1)
    row = lax.broadcasted_iota(jnp.int32, (blk, blk), 0)
    col = lax.broadcasted_iota(jnp.int32, (blk, blk), 1)
    causal = row >= col

    def masked_scores(i, g, hd):
        rows = slice(i * blk, (i + 1) * blk)
        q = _rope(q_ref[rows, hd], cos_ref[rows, :], sin_ref[rows, :])
        s = _dot_nt(q, kr_ref[g, :(i + 1) * blk, :]) * scale
        if i > MOBA_TOPK:
            gate = lax.dot_general(q, kmean_ref[g], (((1,), (1,)), ((), ())),
                                   precision=lax.Precision.HIGHEST, preferred_element_type=F32)
            gate = jnp.where(lane < i, gate, -jnp.inf)
            rank = jnp.zeros((blk, LANES), jnp.int32)
            for m in range(i):
                g_m = gate[:, m:m + 1]
                rank = rank + jnp.where(
                    g_m > gate, 1, jnp.where(g_m == gate, jnp.where(lane > m, 1, 0), 0))
            chosen = jnp.where(rank < MOBA_TOPK, 1.0, 0.0)
            parts = [jnp.where(chosen[:, n:n + 1] > 0.5, s[:, n * blk:(n + 1) * blk], -jnp.inf)
                     for n in range(i)]
        else:
            parts = [s[:, n * blk:(n + 1) * blk] for n in range(i)]
        parts.append(jnp.where(causal, s[:, i * blk:], -jnp.inf))
        return jnp.concatenate(parts, axis=1) if i else parts[0]

    for i in range(nb):
        rows = slice(i * blk, (i + 1) * blk)
        scores = [masked_scores(i, g, hd) for g, hd in enumerate(heads)]
        for g, hd in enumerate(heads):
            s = scores[g]
            m_row = jnp.max(s, axis=1, keepdims=True)
            p = jnp.exp(s - m_row)
            l_row = jnp.sum(p, axis=1, keepdims=True)
            acc = jnp.dot(p.astype(BF16), vb_ref[g, :(i + 1) * blk, :], preferred_element_type=F32)
            o_ref[rows, hd] = (acc / l_row).astype(o_ref.dtype)


def _moba(u_qkv, cos, sin_signed, *, batch, seq, heads):
    t = u_qkv.shape[0]
    per = MOBA_HEADS_PER_STEP
    assert seq % MOBA_BLOCK == 0 and heads % per == 0
    nb = seq // MOBA_BLOCK
    assert nb <= LANES
    kernel = functools.partial(_moba_kernel, nb=nb, scale=MOBA_HEAD ** -0.5)
    groups = heads // per
    full = lambda off: pl.BlockSpec((seq, per * MOBA_HEAD), lambda bi, hi: (bi, off + hi))
    table = pl.BlockSpec((seq, MOBA_HEAD), lambda bi, hi: (0, 0))
    return pl.pallas_call(
        kernel,
        grid=(batch, groups),
        in_specs=[full(0), full(groups), full(2 * groups), table, table],
        out_specs=full(0),
        out_shape=jax.ShapeDtypeStruct((t, heads * MOBA_HEAD), BF16),
        scratch_shapes=[pltpu.VMEM((per, seq, MOBA_HEAD), BF16), pltpu.VMEM((per, seq, MOBA_HEAD), BF16),
                        pltpu.VMEM((per, LANES, MOBA_HEAD), F32)],
        compiler_params=pltpu.CompilerParams(dimension_semantics=("parallel", "parallel")),
        name="moba_attention",
    )(u_qkv, u_qkv, u_qkv, cos, sin_signed)


VMEM_LIMIT = 56 * 1024 * 1024


def _matmul_kernel(x_ref, w_ref, o_ref):
    o_ref[...] = jnp.dot(x_ref[...], w_ref[...], preferred_element_type=F32).astype(o_ref.dtype)


def _matmul(x, w, *, tm, tn, out_dtype=F32):
    m, k = x.shape
    n = w.shape[1]
    assert m % tm == 0 and n % tn == 0
    return pl.pallas_call(
        _matmul_kernel,
        grid=(m // tm, n // tn),
        in_specs=[pl.BlockSpec((tm, k), lambda i, j: (i, 0)),
                  pl.BlockSpec((k, tn), lambda i, j: (0, j))],
        out_specs=pl.BlockSpec((tm, tn), lambda i, j: (i, j)),
        out_shape=jax.ShapeDtypeStruct((m, n), out_dtype),
        compiler_params=pltpu.CompilerParams(
            dimension_semantics=("parallel", "parallel"), vmem_limit_bytes=VMEM_LIMIT),
        name="matmul",
    )(x, w)


def _sigmoid(x):
    return 1.0 / (1.0 + jnp.exp(-x))


def _merge_kernel(a_ref, b_ref, pa_ref, pb_ref, ga_ref, gb_ref, o_ref):
    ya = jnp.dot(a_ref[...], pa_ref[...], preferred_element_type=F32)
    yb = jnp.dot(b_ref[...], pb_ref[...], preferred_element_type=F32)
    o_ref[...] = (_sigmoid(ga_ref[...]) * ya + _sigmoid(gb_ref[...]) * yb).astype(o_ref.dtype)


def _merge(o_a, o_b, p_a, p_b, gates, *, tm, tn):
    m, k = o_a.shape
    n = p_a.shape[1]
    nj = n // tn
    row = pl.BlockSpec((tm, k), lambda i, j: (i, 0))
    wcol = pl.BlockSpec((k, tn), lambda i, j: (0, j))
    return pl.pallas_call(
        _merge_kernel,
        grid=(m // tm, nj),
        in_specs=[row, row, wcol, wcol,
                  pl.BlockSpec((tm, tn), lambda i, j: (i, j)),
                  pl.BlockSpec((tm, tn), lambda i, j: (i, nj + j))],
        out_specs=pl.BlockSpec((tm, tn), lambda i, j: (i, j)),
        out_shape=jax.ShapeDtypeStruct((m, n), BF16),
        compiler_params=pltpu.CompilerParams(
            dimension_semantics=("parallel", "parallel"), vmem_limit_bytes=VMEM_LIMIT),
        name="gated_merge",
    )(o_a, o_b, p_a, p_b, gates, gates)


def _layer_norm(x, g, b, eps):
    mu = jnp.mean(x, axis=-1, keepdims=True)
    xc = x - mu
    var = jnp.mean(xc * xc, axis=-1, keepdims=True)
    return xc * lax.rsqrt(var + eps) * g + b


def _proj_ln_kernel(m_ref, w_ref, x_ref, g_ref, b_ref, o_ref, *, alpha, eps):
    y = jnp.dot(m_ref[...], w_ref[...], preferred_element_type=F32)
    o_ref[...] = _layer_norm(alpha * x_ref[...] + y, g_ref[...], b_ref[...], eps)


def _proj_ln(mixed, w, x, g, b, *, tm, alpha, eps):
    m, k = mixed.shape
    n = w.shape[1]
    vec = pl.BlockSpec((1, n), lambda i: (0, 0))
    return pl.pallas_call(
        functools.partial(_proj_ln_kernel, alpha=alpha, eps=eps),
        grid=(m // tm,),
        in_specs=[pl.BlockSpec((tm, k), lambda i: (i, 0)),
                  pl.BlockSpec((k, n), lambda i: (0, 0)),
                  pl.BlockSpec((tm, n), lambda i: (i, 0)), vec, vec],
        out_specs=pl.BlockSpec((tm, n), lambda i: (i, 0)),
        out_shape=jax.ShapeDtypeStruct((m, n), F32),
        compiler_params=pltpu.CompilerParams(
            dimension_semantics=("parallel",), vmem_limit_bytes=VMEM_LIMIT),
        name="out_proj_layernorm",
    )(mixed, w, x, g.reshape(1, n), b.reshape(1, n))


N_EXPERTS = 256
N_GROUPS = 8
TOPK_GROUPS = 4
TOP_K = 8
ROUTED_SCALE = 2.5


def _router_kernel(h_ref, wt_ref, bias_ref, idx_ref, wgt_ref):
    tm = h_ref.shape[0]
    per = N_EXPERTS // N_GROUPS
    logits = lax.dot_general(wt_ref[...], h_ref[...], (((1,), (1,)), ((), ())),
                             precision=lax.Precision.HIGHEST, preferred_element_type=F32)
    scores = _sigmoid(logits)
    biased = scores + bias_ref[...]
    neg = -jnp.inf

    grp = biased.reshape(N_GROUPS, per, tm)
    slot = lax.broadcasted_iota(jnp.int32, (N_GROUPS, per, tm), 1)
    top1 = jnp.max(grp, axis=1, keepdims=True)
    first = jnp.min(jnp.where(grp == top1, slot, per), axis=1, keepdims=True)
    top2 = jnp.max(jnp.where(slot == first, neg, grp), axis=1, keepdims=True)
    grp_score = (top1 + top2).reshape(N_GROUPS, tm)

    gid = lax.broadcasted_iota(jnp.int32, (N_GROUPS, tm), 0)
    rank = jnp.zeros((N_GROUPS, tm), jnp.int32)
    for m in range(N_GROUPS):
        s_m = grp_score[m:m + 1, :]
        rank = rank + jnp.where(s_m > grp_score, 1,
                                jnp.where(s_m == grp_score, jnp.where(gid > m, 1, 0), 0))
    keep = (rank < TOPK_GROUPS).reshape(N_GROUPS, 1, tm)
    masked = jnp.where(keep, grp, neg).reshape(N_EXPERTS, tm)

    eid = lax.broadcasted_iota(jnp.int32, (N_EXPERTS, tm), 0)
    idxs, wgts = [], []
    for _ in range(TOP_K):
        best = jnp.max(masked, axis=0, keepdims=True)
        pick = jnp.min(jnp.where(masked == best, eid, N_EXPERTS), axis=0, keepdims=True)
        hit = eid == pick
        idxs.append(pick)
        wgts.append(jnp.sum(jnp.where(hit, scores, 0.0), axis=0, keepdims=True))
        masked = jnp.where(hit, neg, masked)
    wgt = jnp.concatenate(wgts, axis=0)
    wgt = wgt / jnp.sum(wgt, axis=0, keepdims=True) * ROUTED_SCALE
    idx_ref[...] = jnp.concatenate(idxs, axis=0)
    wgt_ref[...] = wgt


def _router(h, w_router_t, bias, *, tm):
    t, d = h.shape
    out = pl.BlockSpec((TOP_K, tm), lambda i: (0, i))
    return pl.pallas_call(
        _router_kernel,
        grid=(t // tm,),
        in_specs=[pl.BlockSpec((tm, d), lambda i: (i, 0)),
                  pl.BlockSpec((N_EXPERTS, d), lambda i: (0, 0)),
                  pl.BlockSpec((N_EXPERTS, 1), lambda i: (0, 0))],
        out_specs=[out, out],
        out_shape=[jax.ShapeDtypeStruct((TOP_K, t), jnp.int32),
                   jax.ShapeDtypeStruct((TOP_K, t), F32)],
        compiler_params=pltpu.CompilerParams(
            dimension_semantics=("parallel",), vmem_limit_bytes=VMEM_LIMIT),
        name="moe_router",
    )(h, w_router_t, bias.reshape(N_EXPERTS, 1))


EXPERT_ROWS = 128


def _silu(x):
    return x * _sigmoid(x)


GATHER_AHEAD = 3
X_SLOTS = 4
Y_SLOTS = 4
SCATTER_LAG = 3
IDX_SLOTS = 8


def _experts_kernel(blk_e_ref, nxt_e_ref, n_used_ref,
                    codes_hbm, h_hbm, wg_hbm, wu_hbm, wd_hbm, ys_hbm,
                    idx_ref, x_buf, y_buf, wg_st, wu_st, wd_st, wg_bf, wu_bf, wd_bf,
                    idx_sem, gat_sem, sct_sem, w_sem, *, n_tokens):
    i = pl.program_id(0)
    n_blk = pl.num_programs(0)
    n_used = n_used_ref[0]
    used = i < n_used
    rows = EXPERT_ROWS
    x_chunks = x_buf.shape[2]
    chunks = y_buf.shape[2]

    def idx_copy(k):
        slot = k & (IDX_SLOTS - 1)
        return pltpu.make_async_copy(codes_hbm.at[pl.ds(k, 1), :], idx_ref.at[pl.ds(slot, 1), :],
                                     idx_sem.at[slot])

    def start_rows(k, gather):
        window = k & (IDX_SLOTS - 1)
        for j in range(rows):
            if gather:
                buf = k & (X_SLOTS - 1)
                src = idx_ref[window, j]
                pltpu.make_async_copy(h_hbm.at[pl.ds(src, x_chunks), :], x_buf.at[buf, j],
                                      gat_sem.at[buf]).start()
            else:
                buf = k & (Y_SLOTS - 1)
                dst = idx_ref[window, rows + j]
                pltpu.make_async_copy(y_buf.at[buf, j], ys_hbm.at[pl.ds(dst, chunks), :],
                                      sct_sem.at[buf]).start()

    def wait_gather(k):
        buf = k & (X_SLOTS - 1)
        pltpu.make_async_copy(x_buf.at[buf], x_buf.at[buf], gat_sem.at[buf]).wait()

    def wait_scatter(k):
        buf = k & (Y_SLOTS - 1)
        pltpu.make_async_copy(y_buf.at[buf], y_buf.at[buf], sct_sem.at[buf]).wait()

    def weight_copies(e):
        return (pltpu.make_async_copy(wg_hbm.at[e], wg_st, w_sem.at[0]),
                pltpu.make_async_copy(wu_hbm.at[e], wu_st, w_sem.at[1]),
                pltpu.make_async_copy(wd_hbm.at[e], wd_st, w_sem.at[2]))

    def compute(k):
        u = pltpu.einshape("jcl->cjl", x_buf[k & (X_SLOTS - 1)])
        u = jnp.concatenate([u[c] for c in range(x_chunks)], axis=1)
        lo = lax.bitcast_convert_type(u << 16, F32)
        hi = lax.bitcast_convert_type(u & jnp.uint32(0xFFFF0000), F32)
        x = jnp.concatenate([lo, hi], axis=1).astype(BF16)
        hid = _silu(jnp.dot(x, wg_bf[...], preferred_element_type=F32)) * jnp.dot(
            x, wu_bf[...], preferred_element_type=F32)
        y = jnp.dot(hid.astype(BF16), wd_bf[...], preferred_element_type=F32)
        y = jnp.stack([y[:, c * LANES:(c + 1) * LANES] for c in range(chunks)], axis=0)
        y_buf[k & (Y_SLOTS - 1)] = pltpu.einshape("cjl->jcl", y)

    @pl.when(i == 0)
    def _():
        for cp in weight_copies(blk_e_ref[0]):
            cp.start(priority=1)
        for k in range(GATHER_AHEAD):
            @pl.when(k < n_used)
            def _():
                idx_copy(k).start()
                idx_copy(k).wait()
                start_rows(k, gather=True)

        @pl.when(GATHER_AHEAD < n_used)
        def _():
            idx_copy(GATHER_AHEAD).start()

    @pl.when(i + GATHER_AHEAD + 1 < n_used)
    def _():
        idx_copy(i + GATHER_AHEAD + 1).start()

    has_ahead = i + GATHER_AHEAD < n_used

    @pl.when(has_ahead)
    def _():
        idx_copy(i + GATHER_AHEAD).wait()

    prev_e = blk_e_ref[jnp.maximum(i - 1, 0)]
    fresh = jnp.logical_or(i == 0, blk_e_ref[i] != prev_e)

    @pl.when(jnp.logical_and(used, fresh))
    def _():
        for cp in weight_copies(blk_e_ref[i]):
            cp.wait()
        wg_bf[...] = wg_st[...].astype(BF16)
        wu_bf[...] = wu_st[...].astype(BF16)
        wd_bf[...] = wd_st[...].astype(BF16)

        @pl.when(nxt_e_ref[i] >= 0)
        def _():
            for cp in weight_copies(nxt_e_ref[i]):
                cp.start(priority=1)

    @pl.when(jnp.logical_and(i >= SCATTER_LAG, i - SCATTER_LAG < n_used))
    def _():
        wait_scatter(i - SCATTER_LAG)

    interior = jnp.logical_and(i >= 1, has_ahead)

    @pl.when(interior)
    def _():
        wait_gather(i)
        start_rows(i - 1, gather=False)
        start_rows(i + GATHER_AHEAD, gather=True)
        compute(i)

    @pl.when(jnp.logical_and(used, jnp.logical_not(interior)))
    def _():
        @pl.when(has_ahead)
        def _():
            start_rows(i + GATHER_AHEAD, gather=True)

        @pl.when(i >= 1)
        def _():
            start_rows(i - 1, gather=False)

        wait_gather(i)
        compute(i)

        @pl.when(i + 1 == n_used)
        def _():
            start_rows(i, gather=False)

    @pl.when(i == n_blk - 1)
    def _():
        for back in range(SCATTER_LAG - 1, -1, -1):
            @pl.when(jnp.logical_and(i >= back, i - back < n_used))
            def _():
                wait_scatter(i - back)

        y_buf[...] = jnp.zeros_like(y_buf)
        for half in range(2):
            for j in range(rows):
                first = (n_tokens * TOP_K + half * rows + j) * chunks
                pltpu.make_async_copy(y_buf.at[half, j], ys_hbm.at[pl.ds(first, chunks), :],
                                      sct_sem.at[half]).start()
            wait_scatter(half)


def _pack_rows(h):
    t, d = h.shape
    bits = lax.bitcast_convert_type(h.astype(BF16), jnp.uint16).astype(jnp.uint32)
    words = bits[:, :d // 2] | (bits[:, d // 2:] << 16)
    return words.reshape(t * d // (2 * LANES), LANES)


def _experts(h, codes, blk_e, nxt_e, n_used, w_gate, w_up, w_down):
    d, ff = w_gate.shape[1], w_gate.shape[2]
    chunks = d // LANES
    x_chunks = d // (2 * LANES)
    t = h.shape[0] // x_chunks
    n_blk = blk_e.shape[0]
    any_spec = pl.BlockSpec(memory_space=pl.ANY)
    grid_spec = pltpu.PrefetchScalarGridSpec(
        num_scalar_prefetch=3,
        grid=(n_blk,),
        in_specs=[any_spec] * 5,
        out_specs=any_spec,
        scratch_shapes=[pltpu.SMEM((IDX_SLOTS, 2 * EXPERT_ROWS), jnp.int32),
                        pltpu.VMEM((X_SLOTS, EXPERT_ROWS, x_chunks, LANES), jnp.uint32),
                        pltpu.VMEM((Y_SLOTS, EXPERT_ROWS, chunks, LANES), F32),
                        pltpu.VMEM((d, ff), F32), pltpu.VMEM((d, ff), F32), pltpu.VMEM((ff, d), F32),
                        pltpu.VMEM((d, ff), BF16), pltpu.VMEM((d, ff), BF16),
                        pltpu.VMEM((ff, d), BF16),
                        pltpu.SemaphoreType.DMA((IDX_SLOTS,)), pltpu.SemaphoreType.DMA((X_SLOTS,)),
                        pltpu.SemaphoreType.DMA((Y_SLOTS,)), pltpu.SemaphoreType.DMA((3,))])
    return pl.pallas_call(
        functools.partial(_experts_kernel, n_tokens=t),
        grid_spec=grid_spec,
        out_shape=jax.ShapeDtypeStruct(((t * TOP_K + 2 * EXPERT_ROWS) * chunks, LANES), F32),
        compiler_params=pltpu.CompilerParams(
            dimension_semantics=("arbitrary",), vmem_limit_bytes=VMEM_LIMIT),
        name="routed_experts",
    )(blk_e, nxt_e, n_used, codes, h, w_gate, w_up, w_down)


def _shared_ln_kernel(h_ref, *refs, alpha, eps):
    ys_refs = refs[:TOP_K]
    tw_ref, wg_ref, wu_ref, wd_ref, g_ref, b_ref, o_ref, routed_ref = refs[TOP_K:]
    h = h_ref[...]
    tm, d = h.shape
    chunks = d // LANES
    hb = h.astype(BF16)
    hid = _silu(jnp.dot(hb, wg_ref[...], preferred_element_type=F32)) * jnp.dot(
        hb, wu_ref[...], preferred_element_type=F32)
    moe = jnp.dot(hid.astype(BF16), wd_ref[...], preferred_element_type=F32)
    routed = ys_refs[0][...] * tw_ref[:, 0:1]
    for k in range(1, TOP_K):
        routed = routed + ys_refs[k][...] * tw_ref[:, k:k + 1]
    routed_ref[...] = routed
    moe = moe + jnp.concatenate(
        [routed_ref[pl.ds(c, tm, stride=chunks), :] for c in range(chunks)], axis=1)
    o_ref[...] = _layer_norm(alpha * h + moe, g_ref[...], b_ref[...], eps)


def _shared_ln(h, ys, top_w, w_gate, w_up, w_down, g, b, *, tm, alpha, eps):
    t, d = h.shape
    ff = w_gate.shape[1]
    chunks = d // LANES
    n_tiles = t // tm
    row = pl.BlockSpec((tm, d), lambda i: (i, 0))
    vec = pl.BlockSpec((1, d), lambda i: (0, 0))
    band = lambda k: pl.BlockSpec((tm * chunks, LANES), lambda i: (k * n_tiles + i, 0))
    return pl.pallas_call(
        functools.partial(_shared_ln_kernel, alpha=alpha, eps=eps),
        grid=(n_tiles,),
        in_specs=[row] + [band(k) for k in range(TOP_K)] + [
            pl.BlockSpec((tm * chunks, TOP_K), lambda i: (i, 0)),
            pl.BlockSpec((d, ff), lambda i: (0, 0)), pl.BlockSpec((d, ff), lambda i: (0, 0)),
            pl.BlockSpec((ff, d), lambda i: (0, 0)), vec, vec],
        out_specs=row,
        out_shape=jax.ShapeDtypeStruct((t, d), F32),
        scratch_shapes=[pltpu.VMEM((tm * chunks, LANES), F32)],
        compiler_params=pltpu.CompilerParams(
            dimension_semantics=("parallel",), vmem_limit_bytes=VMEM_LIMIT),
        name="shared_expert_layernorm",
    )(h, *([ys] * TOP_K), top_w, w_gate, w_up, w_down, g.reshape(1, d), b.reshape(1, d))


LN_EPS = 1e-5
ROPE_THETA = 10000.0


def _rope_tables(seq, dim):
    inv = 1.0 / (ROPE_THETA ** (jnp.arange(0, dim, 2, dtype=F32) / dim))
    ang = jnp.arange(seq, dtype=F32)[:, None] * inv[None, :]
    ang = jnp.concatenate([ang, ang], axis=-1)
    sign = jnp.where(jnp.arange(dim) < dim // 2, -1.0, 1.0).astype(F32)
    return jnp.cos(ang), jnp.sin(ang) * sign


def _pad_lora(x, d_decay, d_aaa, axis):
    assert d_decay <= LORA_PAD and d_aaa <= LORA_PAD
    decay, aaa, gate = jnp.split(x, [d_decay, d_decay + d_aaa], axis=axis)
    pad = lambda p: jnp.pad(p, [(0, LORA_PAD - p.shape[a]) if a == axis % x.ndim else (0, 0)
                                for a in range(x.ndim)])
    return jnp.concatenate([pad(decay), pad(aaa), gate], axis=axis)


def _dispatch(top_idx, n_tokens, src_stride, dst_stride):
    n_assign = n_tokens * TOP_K
    i32 = jnp.int32
    e_flat = top_idx.reshape(n_assign).astype(i32)
    _, order = lax.sort((e_flat, jnp.arange(n_assign, dtype=i32)), num_keys=1, is_stable=True)
    experts = jnp.arange(N_EXPERTS, dtype=i32)
    counts = jnp.sum((e_flat.reshape(-1, LANES, 1) == experts).astype(i32), axis=(0, 1))
    grp_start = jnp.cumsum(counts) - counts
    padded = (counts + EXPERT_ROWS - 1) // EXPERT_ROWS * EXPERT_ROWS
    pad_end = jnp.cumsum(padded)
    pad_start = pad_end - padded
    n_blk = -(-n_assign // EXPERT_ROWS) + N_EXPERTS
    n_used = (pad_end[-1] // EXPERT_ROWS).astype(i32)
    first_row = jnp.arange(n_blk, dtype=i32) * EXPERT_ROWS
    blk_e = jnp.minimum(jnp.sum((pad_end[None, :] <= first_row[:, None]).astype(i32), axis=1),
                        N_EXPERTS - 1)
    into = first_row - pad_start[blk_e]
    blk_s0 = jnp.clip(grp_start[blk_e] + into, 0, n_assign)
    blk_nv = jnp.clip(counts[blk_e] - into, 0, EXPERT_ROWS)
    later = jnp.where(counts > 0, experts, N_EXPERTS)
    nxt = lax.cummin(jnp.concatenate([later[1:], jnp.full((1,), N_EXPERTS, i32)]), reverse=True)
    nxt_e = jnp.where(nxt < N_EXPERTS, nxt, -1)[blk_e]
    row = jnp.arange(EXPERT_ROWS, dtype=i32)[None, :]
    src = jnp.minimum(blk_s0[:, None] + row, n_assign - 1)
    pad = n_assign + (jnp.arange(n_blk, dtype=i32)[:, None] % 2) * EXPERT_ROWS + row
    valid = row < blk_nv[:, None]
    ids = order[src]
    codes = jnp.concatenate([jnp.where(valid, ids % n_tokens, 0) * src_stride,
                             jnp.where(valid, ids, pad) * dst_stride], axis=1)
    return codes.astype(i32), blk_e, nxt_e.astype(i32), n_used.reshape(1)


def kernel(x, w_in, shift_mu, w_lora_up, w0, a_lora_up, a0, g_lora_up, k_k, k_a, r_k, ln_x_w, ln_x_b,
           w_o_rwkv, w_o_moba, w_out, ln1_g, ln1_b, w_router, router_bias, w_gate_e, w_up_e,
           w_down_e, w_gate_s, w_up_s, w_down_s, ln2_g, ln2_b):
    batch, seq, d = x.shape
    depth = w_in.shape[0]
    alpha = (2 * depth) ** 0.25
    t = batch * seq
    c = d
    shift_w = shift_mu.shape[1]
    moba_heads = d // MOBA_HEAD
    cos, sin_signed = _rope_tables(seq, MOBA_HEAD)

    h = x.reshape(t, d)
    for layer in range(depth):
        wl = w_in[layer].astype(BF16)
        hb = h.astype(BF16)
        d_decay, d_aaa = w_lora_up.shape[1], a_lora_up.shape[1]
        w_lora = _pad_lora(wl[:, 3 * c:shift_w], d_decay, d_aaa, axis=1)
        assert w_lora.shape[1] % LANES == 0
        u_rkv = _matmul(hb, wl[:, :3 * c], tm=1024, tn=1024)
        u_lora = _matmul(hb, w_lora, tm=1024, tn=w_lora.shape[1])
        u_qkv = _matmul(hb, wl[:, shift_w:shift_w + 3 * d], tm=1024, tn=1024)
        u_gate = _matmul(hb, wl[:, shift_w + 3 * d:], tm=1024, tn=1024)

        pad_rows = lambda p: jnp.pad(p, ((0, LORA_PAD - p.shape[0]), (0, 0)))
        o_rwkv = _rwkv_mix(u_rkv, u_lora, shift_mu[layer][:3 * c],
                           _pad_lora(shift_mu[layer][3 * c:], d_decay, d_aaa, axis=0),
                           pad_rows(w_lora_up[layer]), w0[layer], pad_rows(a_lora_up[layer]),
                           a0[layer], g_lora_up[layer], k_k[layer], k_a[layer], r_k[layer],
                           ln_x_w[layer], ln_x_b[layer], batch=batch, seq=seq)
        o_moba = _moba(u_qkv, cos, sin_signed, batch=batch, seq=seq, heads=moba_heads)

        mixed = _merge(o_rwkv, o_moba, w_o_rwkv[layer].astype(BF16), w_o_moba[layer].astype(BF16),
                       u_gate, tm=512, tn=1024)
        h = _proj_ln(mixed, w_out[layer].astype(BF16), h, ln1_g[layer], ln1_b[layer],
                     tm=512, alpha=alpha, eps=LN_EPS)

        top_idx, top_w = _router(h, w_router[layer].T, router_bias[layer], tm=512)
        chunks = d // LANES
        codes, blk_e, nxt_e, n_used = _dispatch(top_idx, t, chunks // 2, chunks)
        ys = _experts(_pack_rows(h), codes, blk_e, nxt_e, n_used,
                      w_gate_e[layer], w_up_e[layer], w_down_e[layer])
        h = _shared_ln(h, ys, jnp.repeat(top_w.T, chunks, axis=0), w_gate_s[layer].astype(BF16),
                       w_up_s[layer].astype(BF16), w_down_s[layer].astype(BF16), ln2_g[layer],
                       ln2_b[layer], tm=128, alpha=alpha, eps=LN_EPS)
    return h.reshape(batch, seq, d)
```

```python
import functools

import jax
import jax.numpy as jnp
from jax import lax
from jax.experimental import pallas as pl
from jax.experimental.pallas import tpu as pltpu

F32 = jnp.float32
BF16 = jnp.bfloat16

LANES = 128
WKV_HEAD = 64
WKV_CHUNK = 64


def _dot(a, b):
    return jnp.dot(a.astype(BF16), b.astype(BF16), preferred_element_type=F32)


def _dot_nt(a, b):
    return lax.dot_general(a.astype(BF16), b.astype(BF16), (((1,), (1,)), ((), ())),
                           preferred_element_type=F32)


assert WKV_CHUNK == WKV_HEAD
def _each(f, *lists):
    return [f(*xs) for xs in zip(*lists)]


def _wkv_pairs(r, ld, k, v, a, b, m_state):
    c = WKV_CHUNK
    cat0 = lambda *xs: jnp.concatenate(xs, axis=0)
    cat1 = lambda *xs: jnp.concatenate(xs, axis=1)
    lane = lax.broadcasted_iota(jnp.int32, (c, LANES), 1)
    step = lax.broadcasted_iota(jnp.int32, (c, LANES), 0)
    head0 = lane < WKV_HEAD
    other = jnp.where(head0, lane, lane - WKV_HEAD)

    def stack(x):
        x = x.astype(BF16)
        zero = jnp.zeros_like(x)
        return cat0(jnp.where(head0, x, zero), jnp.where(head0, zero, x))

    def cumsum_rows(x):
        shift = 1
        while shift < c:
            x = x + jnp.where(step >= shift, pltpu.roll(x, shift, axis=0), 0.0)
            shift *= 2
        return x

    dot = lambda p, q: jnp.dot(p.astype(BF16), q, preferred_element_type=F32)

    cw = _each(cumsum_rows, ld)
    e_out = _each(lambda x: jnp.exp(-x), cw)
    a_t = _each(lambda x, w, l: x * jnp.exp(w - l), a, cw, ld)
    r_t = _each(lambda x, w: x * jnp.exp(w), r, cw)
    bk_st = _each(lambda p, q, e: cat0(stack(p * e), stack(q * e)), b, k, e_out)
    v_st = _each(stack, v)
    aa = _each(lambda p, q, s: _dot_nt(cat0(p, q), s), a_t, r_t, bk_st)

    strict = step > other
    incl = step >= other
    n_ab = _each(lambda x: jnp.where(strict, x[:c, :LANES], 0.0), aa)
    a_ak = _each(lambda x: jnp.where(strict, x[:c, LANES:], 0.0), aa)
    a_rb = _each(lambda x: jnp.where(incl, x[c:, :LANES], 0.0), aa)
    a_rk = _each(lambda x: jnp.where(incl, x[c:, LANES:], 0.0), aa)

    eye = jnp.where(step == other, 1.0, 0.0)
    t_inv = _each(lambda n: eye + n, n_ab)
    w1 = _each(dot, a_ak, v_st)
    n_pow = _each(lambda n: dot(n, stack(n)), n_ab)
    for _ in range(c.bit_length() - 3):
        pt = _each(lambda n, t: dot(n, cat1(stack(n), stack(t))), n_pow, t_inv)
        n_pow = _each(lambda x: x[:, :LANES], pt)
        t_inv = _each(lambda t, x: t + x[:, LANES:], t_inv, pt)
    t_inv = _each(lambda t, n: t + dot(n, stack(t)), t_inv, n_pow)

    tw = _each(lambda t, w, x: dot(t, cat1(stack(w), stack(x))), t_inv, w1, a_t)
    rhs = _each(lambda x, v_: cat0(cat1(stack(x[:, LANES:]), stack(x[:, :LANES])),
                                   cat1(jnp.zeros_like(v_), v_)), tw, v_st)
    out_p = _each(lambda p, q, s: dot(cat1(p, q), s), a_rb, a_rk, rhs)
    tail = _each(lambda w: jnp.exp(w[c - 1:c, :] - w), cw)
    bkp_st = _each(lambda p, q, e: cat0(stack(p * e), stack(q * e)), b, k, tail)
    upd = _each(lambda p, s: lax.dot_general(p, s, (((0,), (0,)), ((), ())),
                                             preferred_element_type=F32), bkp_st, rhs)
    upd_p = _each(lambda u: u[:WKV_HEAD] + u[WKV_HEAD:], upd)
    r_hat = _each(lambda x, o: x + o[:, :LANES], r_t, out_p)
    g = _each(lambda w, u: jnp.where(step == other, jnp.exp(w[c - 1:c, :]), 0.0) + u[:, :LANES],
              cw, upd_p)

    def apply_state(lhs, m):
        l_hi = lhs.astype(BF16)
        l_lo = (lhs - l_hi.astype(F32)).astype(BF16)
        m_hi = m.astype(BF16)
        m_lo = m - m_hi.astype(F32)
        both = jnp.dot(l_hi, cat1(stack(m_hi), stack(m_lo)), preferred_element_type=F32)
        return (both[:, :LANES] + both[:, LANES:]) + jnp.dot(l_lo, stack(m_hi),
                                                             preferred_element_type=F32)

    rm = _each(lambda p, q, m: apply_state(cat0(p, q), m), r_hat, g, m_state)
    y = _each(lambda x, o: x[:c] + o[:, LANES:], rm, out_p)
    m_new = _each(lambda x, u: x[c:] + u[:, LANES:], rm, upd_p)
    return y, m_new


RWKV_GN_EPS = 64e-5
LORA_PAD = LANES
RWKV_SEQS_PER_STEP = 1


def _rwkv_kernel(u_ref, ul_ref, mu_ref, mul_ref, wl_ref, w0_ref, al_ref, a0_ref, gl_ref, kk_ref,
                 ka_ref, rk_ref, lnw_ref, lnb_ref, o_ref, m_ref, prev_ref, prevl_ref):
    c = WKV_CHUNK
    n_seq, _, width = o_ref.shape

    @pl.when(pl.program_id(1) == 0)
    def _():
        m_ref[...] = jnp.zeros_like(m_ref)
        prev_ref[...] = jnp.zeros_like(prev_ref)
        prevl_ref[...] = jnp.zeros_like(prevl_ref)

    def shifted(u, prev, mu):
        first = lax.broadcasted_iota(jnp.int32, u.shape, 0) == 0
        before = jnp.where(first, prev[7:8, :], pltpu.roll(u, 1, axis=0))
        return u + (before - u) * mu

    lanes = [slice(p * LANES, (p + 1) * LANES) for p in range(width // LANES)]
    split = lambda x: [x[:, sl] for sl in lanes]
    row = lax.broadcasted_iota(jnp.int32, (LANES, LANES), 0)
    col = lax.broadcasted_iota(jnp.int32, (LANES, LANES), 1)
    same_head = jnp.where((row < WKV_HEAD) == (col < WKV_HEAD), 1.0, 0.0).astype(BF16)

    def head_sums(x):
        stacked = jnp.concatenate(split(x), axis=0).astype(BF16)
        s = jnp.dot(stacked, same_head, preferred_element_type=F32)
        return [s[p * c:(p + 1) * c] for p in range(len(lanes))]

    def prepare(s):
        u = u_ref[s]
        ul = ul_ref[s]
        z = shifted(u, prev_ref[s], mu_ref[...])
        zl = shifted(ul, prevl_ref[s], mul_ref[...])
        prev_ref[s] = u[c - 8:, :]
        prevl_ref[s] = ul[c - 8:, :]
        r = z[:, :width]
        k_raw = z[:, width:2 * width]
        v = z[:, 2 * width:]
        w = w0_ref[...] + _dot(jnp.tanh(zl[:, :LORA_PAD]), wl_ref[...])
        ld = -jnp.exp(-(jnp.maximum(-w, 0.0) + jnp.log(1.0 + jnp.exp(-jnp.abs(w)))) - 0.5)
        a = _sigmoid(a0_ref[...] + _dot(zl[:, LORA_PAD:2 * LORA_PAD], al_ref[...]))
        g = _dot(_sigmoid(zl[:, 2 * LORA_PAD:]), gl_ref[...])
        kk_raw = k_raw * kk_ref[...]
        k = k_raw * (1.0 + (a - 1.0) * ka_ref[...])
        norm2 = head_sums(kk_raw * kk_raw)
        kk = _each(lambda x, n2: x * lax.rsqrt(jnp.maximum(n2, 1e-24)), split(kk_raw), norm2)
        bonus = head_sums(r * k * rk_ref[...])
        return r, ld, k, v, kk, a, g, bonus

    def recur(s, r, ld, k, v, kk, a):
        m_state = [m_ref[s, p] for p in range(len(lanes))]
        y, m_new = _wkv_pairs(split(r), split(ld), split(k), split(v), _each(lambda x: -x, kk),
                              _each(lambda x, a_: x * a_, kk, split(a)), m_state)
        for p in range(len(lanes)):
            m_ref[s, p] = m_new[p]
        return jnp.concatenate(y, axis=1)

    def finish(s, y, v, g, bonus):
        inv_n = 1.0 / WKV_HEAD
        mean = jnp.concatenate(head_sums(y), axis=1) * inv_n
        yc = y - mean
        var = jnp.concatenate(head_sums(yc * yc), axis=1) * inv_n
        yn = yc * lax.rsqrt(var + RWKV_GN_EPS) * lnw_ref[...] + lnb_ref[...]
        out = (yn + jnp.concatenate(bonus, axis=1) * v) * g
        o_ref[s] = out.astype(o_ref.dtype)

    prepared = [prepare(s) for s in range(n_seq)]
    ys = [recur(s, *prepared[s][:6]) for s in range(n_seq)]
    for s in range(n_seq):
        _, _, _, v, _, _, g, bonus = prepared[s]
        finish(s, ys[s], v, g, bonus)


def _rwkv_mix(u_rkv, u_lora, mu_rkv, mu_lora, w_lora, w0, a_lora, a0, g_lora, k_k, k_a, r_k,
              ln_w, ln_b, *, batch, seq):
    t = u_rkv.shape[0]
    width = u_rkv.shape[1] // 3
    n_lora = u_lora.shape[1]
    per = RWKV_SEQS_PER_STEP
    assert t == batch * seq and seq % WKV_CHUNK == 0 and width % LANES == 0 and batch % per == 0
    n_chunks = seq // WKV_CHUNK
    rows = lambda n: pl.BlockSpec((per, WKV_CHUNK, n), lambda bi, ci: (bi, ci, 0))
    whole = lambda x: pl.BlockSpec(x.shape, lambda bi, ci: (0,) * x.ndim)
    vec = lambda x: x.reshape(1, -1)
    params = [vec(mu_rkv), vec(mu_lora), w_lora.astype(BF16), vec(w0), a_lora.astype(BF16), vec(a0),
              g_lora.astype(BF16), vec(k_k), vec(k_a), vec(r_k), vec(ln_w), vec(ln_b)]
    out = pl.pallas_call(
        _rwkv_kernel,
        grid=(batch // per, n_chunks),
        in_specs=[rows(3 * width), rows(n_lora)] + [whole(p) for p in params],
        out_specs=rows(width),
        out_shape=jax.ShapeDtypeStruct((batch, seq, width), BF16),
        scratch_shapes=[pltpu.VMEM((per, width // LANES, WKV_HEAD, LANES), F32),
                        pltpu.VMEM((per, 8, 3 * width), F32), pltpu.VMEM((per, 8, n_lora), F32)],
        compiler_params=pltpu.CompilerParams(
            dimension_semantics=("parallel", "arbitrary"), vmem_limit_bytes=VMEM_LIMIT),
        name="rwkv7_time_mix",
    )(u_rkv.reshape(batch, seq, 3 * width), u_lora.reshape(batch, seq, n_lora), *params)
    return out.reshape(t, width)


MOBA_BLOCK = 256
MOBA_TOPK = 3
MOBA_HEAD = 128


def _rope(t, cos, sin_signed):
    return t * cos + pltpu.roll(t, MOBA_HEAD // 2, axis=1) * sin_signed


MOBA_HEADS_PER_STEP = 2


def _moba_kernel(q_ref, k_ref, v_ref, cos_ref, sin_ref, o_ref, kr_ref, vb_ref, kmean_ref, *,
                 nb, scale):
    blk = MOBA_BLOCK
    heads = [slice(g * MOBA_HEAD, (g + 1) * MOBA_HEAD) for g in range(MOBA_HEADS_PER_STEP)]
    kmean_ref[...] = jnp.zeros_like(kmean_ref)
    for n in range(nb):
        rows = slice(n * blk, (n + 1) * blk)
        for g, hd in enumerate(heads):
            kr = _rope(k_ref[rows, hd], cos_ref[rows, :], sin_ref[rows, :])
            kr_ref[g, rows, :] = kr.astype(BF16)
            vb_ref[g, rows, :] = v_ref[rows, hd].astype(BF16)
            kmean_ref[g, n:n + 1, :] = jnp.mean(kr, axis=0, keepdims=True)

    lane = lax.broadcasted_iota(jnp.int32, (blk, LANES), 1)
    row = lax.broadcasted_iota(jnp.int32, (blk, blk), 0)
    col = lax.broadcasted_iota(jnp.int32, (blk, blk), 1)
    causal = row >= col

    def masked_scores(i, g, hd):
        rows = slice(i * blk, (i + 1) * blk)
        q = _rope(q_ref[rows, hd], cos_ref[rows, :], sin_ref[rows, :])
        s = _dot_nt(q, kr_ref[g, :(i + 1) * blk, :]) * scale
        if i > MOBA_TOPK:
            gate = lax.dot_general(q, kmean_ref[g], (((1,), (1,)), ((), ())),
                                   precision=lax.Precision.HIGHEST, preferred_element_type=F32)
            gate = jnp.where(lane < i, gate, -jnp.inf)
            rank = jnp.zeros((blk, LANES), jnp.int32)
            for m in range(i):
                g_m = gate[:, m:m + 1]
                rank = rank + jnp.where(
                    g_m > gate, 1, jnp.where(g_m == gate, jnp.where(lane > m, 1, 0), 0))
            chosen = jnp.where(rank < MOBA_TOPK, 1.0, 0.0)
            parts = [jnp.where(chosen[:, n:n + 1] > 0.5, s[:, n * blk:(n + 1) * blk], -jnp.inf)
                     for n in range(i)]
        else:
            parts = [s[:, n * blk:(n + 1) * blk] for n in range(i)]
        parts.append(jnp.where(causal, s[:, i * blk:], -jnp.inf))
        return jnp.concatenate(parts, axis=1) if i else parts[0]

    for i in range(nb):
        rows = slice(i * blk, (i + 1) * blk)
        scores = [masked_scores(i, g, hd) for g, hd in enumerate(heads)]
        for g, hd in enumerate(heads):
            s = scores[g]
            m_row = jnp.max(s, axis=1, keepdims=True)
            p = jnp.exp(s - m_row)
            l_row = jnp.sum(p, axis=1, keepdims=True)
            acc = jnp.dot(p.astype(BF16), vb_ref[g, :(i + 1) * blk, :], preferred_element_type=F32)
            o_ref[rows, hd] = (acc / l_row).astype(o_ref.dtype)


def _moba(u_qkv, cos, sin_signed, *, batch, seq, heads):
    t = u_qkv.shape[0]
    per = MOBA_HEADS_PER_STEP
    assert seq % MOBA_BLOCK == 0 and heads % per == 0
    nb = seq // MOBA_BLOCK
    assert nb <= LANES
    kernel = functools.partial(_moba_kernel, nb=nb, scale=MOBA_HEAD ** -0.5)
    groups = heads // per
    full = lambda off: pl.BlockSpec((seq, per * MOBA_HEAD), lambda bi, hi: (bi, off + hi))
    table = pl.BlockSpec((seq, MOBA_HEAD), lambda bi, hi: (0, 0))
    return pl.pallas_call(
        kernel,
        grid=(batch, groups),
        in_specs=[full(0), full(groups), full(2 * groups), table, table],
        out_specs=full(0),
        out_shape=jax.ShapeDtypeStruct((t, heads * MOBA_HEAD), BF16),
        scratch_shapes=[pltpu.VMEM((per, seq, MOBA_HEAD), BF16), pltpu.VMEM((per, seq, MOBA_HEAD), BF16),
                        pltpu.VMEM((per, LANES, MOBA_HEAD), F32)],
        compiler_params=pltpu.CompilerParams(dimension_semantics=("parallel", "parallel")),
        name="moba_attention",
    )(u_qkv, u_qkv, u_qkv, cos, sin_signed)


VMEM_LIMIT = 56 * 1024 * 1024


def _matmul_kernel(x_ref, w_ref, o_ref):
    o_ref[...] = jnp.dot(x_ref[...], w_ref[...], preferred_element_type=F32).astype(o_ref.dtype)


def _matmul(x, w, *, tm, tn, out_dtype=F32):
    m, k = x.shape
    n = w.shape[1]
    assert m % tm == 0 and n % tn == 0
    return pl.pallas_call(
        _matmul_kernel,
        grid=(m // tm, n // tn),
        in_specs=[pl.BlockSpec((tm, k), lambda i, j: (i, 0)),
                  pl.BlockSpec((k, tn), lambda i, j: (0, j))],
        out_specs=pl.BlockSpec((tm, tn), lambda i, j: (i, j)),
        out_shape=jax.ShapeDtypeStruct((m, n), out_dtype),
        compiler_params=pltpu.CompilerParams(
            dimension_semantics=("parallel", "parallel"), vmem_limit_bytes=VMEM_LIMIT),
        name="matmul",
    )(x, w)


def _sigmoid(x):
    return 1.0 / (1.0 + jnp.exp(-x))


def _merge_kernel(a_ref, b_ref, pa_ref, pb_ref, ga_ref, gb_ref, o_ref):
    ya = jnp.dot(a_ref[...], pa_ref[...], preferred_element_type=F32)
    yb = jnp.dot(b_ref[...], pb_ref[...], preferred_element_type=F32)
    o_ref[...] = (_sigmoid(ga_ref[...]) * ya + _sigmoid(gb_ref[...]) * yb).astype(o_ref.dtype)


def _merge(o_a, o_b, p_a, p_b, gates, *, tm, tn):
    m, k = o_a.shape
    n = p_a.shape[1]
    nj = n // tn
    row = pl.BlockSpec((tm, k), lambda i, j: (i, 0))
    wcol = pl.BlockSpec((k, tn), lambda i, j: (0, j))
    return pl.pallas_call(
        _merge_kernel,
        grid=(m // tm, nj),
        in_specs=[row, row, wcol, wcol,
                  pl.BlockSpec((tm, tn), lambda i, j: (i, j)),
                  pl.BlockSpec((tm, tn), lambda i, j: (i, nj + j))],
        out_specs=pl.BlockSpec((tm, tn), lambda i, j: (i, j)),
        out_shape=jax.ShapeDtypeStruct((m, n), BF16),
        compiler_params=pltpu.CompilerParams(
            dimension_semantics=("parallel", "parallel"), vmem_limit_bytes=VMEM_LIMIT),
        name="gated_merge",
    )(o_a, o_b, p_a, p_b, gates, gates)


def _layer_norm(x, g, b, eps):
    mu = jnp.mean(x, axis=-1, keepdims=True)
    xc = x - mu
    var = jnp.mean(xc * xc, axis=-1, keepdims=True)
    return xc * lax.rsqrt(var + eps) * g + b


def _proj_ln_kernel(m_ref, w_ref, x_ref, g_ref, b_ref, o_ref, *, alpha, eps):
    y = jnp.dot(m_ref[...], w_ref[...], preferred_element_type=F32)
    o_ref[...] = _layer_norm(alpha * x_ref[...] + y, g_ref[...], b_ref[...], eps)


def _proj_ln(mixed, w, x, g, b, *, tm, alpha, eps):
    m, k = mixed.shape
    n = w.shape[1]
    vec = pl.BlockSpec((1, n), lambda i: (0, 0))
    return pl.pallas_call(
        functools.partial(_proj_ln_kernel, alpha=alpha, eps=eps),
        grid=(m // tm,),
        in_specs=[pl.BlockSpec((tm, k), lambda i: (i, 0)),
                  pl.BlockSpec((k, n), lambda i: (0, 0)),
                  pl.BlockSpec((tm, n), lambda i: (i, 0)), vec, vec],
        out_specs=pl.BlockSpec((tm, n), lambda i: (i, 0)),
        out_shape=jax.ShapeDtypeStruct((m, n), F32),
        compiler_params=pltpu.CompilerParams(
            dimension_semantics=("parallel",), vmem_limit_bytes=VMEM_LIMIT),
        name="out_proj_layernorm",
    )(mixed, w, x, g.reshape(1, n), b.reshape(1, n))


N_EXPERTS = 256
N_GROUPS = 8
TOPK_GROUPS = 4
TOP_K = 8
ROUTED_SCALE = 2.5


def _router_kernel(h_ref, wt_ref, bias_ref, idx_ref, wgt_ref):
    tm = h_ref.shape[0]
    per = N_EXPERTS // N_GROUPS
    logits = lax.dot_general(wt_ref[...], h_ref[...], (((1,), (1,)), ((), ())),
                             precision=lax.Precision.HIGHEST, preferred_element_type=F32)
    scores = _sigmoid(logits)
    biased = scores + bias_ref[...]
    neg = -jnp.inf

    grp = biased.reshape(N_GROUPS, per, tm)
    slot = lax.broadcasted_iota(jnp.int32, (N_GROUPS, per, tm), 1)
    top1 = jnp.max(grp, axis=1, keepdims=True)
    first = jnp.min(jnp.where(grp == top1, slot, per), axis=1, keepdims=True)
    top2 = jnp.max(jnp.where(slot == first, neg, grp), axis=1, keepdims=True)
    grp_score = (top1 + top2).reshape(N_GROUPS, tm)

    gid = lax.broadcasted_iota(jnp.int32, (N_GROUPS, tm), 0)
    rank = jnp.zeros((N_GROUPS, tm), jnp.int32)
    for m in range(N_GROUPS):
        s_m = grp_score[m:m + 1, :]
        rank = rank + jnp.where(s_m > grp_score, 1,
                                jnp.where(s_m == grp_score, jnp.where(gid > m, 1, 0), 0))
    keep = (rank < TOPK_GROUPS).reshape(N_GROUPS, 1, tm)
    masked = jnp.where(keep, grp, neg).reshape(N_EXPERTS, tm)

    eid = lax.broadcasted_iota(jnp.int32, (N_EXPERTS, tm), 0)
    idxs, wgts = [], []
    for _ in range(TOP_K):
        best = jnp.max(masked, axis=0, keepdims=True)
        pick = jnp.min(jnp.where(masked == best, eid, N_EXPERTS), axis=0, keepdims=True)
        hit = eid == pick
        idxs.append(pick)
        wgts.append(jnp.sum(jnp.where(hit, scores, 0.0), axis=0, keepdims=True))
        masked = jnp.where(hit, neg, masked)
    wgt = jnp.concatenate(wgts, axis=0)
    wgt = wgt / jnp.sum(wgt, axis=0, keepdims=True) * ROUTED_SCALE
    idx_ref[...] = jnp.concatenate(idxs, axis=0)
    wgt_ref[...] = wgt


def _router(h, w_router_t, bias, *, tm):
    t, d = h.shape
    out = pl.BlockSpec((TOP_K, tm), lambda i: (0, i))
    return pl.pallas_call(
        _router_kernel,
        grid=(t // tm,),
        in_specs=[pl.BlockSpec((tm, d), lambda i: (i, 0)),
                  pl.BlockSpec((N_EXPERTS, d), lambda i: (0, 0)),
                  pl.BlockSpec((N_EXPERTS, 1), lambda i: (0, 0))],
        out_specs=[out, out],
        out_shape=[jax.ShapeDtypeStruct((TOP_K, t), jnp.int32),
                   jax.ShapeDtypeStruct((TOP_K, t), F32)],
        compiler_params=pltpu.CompilerParams(
            dimension_semantics=("parallel",), vmem_limit_bytes=VMEM_LIMIT),
        name="moe_router",
    )(h, w_router_t, bias.reshape(N_EXPERTS, 1))


EXPERT_ROWS = 128


def _silu(x):
    return x * _sigmoid(x)


GATHER_AHEAD = 3
X_SLOTS = 4
Y_SLOTS = 4
SCATTER_LAG = 3
IDX_SLOTS = 8


def _experts_kernel(blk_e_ref, nxt_e_ref, n_used_ref,
                    codes_hbm, h_hbm, wg_hbm, wu_hbm, wd_hbm, ys_hbm,
                    idx_ref, x_buf, y_buf, wg_st, wu_st, wd_st, wg_bf, wu_bf, wd_bf,
                    idx_sem, gat_sem, sct_sem, w_sem, *, n_tokens):
    i = pl.program_id(0)
    n_blk = pl.num_programs(0)
    n_used = n_used_ref[0]
    used = i < n_used
    rows = EXPERT_ROWS
    x_chunks = x_buf.shape[2]
    chunks = y_buf.shape[2]
    assert chunks == x_chunks

    def idx_copy(k):
        slot = k & (IDX_SLOTS - 1)
        return pltpu.make_async_copy(codes_hbm.at[pl.ds(k, 1), :], idx_ref.at[pl.ds(slot, 1), :],
                                     idx_sem.at[slot])

    def start_rows(k, gather):
        window = k & (IDX_SLOTS - 1)
        for j in range(rows):
            if gather:
                buf = k & (X_SLOTS - 1)
                src = idx_ref[window, j]
                pltpu.make_async_copy(h_hbm.at[pl.ds(src, x_chunks), :], x_buf.at[buf, j],
                                      gat_sem.at[buf]).start()
            else:
                buf = k & (Y_SLOTS - 1)
                dst = idx_ref[window, rows + j]
                pltpu.make_async_copy(y_buf.at[buf, j], ys_hbm.at[pl.ds(dst, chunks), :],
                                      sct_sem.at[buf]).start()

    def wait_gather(k):
        buf = k & (X_SLOTS - 1)
        pltpu.make_async_copy(x_buf.at[buf], x_buf.at[buf], gat_sem.at[buf]).wait()

    def wait_scatter(k):
        buf = k & (Y_SLOTS - 1)
        pltpu.make_async_copy(y_buf.at[buf], y_buf.at[buf], sct_sem.at[buf]).wait()

    def weight_copies(e):
        return (pltpu.make_async_copy(wg_hbm.at[e], wg_st, w_sem.at[0]),
                pltpu.make_async_copy(wu_hbm.at[e], wu_st, w_sem.at[1]),
                pltpu.make_async_copy(wd_hbm.at[e], wd_st, w_sem.at[2]))

    def compute(k):
        u = pltpu.einshape("jcl->cjl", x_buf[k & (X_SLOTS - 1)])
        u = jnp.concatenate([u[c] for c in range(x_chunks)], axis=1)
        lo = lax.bitcast_convert_type(u << 16, F32)
        hi = lax.bitcast_convert_type(u & jnp.uint32(0xFFFF0000), F32)
        x = jnp.concatenate([lo, hi], axis=1).astype(BF16)
        hid = _silu(jnp.dot(x, wg_bf[...], preferred_element_type=F32)) * jnp.dot(
            x, wu_bf[...], preferred_element_type=F32)
        y = jnp.dot(hid.astype(BF16), wd_bf[...], preferred_element_type=F32)
        half = y.shape[1] // 2
        bits = lax.bitcast_convert_type(y.astype(BF16).astype(F32), jnp.uint32)
        words = (bits[:, :half] >> 16) | bits[:, half:]
        words = jnp.stack([words[:, c * LANES:(c + 1) * LANES] for c in range(chunks)], axis=0)
        y_buf[k & (Y_SLOTS - 1)] = pltpu.einshape("cjl->jcl", words)

    @pl.when(i == 0)
    def _():
        for cp in weight_copies(blk_e_ref[0]):
            cp.start(priority=1)
        for k in range(GATHER_AHEAD):
            @pl.when(k < n_used)
            def _():
                idx_copy(k).start()
                idx_copy(k).wait()
                start_rows(k, gather=True)

        @pl.when(GATHER_AHEAD < n_used)
        def _():
            idx_copy(GATHER_AHEAD).start()

    @pl.when(i + GATHER_AHEAD + 1 < n_used)
    def _():
        idx_copy(i + GATHER_AHEAD + 1).start()

    has_ahead = i + GATHER_AHEAD < n_used

    @pl.when(has_ahead)
    def _():
        idx_copy(i + GATHER_AHEAD).wait()

    prev_e = blk_e_ref[jnp.maximum(i - 1, 0)]
    fresh = jnp.logical_or(i == 0, blk_e_ref[i] != prev_e)

    @pl.when(jnp.logical_and(used, fresh))
    def _():
        for cp in weight_copies(blk_e_ref[i]):
            cp.wait()
        wg_bf[...] = wg_st[...].astype(BF16)
        wu_bf[...] = wu_st[...].astype(BF16)
        wd_bf[...] = wd_st[...].astype(BF16)

        @pl.when(nxt_e_ref[i] >= 0)
        def _():
            for cp in weight_copies(nxt_e_ref[i]):
                cp.start(priority=1)

    @pl.when(jnp.logical_and(i >= SCATTER_LAG, i - SCATTER_LAG < n_used))
    def _():
        wait_scatter(i - SCATTER_LAG)

    interior = jnp.logical_and(i >= 1, has_ahead)

    @pl.when(interior)
    def _():
        wait_gather(i)
        start_rows(i - 1, gather=False)
        start_rows(i + GATHER_AHEAD, gather=True)
        compute(i)

    @pl.when(jnp.logical_and(used, jnp.logical_not(interior)))
    def _():
        @pl.when(has_ahead)
        def _():
            start_rows(i + GATHER_AHEAD, gather=True)

        @pl.when(i >= 1)
        def _():
            start_rows(i - 1, gather=False)

        wait_gather(i)
        compute(i)

        @pl.when(i + 1 == n_used)
        def _():
            start_rows(i, gather=False)

    @pl.when(i == n_blk - 1)
    def _():
        for back in range(SCATTER_LAG - 1, -1, -1):
            @pl.when(jnp.logical_and(i >= back, i - back < n_used))
            def _():
                wait_scatter(i - back)

        y_buf[...] = jnp.zeros_like(y_buf)
        for half in range(2):
            for j in range(rows):
                first = (n_tokens * TOP_K + half * rows + j) * chunks
                pltpu.make_async_copy(y_buf.at[half, j], ys_hbm.at[pl.ds(first, chunks), :],
                                      sct_sem.at[half]).start()
            wait_scatter(half)


def _pack_rows(h):
    t, d = h.shape
    bits = lax.bitcast_convert_type(h.astype(BF16), jnp.uint16).astype(jnp.uint32)
    words = bits[:, :d // 2] | (bits[:, d // 2:] << 16)
    return words.reshape(t * d // (2 * LANES), LANES)


def _experts(h, codes, blk_e, nxt_e, n_used, w_gate, w_up, w_down):
    d, ff = w_gate.shape[1], w_gate.shape[2]
    x_chunks = d // (2 * LANES)
    t = h.shape[0] // x_chunks
    n_blk = blk_e.shape[0]
    any_spec = pl.BlockSpec(memory_space=pl.ANY)
    grid_spec = pltpu.PrefetchScalarGridSpec(
        num_scalar_prefetch=3,
        grid=(n_blk,),
        in_specs=[any_spec] * 5,
        out_specs=any_spec,
        scratch_shapes=[pltpu.SMEM((IDX_SLOTS, 2 * EXPERT_ROWS), jnp.int32),
                        pltpu.VMEM((X_SLOTS, EXPERT_ROWS, x_chunks, LANES), jnp.uint32),
                        pltpu.VMEM((Y_SLOTS, EXPERT_ROWS, x_chunks, LANES), jnp.uint32),
                        pltpu.VMEM((d, ff), F32), pltpu.VMEM((d, ff), F32), pltpu.VMEM((ff, d), F32),
                        pltpu.VMEM((d, ff), BF16), pltpu.VMEM((d, ff), BF16),
                        pltpu.VMEM((ff, d), BF16),
                        pltpu.SemaphoreType.DMA((IDX_SLOTS,)), pltpu.SemaphoreType.DMA((X_SLOTS,)),
                        pltpu.SemaphoreType.DMA((Y_SLOTS,)), pltpu.SemaphoreType.DMA((3,))])
    return pl.pallas_call(
        functools.partial(_experts_kernel, n_tokens=t),
        grid_spec=grid_spec,
        out_shape=jax.ShapeDtypeStruct(((t * TOP_K + 2 * EXPERT_ROWS) * x_chunks, LANES), jnp.uint32),
        compiler_params=pltpu.CompilerParams(
            dimension_semantics=("arbitrary",), vmem_limit_bytes=VMEM_LIMIT),
        name="routed_experts",
    )(blk_e, nxt_e, n_used, codes, h, w_gate, w_up, w_down)


def _shared_ln_kernel(h_ref, *refs, alpha, eps):
    ys_refs = refs[:TOP_K]
    tw_ref, wg_ref, wu_ref, wd_ref, g_ref, b_ref, o_ref, routed_ref = refs[TOP_K:]
    h = h_ref[...]
    tm, d = h.shape
    chunks = d // (2 * LANES)
    hb = h.astype(BF16)
    hid = _silu(jnp.dot(hb, wg_ref[...], preferred_element_type=F32)) * jnp.dot(
        hb, wu_ref[...], preferred_element_type=F32)
    moe = jnp.dot(hid.astype(BF16), wd_ref[...], preferred_element_type=F32)
    lo = hi = None
    for k in range(TOP_K):
        words = ys_refs[k][...]
        w_k = tw_ref[:, k:k + 1]
        lo_k = lax.bitcast_convert_type(words << 16, F32) * w_k
        hi_k = lax.bitcast_convert_type(words & jnp.uint32(0xFFFF0000), F32) * w_k
        lo = lo_k if lo is None else lo + lo_k
        hi = hi_k if hi is None else hi + hi_k
    routed_ref[0] = lo
    routed_ref[1] = hi
    moe = moe + jnp.concatenate(
        [routed_ref[half, pl.ds(c, tm, stride=chunks), :] for half in range(2)
         for c in range(chunks)], axis=1)
    o_ref[...] = _layer_norm(alpha * h + moe, g_ref[...], b_ref[...], eps)


def _shared_ln(h, ys, top_w, w_gate, w_up, w_down, g, b, *, tm, alpha, eps):
    t, d = h.shape
    ff = w_gate.shape[1]
    chunks = d // (2 * LANES)
    n_tiles = t // tm
    row = pl.BlockSpec((tm, d), lambda i: (i, 0))
    vec = pl.BlockSpec((1, d), lambda i: (0, 0))
    band = lambda k: pl.BlockSpec((tm * chunks, LANES), lambda i: (k * n_tiles + i, 0))
    return pl.pallas_call(
        functools.partial(_shared_ln_kernel, alpha=alpha, eps=eps),
        grid=(n_tiles,),
        in_specs=[row] + [band(k) for k in range(TOP_K)] + [
            pl.BlockSpec((tm * chunks, TOP_K), lambda i: (i, 0)),
            pl.BlockSpec((d, ff), lambda i: (0, 0)), pl.BlockSpec((d, ff), lambda i: (0, 0)),
            pl.BlockSpec((ff, d), lambda i: (0, 0)), vec, vec],
        out_specs=row,
        out_shape=jax.ShapeDtypeStruct((t, d), F32),
        scratch_shapes=[pltpu.VMEM((2, tm * chunks, LANES), F32)],
        compiler_params=pltpu.CompilerParams(
            dimension_semantics=("parallel",), vmem_limit_bytes=VMEM_LIMIT),
        name="shared_expert_layernorm",
    )(h, *([ys] * TOP_K), top_w, w_gate, w_up, w_down, g.reshape(1, d), b.reshape(1, d))


LN_EPS = 1e-5
ROPE_THETA = 10000.0


def _rope_tables(seq, dim):
    inv = 1.0 / (ROPE_THETA ** (jnp.arange(0, dim, 2, dtype=F32) / dim))
    ang = jnp.arange(seq, dtype=F32)[:, None] * inv[None, :]
    ang = jnp.concatenate([ang, ang], axis=-1)
    sign = jnp.where(jnp.arange(dim) < dim // 2, -1.0, 1.0).astype(F32)
    return jnp.cos(ang), jnp.sin(ang) * sign


def _pad_lora(x, d_decay, d_aaa, axis):
    assert d_decay <= LORA_PAD and d_aaa <= LORA_PAD
    decay, aaa, gate = jnp.split(x, [d_decay, d_decay + d_aaa], axis=axis)
    pad = lambda p: jnp.pad(p, [(0, LORA_PAD - p.shape[a]) if a == axis % x.ndim else (0, 0)
                                for a in range(x.ndim)])
    return jnp.concatenate([pad(decay), pad(aaa), gate], axis=axis)


def _dispatch(top_idx, n_tokens, src_stride, dst_stride):
    n_assign = n_tokens * TOP_K
    i32 = jnp.int32
    e_flat = top_idx.reshape(n_assign).astype(i32)
    _, order = lax.sort((e_flat, jnp.arange(n_assign, dtype=i32)), num_keys=1, is_stable=True)
    experts = jnp.arange(N_EXPERTS, dtype=i32)
    counts = jnp.sum((e_flat.reshape(-1, LANES, 1) == experts).astype(i32), axis=(0, 1))
    grp_start = jnp.cumsum(counts) - counts
    padded = (counts + EXPERT_ROWS - 1) // EXPERT_ROWS * EXPERT_ROWS
    pad_end = jnp.cumsum(padded)
    pad_start = pad_end - padded
    n_blk = -(-n_assign // EXPERT_ROWS) + N_EXPERTS
    n_used = (pad_end[-1] // EXPERT_ROWS).astype(i32)
    first_row = jnp.arange(n_blk, dtype=i32) * EXPERT_ROWS
    blk_e = jnp.minimum(jnp.sum((pad_end[None, :] <= first_row[:, None]).astype(i32), axis=1),
                        N_EXPERTS - 1)
    into = first_row - pad_start[blk_e]
    blk_s0 = jnp.clip(grp_start[blk_e] + into, 0, n_assign)
    blk_nv = jnp.clip(counts[blk_e] - into, 0, EXPERT_ROWS)
    later = jnp.where(counts > 0, experts, N_EXPERTS)
    nxt = lax.cummin(jnp.concatenate([later[1:], jnp.full((1,), N_EXPERTS, i32)]), reverse=True)
    nxt_e = jnp.where(nxt < N_EXPERTS, nxt, -1)[blk_e]
    row = jnp.arange(EXPERT_ROWS, dtype=i32)[None, :]
    src = jnp.minimum(blk_s0[:, None] + row, n_assign - 1)
    pad = n_assign + (jnp.arange(n_blk, dtype=i32)[:, None] % 2) * EXPERT_ROWS + row
    valid = row < blk_nv[:, None]
    ids = order[src]
    codes = jnp.concatenate([jnp.where(valid, ids % n_tokens, 0) * src_stride,
                             jnp.where(valid, ids, pad) * dst_stride], axis=1)
    return codes.astype(i32), blk_e, nxt_e.astype(i32), n_used.reshape(1)


def kernel(x, w_in, shift_mu, w_lora_up, w0, a_lora_up, a0, g_lora_up, k_k, k_a, r_k, ln_x_w, ln_x_b,
           w_o_rwkv, w_o_moba, w_out, ln1_g, ln1_b, w_router, router_bias, w_gate_e, w_up_e,
           w_down_e, w_gate_s, w_up_s, w_down_s, ln2_g, ln2_b):
    batch, seq, d = x.shape
    depth = w_in.shape[0]
    alpha = (2 * depth) ** 0.25
    t = batch * seq
    c = d
    shift_w = shift_mu.shape[1]
    moba_heads = d // MOBA_HEAD
    cos, sin_signed = _rope_tables(seq, MOBA_HEAD)

    h = x.reshape(t, d)
    for layer in range(depth):
        wl = w_in[layer].astype(BF16)
        hb = h.astype(BF16)
        d_decay, d_aaa = w_lora_up.shape[1], a_lora_up.shape[1]
        w_lora = _pad_lora(wl[:, 3 * c:shift_w], d_decay, d_aaa, axis=1)
        assert w_lora.shape[1] % LANES == 0
        u_rkv = _matmul(hb, wl[:, :3 * c], tm=1024, tn=1024)
        u_lora = _matmul(hb, w_lora, tm=1024, tn=w_lora.shape[1])
        u_qkv = _matmul(hb, wl[:, shift_w:shift_w + 3 * d], tm=1024, tn=1024)
        u_gate = _matmul(hb, wl[:, shift_w + 3 * d:], tm=1024, tn=1024)

        pad_rows = lambda p: jnp.pad(p, ((0, LORA_PAD - p.shape[0]), (0, 0)))
        o_rwkv = _rwkv_mix(u_rkv, u_lora, shift_mu[layer][:3 * c],
                           _pad_lora(shift_mu[layer][3 * c:], d_decay, d_aaa, axis=0),
                           pad_rows(w_lora_up[layer]), w0[layer], pad_rows(a_lora_up[layer]),
                           a0[layer], g_lora_up[layer], k_k[layer], k_a[layer], r_k[layer],
                           ln_x_w[layer], ln_x_b[layer], batch=batch, seq=seq)
        o_moba = _moba(u_qkv, cos, sin_signed, batch=batch, seq=seq, heads=moba_heads)

        mixed = _merge(o_rwkv, o_moba, w_o_rwkv[layer].astype(BF16), w_o_moba[layer].astype(BF16),
                       u_gate, tm=512, tn=1024)
        h = _proj_ln(mixed, w_out[layer].astype(BF16), h, ln1_g[layer], ln1_b[layer],
                     tm=512, alpha=alpha, eps=LN_EPS)

        top_idx, top_w = _router(h, w_router[layer].T, router_bias[layer], tm=512)
        chunks = d // LANES
        codes, blk_e, nxt_e, n_used = _dispatch(top_idx, t, chunks // 2, chunks // 2)
        ys = _experts(_pack_rows(h), codes, blk_e, nxt_e, n_used,
                      w_gate_e[layer], w_up_e[layer], w_down_e[layer])
        h = _shared_ln(h, ys, jnp.repeat(top_w.T, chunks // 2, axis=0), w_gate_s[layer].astype(BF16),
                       w_up_s[layer].astype(BF16), w_down_s[layer].astype(BF16), ln2_g[layer],
                       ln2_b[layer], tm=128, alpha=alpha, eps=LN_EPS)
    return h.reshape(batch, seq, d)
```

```python
import functools

import jax
import jax.numpy as jnp
from jax import lax
from jax.experimental import pallas as pl
from jax.experimental.pallas import tpu as pltpu

F32 = jnp.float32
BF16 = jnp.bfloat16

LANES = 128
WKV_HEAD = 64
WKV_CHUNK = 64


def _dot(a, b):
    return jnp.dot(a.astype(BF16), b.astype(BF16), preferred_element_type=F32)


def _dot_nt(a, b):
    return lax.dot_general(a.astype(BF16), b.astype(BF16), (((1,), (1,)), ((), ())),
                           preferred_element_type=F32)


assert WKV_CHUNK == WKV_HEAD
def _each(f, *lists):
    return [f(*xs) for xs in zip(*lists)]


def _wkv_pairs(r, ld, k, v, a, b, m_state):
    c = WKV_CHUNK
    cat0 = lambda *xs: jnp.concatenate(xs, axis=0)
    cat1 = lambda *xs: jnp.concatenate(xs, axis=1)
    lane = lax.broadcasted_iota(jnp.int32, (c, LANES), 1)
    step = lax.broadcasted_iota(jnp.int32, (c, LANES), 0)
    head0 = lane < WKV_HEAD
    other = jnp.where(head0, lane, lane - WKV_HEAD)

    def stack(x):
        x = x.astype(BF16)
        zero = jnp.zeros_like(x)
        return cat0(jnp.where(head0, x, zero), jnp.where(head0, zero, x))

    def cumsum_rows(x):
        shift = 1
        while shift < c:
            x = x + jnp.where(step >= shift, pltpu.roll(x, shift, axis=0), 0.0)
            shift *= 2
        return x

    dot = lambda p, q: jnp.dot(p.astype(BF16), q, preferred_element_type=F32)

    cw = _each(cumsum_rows, ld)
    e_out = _each(lambda x: jnp.exp(-x), cw)
    a_t = _each(lambda x, w, l: x * jnp.exp(w - l), a, cw, ld)
    r_t = _each(lambda x, w: x * jnp.exp(w), r, cw)
    bk_st = _each(lambda p, q, e: cat0(stack(p * e), stack(q * e)), b, k, e_out)
    v_st = _each(stack, v)
    aa = _each(lambda p, q, s: _dot_nt(cat0(p, q), s), a_t, r_t, bk_st)

    strict = step > other
    incl = step >= other
    n_ab = _each(lambda x: jnp.where(strict, x[:c, :LANES], 0.0), aa)
    a_ak = _each(lambda x: jnp.where(strict, x[:c, LANES:], 0.0), aa)
    a_rb = _each(lambda x: jnp.where(incl, x[c:, :LANES], 0.0), aa)
    a_rk = _each(lambda x: jnp.where(incl, x[c:, LANES:], 0.0), aa)

    eye = jnp.where(step == other, 1.0, 0.0)
    t_inv = _each(lambda n: eye + n, n_ab)
    w1 = _each(dot, a_ak, v_st)
    n_pow = _each(lambda n: dot(n, stack(n)), n_ab)
    for _ in range(c.bit_length() - 3):
        pt = _each(lambda n, t: dot(n, cat1(stack(n), stack(t))), n_pow, t_inv)
        n_pow = _each(lambda x: x[:, :LANES], pt)
        t_inv = _each(lambda t, x: t + x[:, LANES:], t_inv, pt)
    t_inv = _each(lambda t, n: t + dot(n, stack(t)), t_inv, n_pow)

    tw = _each(lambda t, w, x: dot(t, cat1(stack(w), stack(x))), t_inv, w1, a_t)
    rhs = _each(lambda x, v_: cat0(cat1(stack(x[:, LANES:]), stack(x[:, :LANES])),
                                   cat1(jnp.zeros_like(v_), v_)), tw, v_st)
    out_p = _each(lambda p, q, s: dot(cat1(p, q), s), a_rb, a_rk, rhs)
    tail = _each(lambda w: jnp.exp(w[c - 1:c, :] - w), cw)
    bkp_st = _each(lambda p, q, e: cat0(stack(p * e), stack(q * e)), b, k, tail)
    upd = _each(lambda p, s: lax.dot_general(p, s, (((0,), (0,)), ((), ())),
                                             preferred_element_type=F32), bkp_st, rhs)
    upd_p = _each(lambda u: u[:WKV_HEAD] + u[WKV_HEAD:], upd)
    r_hat = _each(lambda x, o: x + o[:, :LANES], r_t, out_p)
    g = _each(lambda w, u: jnp.where(step == other, jnp.exp(w[c - 1:c, :]), 0.0) + u[:, :LANES],
              cw, upd_p)

    def apply_state(lhs, m):
        l_hi = lhs.astype(BF16)
        l_lo = (lhs - l_hi.astype(F32)).astype(BF16)
        m_hi = m.astype(BF16)
        m_lo = m - m_hi.astype(F32)
        both = jnp.dot(l_hi, cat1(stack(m_hi), stack(m_lo)), preferred_element_type=F32)
        return (both[:, :LANES] + both[:, LANES:]) + jnp.dot(l_lo, stack(m_hi),
                                                             preferred_element_type=F32)

    rm = _each(lambda p, q, m: apply_state(cat0(p, q), m), r_hat, g, m_state)
    y = _each(lambda x, o: x[:c] + o[:, LANES:], rm, out_p)
    m_new = _each(lambda x, u: x[c:] + u[:, LANES:], rm, upd_p)
    return y, m_new


RWKV_GN_EPS = 64e-5
LORA_PAD = LANES
RWKV_SEQS_PER_STEP = 1


def _rwkv_kernel(u_ref, ul_ref, mu_ref, mul_ref, wl_ref, w0_ref, al_ref, a0_ref, gl_ref, kk_ref,
                 ka_ref, rk_ref, lnw_ref, lnb_ref, o_ref, m_ref, prev_ref, prevl_ref):
    c = WKV_CHUNK
    n_seq, _, width = o_ref.shape

    @pl.when(pl.program_id(1) == 0)
    def _():
        m_ref[...] = jnp.zeros_like(m_ref)
        prev_ref[...] = jnp.zeros_like(prev_ref)
        prevl_ref[...] = jnp.zeros_like(prevl_ref)

    def shifted(u, prev, mu):
        first = lax.broadcasted_iota(jnp.int32, u.shape, 0) == 0
        before = jnp.where(first, prev[7:8, :], pltpu.roll(u, 1, axis=0))
        return u + (before - u) * mu

    lanes = [slice(p * LANES, (p + 1) * LANES) for p in range(width // LANES)]
    split = lambda x: [x[:, sl] for sl in lanes]
    row = lax.broadcasted_iota(jnp.int32, (LANES, LANES), 0)
    col = lax.broadcasted_iota(jnp.int32, (LANES, LANES), 1)
    same_head = jnp.where((row < WKV_HEAD) == (col < WKV_HEAD), 1.0, 0.0).astype(BF16)

    def head_sums(x):
        stacked = jnp.concatenate(split(x), axis=0).astype(BF16)
        s = jnp.dot(stacked, same_head, preferred_element_type=F32)
        return [s[p * c:(p + 1) * c] for p in range(len(lanes))]

    def prepare(s):
        u = u_ref[s]
        ul = ul_ref[s]
        z = shifted(u, prev_ref[s], mu_ref[...])
        zl = shifted(ul, prevl_ref[s], mul_ref[...])
        prev_ref[s] = u[c - 8:, :]
        prevl_ref[s] = ul[c - 8:, :]
        r = z[:, :width]
        k_raw = z[:, width:2 * width]
        v = z[:, 2 * width:]
        w = w0_ref[...] + _dot(jnp.tanh(zl[:, :LORA_PAD]), wl_ref[...])
        ld = -jnp.exp(-(jnp.maximum(-w, 0.0) + jnp.log(1.0 + jnp.exp(-jnp.abs(w)))) - 0.5)
        a = _sigmoid(a0_ref[...] + _dot(zl[:, LORA_PAD:2 * LORA_PAD], al_ref[...]))
        g = _dot(_sigmoid(zl[:, 2 * LORA_PAD:]), gl_ref[...])
        kk_raw = k_raw * kk_ref[...]
        k = k_raw * (1.0 + (a - 1.0) * ka_ref[...])
        norm2 = head_sums(kk_raw * kk_raw)
        kk = _each(lambda x, n2: x * lax.rsqrt(jnp.maximum(n2, 1e-24)), split(kk_raw), norm2)
        bonus = head_sums(r * k * rk_ref[...])
        return r, ld, k, v, kk, a, g, bonus

    def recur(s, r, ld, k, v, kk, a):
        m_state = [m_ref[s, p] for p in range(len(lanes))]
        y, m_new = _wkv_pairs(split(r), split(ld), split(k), split(v), _each(lambda x: -x, kk),
                              _each(lambda x, a_: x * a_, kk, split(a)), m_state)
        for p in range(len(lanes)):
            m_ref[s, p] = m_new[p]
        return jnp.concatenate(y, axis=1)

    def finish(s, y, v, g, bonus):
        inv_n = 1.0 / WKV_HEAD
        mean = jnp.concatenate(head_sums(y), axis=1) * inv_n
        yc = y - mean
        var = jnp.concatenate(head_sums(yc * yc), axis=1) * inv_n
        yn = yc * lax.rsqrt(var + RWKV_GN_EPS) * lnw_ref[...] + lnb_ref[...]
        out = (yn + jnp.concatenate(bonus, axis=1) * v) * g
        o_ref[s] = out.astype(o_ref.dtype)

    prepared = [prepare(s) for s in range(n_seq)]
    ys = [recur(s, *prepared[s][:6]) for s in range(n_seq)]
    for s in range(n_seq):
        _, _, _, v, _, _, g, bonus = prepared[s]
        finish(s, ys[s], v, g, bonus)


def _rwkv_mix(u_rkv, u_lora, mu_rkv, mu_lora, w_lora, w0, a_lora, a0, g_lora, k_k, k_a, r_k,
              ln_w, ln_b, *, batch, seq):
    t = u_rkv.shape[0]
    width = u_rkv.shape[1] // 3
    n_lora = u_lora.shape[1]
    per = RWKV_SEQS_PER_STEP
    assert t == batch * seq and seq % WKV_CHUNK == 0 and width % LANES == 0 and batch % per == 0
    n_chunks = seq // WKV_CHUNK
    rows = lambda n: pl.BlockSpec((per, WKV_CHUNK, n), lambda bi, ci: (bi, ci, 0))
    whole = lambda x: pl.BlockSpec(x.shape, lambda bi, ci: (0,) * x.ndim)
    vec = lambda x: x.reshape(1, -1)
    params = [vec(mu_rkv), vec(mu_lora), w_lora.astype(BF16), vec(w0), a_lora.astype(BF16), vec(a0),
              g_lora.astype(BF16), vec(k_k), vec(k_a), vec(r_k), vec(ln_w), vec(ln_b)]
    out = pl.pallas_call(
        _rwkv_kernel,
        grid=(batch // per, n_chunks),
        in_specs=[rows(3 * width), rows(n_lora)] + [whole(p) for p in params],
        out_specs=rows(width),
        out_shape=jax.ShapeDtypeStruct((batch, seq, width), BF16),
        scratch_shapes=[pltpu.VMEM((per, width // LANES, WKV_HEAD, LANES), F32),
                        pltpu.VMEM((per, 8, 3 * width), F32), pltpu.VMEM((per, 8, n_lora), F32)],
        compiler_params=pltpu.CompilerParams(
            dimension_semantics=("parallel", "arbitrary"), vmem_limit_bytes=VMEM_LIMIT),
        name="rwkv7_time_mix",
    )(u_rkv.reshape(batch, seq, 3 * width), u_lora.reshape(batch, seq, n_lora), *params)
    return out.reshape(t, width)


MOBA_BLOCK = 256
MOBA_TOPK = 3
MOBA_HEAD = 128


def _rope(t, cos, sin_signed):
    return t * cos + pltpu.roll(t, MOBA_HEAD // 2, axis=1) * sin_signed


MOBA_HEADS_PER_STEP = 2


def _moba_kernel(q_ref, k_ref, v_ref, cos_ref, sin_ref, o_ref, kr_ref, vb_ref, kmean_ref, *,
                 nb, scale):
    blk = MOBA_BLOCK
    heads = [slice(g * MOBA_HEAD, (g + 1) * MOBA_HEAD) for g in range(MOBA_HEADS_PER_STEP)]
    kmean_ref[...] = jnp.zeros_like(kmean_ref)
    for n in range(nb):
        rows = slice(n * blk, (n + 1) * blk)
        for g, hd in enumerate(heads):
            kr = _rope(k_ref[rows, hd], cos_ref[rows, :], sin_ref[rows, :])
            kr_ref[g, rows, :] = kr.astype(BF16)
            vb_ref[g, rows, :] = v_ref[rows, hd].astype(BF16)
            kmean_ref[g, n:n + 1, :] = jnp.mean(kr, axis=0, keepdims=True)

    lane = lax.broadcasted_iota(jnp.int32, (blk, LANES), 1)
    row = lax.broadcasted_iota(jnp.int32, (blk, blk), 0)
    col = lax.broadcasted_iota(jnp.int32, (blk, blk), 1)
    causal = row >= col

    def masked_scores(i, g, hd):
        rows = slice(i * blk, (i + 1) * blk)
        q = _rope(q_ref[rows, hd], cos_ref[rows, :], sin_ref[rows, :])
        s = _dot_nt(q, kr_ref[g, :(i + 1) * blk, :]) * scale
        if i > MOBA_TOPK:
            gate = lax.dot_general(q, kmean_ref[g], (((1,), (1,)), ((), ())),
                                   precision=lax.Precision.HIGHEST, preferred_element_type=F32)
            gate = jnp.where(lane < i, gate, -jnp.inf)
            rank = jnp.zeros((blk, LANES), jnp.int32)
            for m in range(i):
                g_m = gate[:, m:m + 1]
                rank = rank + jnp.where(
                    g_m > gate, 1, jnp.where(g_m == gate, jnp.where(lane > m, 1, 0), 0))
            chosen = jnp.where(rank < MOBA_TOPK, 1.0, 0.0)
            parts = [jnp.where(chosen[:, n:n + 1] > 0.5, s[:, n * blk:(n + 1) * blk], -jnp.inf)
                     for n in range(i)]
        else:
            parts = [s[:, n * blk:(n + 1) * blk] for n in range(i)]
        parts.append(jnp.where(causal, s[:, i * blk:], -jnp.inf))
        return jnp.concatenate(parts, axis=1) if i else parts[0]

    for i in range(nb):
        rows = slice(i * blk, (i + 1) * blk)
        scores = [masked_scores(i, g, hd) for g, hd in enumerate(heads)]
        for g, hd in enumerate(heads):
            s = scores[g]
            m_row = jnp.max(s, axis=1, keepdims=True)
            p = jnp.exp(s - m_row)
            l_row = jnp.sum(p, axis=1, keepdims=True)
            acc = jnp.dot(p.astype(BF16), vb_ref[g, :(i + 1) * blk, :], preferred_element_type=F32)
            o_ref[rows, hd] = (acc / l_row).astype(o_ref.dtype)


def _moba(u_qkv, cos, sin_signed, *, batch, seq, heads):
    t = u_qkv.shape[0]
    per = MOBA_HEADS_PER_STEP
    assert seq % MOBA_BLOCK == 0 and heads % per == 0
    nb = seq // MOBA_BLOCK
    assert nb <= LANES
    kernel = functools.partial(_moba_kernel, nb=nb, scale=MOBA_HEAD ** -0.5)
    groups = heads // per
    full = lambda off: pl.BlockSpec((seq, per * MOBA_HEAD), lambda bi, hi: (bi, off + hi))
    table = pl.BlockSpec((seq, MOBA_HEAD), lambda bi, hi: (0, 0))
    return pl.pallas_call(
        kernel,
        grid=(batch, groups),
        in_specs=[full(0), full(groups), full(2 * groups), table, table],
        out_specs=full(0),
        out_shape=jax.ShapeDtypeStruct((t, heads * MOBA_HEAD), BF16),
        scratch_shapes=[pltpu.VMEM((per, seq, MOBA_HEAD), BF16), pltpu.VMEM((per, seq, MOBA_HEAD), BF16),
                        pltpu.VMEM((per, LANES, MOBA_HEAD), F32)],
        compiler_params=pltpu.CompilerParams(dimension_semantics=("parallel", "parallel")),
        name="moba_attention",
    )(u_qkv, u_qkv, u_qkv, cos, sin_signed)


VMEM_LIMIT = 56 * 1024 * 1024


def _matmul_kernel(x_ref, w_ref, o_ref):
    o_ref[...] = jnp.dot(x_ref[...], w_ref[...], preferred_element_type=F32).astype(o_ref.dtype)


def _matmul(x, w, *, tm, tn, out_dtype=F32):
    m, k = x.shape
    n = w.shape[1]
    assert m % tm == 0 and n % tn == 0
    return pl.pallas_call(
        _matmul_kernel,
        grid=(m // tm, n // tn),
        in_specs=[pl.BlockSpec((tm, k), lambda i, j: (i, 0)),
                  pl.BlockSpec((k, tn), lambda i, j: (0, j))],
        out_specs=pl.BlockSpec((tm, tn), lambda i, j: (i, j)),
        out_shape=jax.ShapeDtypeStruct((m, n), out_dtype),
        compiler_params=pltpu.CompilerParams(
            dimension_semantics=("parallel", "parallel"), vmem_limit_bytes=VMEM_LIMIT),
        name="matmul",
    )(x, w)


def _sigmoid(x):
    return 1.0 / (1.0 + jnp.exp(-x))


def _merge_kernel(a_ref, b_ref, pa_ref, pb_ref, ga_ref, gb_ref, o_ref):
    ya = jnp.dot(a_ref[...], pa_ref[...], preferred_element_type=F32)
    yb = jnp.dot(b_ref[...], pb_ref[...], preferred_element_type=F32)
    o_ref[...] = (_sigmoid(ga_ref[...]) * ya + _sigmoid(gb_ref[...]) * yb).astype(o_ref.dtype)


def _merge(o_a, o_b, p_a, p_b, gates, *, tm, tn):
    m, k = o_a.shape
    n = p_a.shape[1]
    nj = n // tn
    row = pl.BlockSpec((tm, k), lambda i, j: (i, 0))
    wcol = pl.BlockSpec((k, tn), lambda i, j: (0, j))
    return pl.pallas_call(
        _merge_kernel,
        grid=(m // tm, nj),
        in_specs=[row, row, wcol, wcol,
                  pl.BlockSpec((tm, tn), lambda i, j: (i, j)),
                  pl.BlockSpec((tm, tn), lambda i, j: (i, nj + j))],
        out_specs=pl.BlockSpec((tm, tn), lambda i, j: (i, j)),
        out_shape=jax.ShapeDtypeStruct((m, n), BF16),
        compiler_params=pltpu.CompilerParams(
            dimension_semantics=("parallel", "parallel"), vmem_limit_bytes=VMEM_LIMIT),
        name="gated_merge",
    )(o_a, o_b, p_a, p_b, gates, gates)


def _layer_norm(x, g, b, eps):
    mu = jnp.mean(x, axis=-1, keepdims=True)
    xc = x - mu
    var = jnp.mean(xc * xc, axis=-1, keepdims=True)
    return xc * lax.rsqrt(var + eps) * g + b


def _pack_words(x):
    half = x.shape[1] // 2
    bits = lax.bitcast_convert_type(x.astype(BF16).astype(F32), jnp.uint32)
    words = (bits[:, :half] >> 16) | bits[:, half:]
    words = jnp.stack([words[:, c * LANES:(c + 1) * LANES] for c in range(half // LANES)], axis=0)
    return pltpu.einshape("cjl->jcl", words)


def _proj_ln_kernel(m_ref, w_ref, x_ref, g_ref, b_ref, o_ref, packed_ref, *, alpha, eps):
    y = jnp.dot(m_ref[...], w_ref[...], preferred_element_type=F32)
    out = _layer_norm(alpha * x_ref[...] + y, g_ref[...], b_ref[...], eps)
    o_ref[...] = out
    packed_ref[...] = _pack_words(out)


def _proj_ln(mixed, w, x, g, b, *, tm, alpha, eps):
    m, k = mixed.shape
    n = w.shape[1]
    chunks = n // (2 * LANES)
    vec = pl.BlockSpec((1, n), lambda i: (0, 0))
    h, packed = pl.pallas_call(
        functools.partial(_proj_ln_kernel, alpha=alpha, eps=eps),
        grid=(m // tm,),
        in_specs=[pl.BlockSpec((tm, k), lambda i: (i, 0)),
                  pl.BlockSpec((k, n), lambda i: (0, 0)),
                  pl.BlockSpec((tm, n), lambda i: (i, 0)), vec, vec],
        out_specs=[pl.BlockSpec((tm, n), lambda i: (i, 0)),
                   pl.BlockSpec((tm, chunks, LANES), lambda i: (i, 0, 0))],
        out_shape=[jax.ShapeDtypeStruct((m, n), F32),
                   jax.ShapeDtypeStruct((m, chunks, LANES), jnp.uint32)],
        compiler_params=pltpu.CompilerParams(
            dimension_semantics=("parallel",), vmem_limit_bytes=VMEM_LIMIT),
        name="out_proj_layernorm",
    )(mixed, w, x, g.reshape(1, n), b.reshape(1, n))
    return h, packed.reshape(m * chunks, LANES)


N_EXPERTS = 256
N_GROUPS = 8
TOPK_GROUPS = 4
TOP_K = 8
ROUTED_SCALE = 2.5


def _router_kernel(h_ref, wt_ref, bias_ref, idx_ref, wgt_ref):
    tm = h_ref.shape[0]
    per = N_EXPERTS // N_GROUPS
    logits = lax.dot_general(wt_ref[...], h_ref[...], (((1,), (1,)), ((), ())),
                             precision=lax.Precision.HIGHEST, preferred_element_type=F32)
    scores = _sigmoid(logits)
    biased = scores + bias_ref[...]
    neg = -jnp.inf

    grp = biased.reshape(N_GROUPS, per, tm)
    slot = lax.broadcasted_iota(jnp.int32, (N_GROUPS, per, tm), 1)
    top1 = jnp.max(grp, axis=1, keepdims=True)
    first = jnp.min(jnp.where(grp == top1, slot, per), axis=1, keepdims=True)
    top2 = jnp.max(jnp.where(slot == first, neg, grp), axis=1, keepdims=True)
    grp_score = (top1 + top2).reshape(N_GROUPS, tm)

    gid = lax.broadcasted_iota(jnp.int32, (N_GROUPS, tm), 0)
    rank = jnp.zeros((N_GROUPS, tm), jnp.int32)
    for m in range(N_GROUPS):
        s_m = grp_score[m:m + 1, :]
        rank = rank + jnp.where(s_m > grp_score, 1,
                                jnp.where(s_m == grp_score, jnp.where(gid > m, 1, 0), 0))
    keep = (rank < TOPK_GROUPS).reshape(N_GROUPS, 1, tm)
    masked = jnp.where(keep, grp, neg).reshape(N_EXPERTS, tm)

    eid = lax.broadcasted_iota(jnp.int32, (N_EXPERTS, tm), 0)
    idxs, wgts = [], []
    for _ in range(TOP_K):
        best = jnp.max(masked, axis=0, keepdims=True)
        pick = jnp.min(jnp.where(masked == best, eid, N_EXPERTS), axis=0, keepdims=True)
        hit = eid == pick
        idxs.append(pick)
        wgts.append(jnp.sum(jnp.where(hit, scores, 0.0), axis=0, keepdims=True))
        masked = jnp.where(hit, neg, masked)
    wgt = jnp.concatenate(wgts, axis=0)
    wgt = wgt / jnp.sum(wgt, axis=0, keepdims=True) * ROUTED_SCALE
    idx_ref[...] = jnp.concatenate(idxs, axis=0)
    wgt_ref[...] = wgt


def _router(h, w_router_t, bias, *, tm):
    t, d = h.shape
    out = pl.BlockSpec((TOP_K, tm), lambda i: (0, i))
    return pl.pallas_call(
        _router_kernel,
        grid=(t // tm,),
        in_specs=[pl.BlockSpec((tm, d), lambda i: (i, 0)),
                  pl.BlockSpec((N_EXPERTS, d), lambda i: (0, 0)),
                  pl.BlockSpec((N_EXPERTS, 1), lambda i: (0, 0))],
        out_specs=[out, out],
        out_shape=[jax.ShapeDtypeStruct((TOP_K, t), jnp.int32),
                   jax.ShapeDtypeStruct((TOP_K, t), F32)],
        compiler_params=pltpu.CompilerParams(
            dimension_semantics=("parallel",), vmem_limit_bytes=VMEM_LIMIT),
        name="moe_router",
    )(h, w_router_t, bias.reshape(N_EXPERTS, 1))


EXPERT_ROWS = 128


def _silu(x):
    return x * _sigmoid(x)


GATHER_AHEAD = 3
X_SLOTS = 4
Y_SLOTS = 4
SCATTER_LAG = 3
IDX_SLOTS = 8


def _experts_kernel(blk_e_ref, nxt_e_ref, n_used_ref,
                    codes_hbm, h_hbm, wg_hbm, wu_hbm, wd_hbm, ys_hbm,
                    idx_ref, x_buf, y_buf, wg_st, wu_st, wd_st, wg_bf, wu_bf, wd_bf,
                    idx_sem, gat_sem, sct_sem, w_sem, *, n_tokens):
    i = pl.program_id(0)
    n_blk = pl.num_programs(0)
    n_used = n_used_ref[0]
    used = i < n_used
    rows = EXPERT_ROWS
    x_chunks = x_buf.shape[2]
    chunks = y_buf.shape[2]
    assert chunks == x_chunks

    def idx_copy(k):
        slot = k & (IDX_SLOTS - 1)
        return pltpu.make_async_copy(codes_hbm.at[pl.ds(k, 1), :], idx_ref.at[pl.ds(slot, 1), :],
                                     idx_sem.at[slot])

    def start_rows(k, gather):
        window = k & (IDX_SLOTS - 1)
        for j in range(rows):
            if gather:
                buf = k & (X_SLOTS - 1)
                src = idx_ref[window, j]
                pltpu.make_async_copy(h_hbm.at[pl.ds(src, x_chunks), :], x_buf.at[buf, j],
                                      gat_sem.at[buf]).start()
            else:
                buf = k & (Y_SLOTS - 1)
                dst = idx_ref[window, rows + j]
                pltpu.make_async_copy(y_buf.at[buf, j], ys_hbm.at[pl.ds(dst, chunks), :],
                                      sct_sem.at[buf]).start()

    def wait_gather(k):
        buf = k & (X_SLOTS - 1)
        pltpu.make_async_copy(x_buf.at[buf], x_buf.at[buf], gat_sem.at[buf]).wait()

    def wait_scatter(k):
        buf = k & (Y_SLOTS - 1)
        pltpu.make_async_copy(y_buf.at[buf], y_buf.at[buf], sct_sem.at[buf]).wait()

    def weight_copies(e):
        return (pltpu.make_async_copy(wg_hbm.at[e], wg_st, w_sem.at[0]),
                pltpu.make_async_copy(wu_hbm.at[e], wu_st, w_sem.at[1]),
                pltpu.make_async_copy(wd_hbm.at[e], wd_st, w_sem.at[2]))

    def compute(k):
        u = pltpu.einshape("jcl->cjl", x_buf[k & (X_SLOTS - 1)])
        u = jnp.concatenate([u[c] for c in range(x_chunks)], axis=1)
        lo = lax.bitcast_convert_type(u << 16, F32)
        hi = lax.bitcast_convert_type(u & jnp.uint32(0xFFFF0000), F32)
        x = jnp.concatenate([lo, hi], axis=1).astype(BF16)
        hid = _silu(jnp.dot(x, wg_bf[...], preferred_element_type=F32)) * jnp.dot(
            x, wu_bf[...], preferred_element_type=F32)
        y = jnp.dot(hid.astype(BF16), wd_bf[...], preferred_element_type=F32)
        y_buf[k & (Y_SLOTS - 1)] = _pack_words(y)

    @pl.when(i == 0)
    def _():
        for cp in weight_copies(blk_e_ref[0]):
            cp.start(priority=1)
        for k in range(GATHER_AHEAD):
            @pl.when(k < n_used)
            def _():
                idx_copy(k).start()
                idx_copy(k).wait()
                start_rows(k, gather=True)

        @pl.when(GATHER_AHEAD < n_used)
        def _():
            idx_copy(GATHER_AHEAD).start()

    @pl.when(i + GATHER_AHEAD + 1 < n_used)
    def _():
        idx_copy(i + GATHER_AHEAD + 1).start()

    has_ahead = i + GATHER_AHEAD < n_used

    @pl.when(has_ahead)
    def _():
        idx_copy(i + GATHER_AHEAD).wait()

    prev_e = blk_e_ref[jnp.maximum(i - 1, 0)]
    fresh = jnp.logical_or(i == 0, blk_e_ref[i] != prev_e)

    @pl.when(jnp.logical_and(used, fresh))
    def _():
        for cp in weight_copies(blk_e_ref[i]):
            cp.wait()
        wg_bf[...] = wg_st[...].astype(BF16)
        wu_bf[...] = wu_st[...].astype(BF16)
        wd_bf[...] = wd_st[...].astype(BF16)

        @pl.when(nxt_e_ref[i] >= 0)
        def _():
            for cp in weight_copies(nxt_e_ref[i]):
                cp.start(priority=1)

    @pl.when(jnp.logical_and(i >= SCATTER_LAG, i - SCATTER_LAG < n_used))
    def _():
        wait_scatter(i - SCATTER_LAG)

    interior = jnp.logical_and(i >= 1, has_ahead)

    @pl.when(interior)
    def _():
        wait_gather(i)
        start_rows(i - 1, gather=False)
        start_rows(i + GATHER_AHEAD, gather=True)
        compute(i)

    @pl.when(jnp.logical_and(used, jnp.logical_not(interior)))
    def _():
        @pl.when(has_ahead)
        def _():
            start_rows(i + GATHER_AHEAD, gather=True)

        @pl.when(i >= 1)
        def _():
            start_rows(i - 1, gather=False)

        wait_gather(i)
        compute(i)

        @pl.when(i + 1 == n_used)
        def _():
            start_rows(i, gather=False)

    @pl.when(i == n_blk - 1)
    def _():
        for back in range(SCATTER_LAG - 1, -1, -1):
            @pl.when(jnp.logical_and(i >= back, i - back < n_used))
            def _():
                wait_scatter(i - back)

        y_buf[...] = jnp.zeros_like(y_buf)
        for half in range(2):
            for j in range(rows):
                first = (n_tokens * TOP_K + half * rows + j) * chunks
                pltpu.make_async_copy(y_buf.at[half, j], ys_hbm.at[pl.ds(first, chunks), :],
                                      sct_sem.at[half]).start()
            wait_scatter(half)


def _experts(h, codes, blk_e, nxt_e, n_used, w_gate, w_up, w_down):
    d, ff = w_gate.shape[1], w_gate.shape[2]
    x_chunks = d // (2 * LANES)
    t = h.shape[0] // x_chunks
    n_blk = blk_e.shape[0]
    any_spec = pl.BlockSpec(memory_space=pl.ANY)
    grid_spec = pltpu.PrefetchScalarGridSpec(
        num_scalar_prefetch=3,
        grid=(n_blk,),
        in_specs=[any_spec] * 5,
        out_specs=any_spec,
        scratch_shapes=[pltpu.SMEM((IDX_SLOTS, 2 * EXPERT_ROWS), jnp.int32),
                        pltpu.VMEM((X_SLOTS, EXPERT_ROWS, x_chunks, LANES), jnp.uint32),
                        pltpu.VMEM((Y_SLOTS, EXPERT_ROWS, x_chunks, LANES), jnp.uint32),
                        pltpu.VMEM((d, ff), F32), pltpu.VMEM((d, ff), F32), pltpu.VMEM((ff, d), F32),
                        pltpu.VMEM((d, ff), BF16), pltpu.VMEM((d, ff), BF16),
                        pltpu.VMEM((ff, d), BF16),
                        pltpu.SemaphoreType.DMA((IDX_SLOTS,)), pltpu.SemaphoreType.DMA((X_SLOTS,)),
                        pltpu.SemaphoreType.DMA((Y_SLOTS,)), pltpu.SemaphoreType.DMA((3,))])
    return pl.pallas_call(
        functools.partial(_experts_kernel, n_tokens=t),
        grid_spec=grid_spec,
        out_shape=jax.ShapeDtypeStruct(((t * TOP_K + 2 * EXPERT_ROWS) * x_chunks, LANES), jnp.uint32),
        compiler_params=pltpu.CompilerParams(
            dimension_semantics=("arbitrary",), vmem_limit_bytes=VMEM_LIMIT),
        name="routed_experts",
    )(blk_e, nxt_e, n_used, codes, h, w_gate, w_up, w_down)


def _shared_ln_kernel(h_ref, *refs, alpha, eps):
    ys_refs = refs[:TOP_K]
    tw_ref, wg_ref, wu_ref, wd_ref, g_ref, b_ref, o_ref, routed_ref = refs[TOP_K:]
    h = h_ref[...]
    tm, d = h.shape
    chunks = d // (2 * LANES)
    hb = h.astype(BF16)
    hid = _silu(jnp.dot(hb, wg_ref[...], preferred_element_type=F32)) * jnp.dot(
        hb, wu_ref[...], preferred_element_type=F32)
    moe = jnp.dot(hid.astype(BF16), wd_ref[...], preferred_element_type=F32)
    lo = hi = None
    for k in range(TOP_K):
        words = ys_refs[k][...]
        w_k = tw_ref[:, k:k + 1]
        lo_k = lax.bitcast_convert_type(words << 16, F32) * w_k
        hi_k = lax.bitcast_convert_type(words & jnp.uint32(0xFFFF0000), F32) * w_k
        lo = lo_k if lo is None else lo + lo_k
        hi = hi_k if hi is None else hi + hi_k
    routed_ref[0] = lo
    routed_ref[1] = hi
    moe = moe + jnp.concatenate(
        [routed_ref[half, pl.ds(c, tm, stride=chunks), :] for half in range(2)
         for c in range(chunks)], axis=1)
    o_ref[...] = _layer_norm(alpha * h + moe, g_ref[...], b_ref[...], eps)


def _shared_ln(h, ys, top_w, w_gate, w_up, w_down, g, b, *, tm, alpha, eps):
    t, d = h.shape
    ff = w_gate.shape[1]
    chunks = d // (2 * LANES)
    n_tiles = t // tm
    row = pl.BlockSpec((tm, d), lambda i: (i, 0))
    vec = pl.BlockSpec((1, d), lambda i: (0, 0))
    band = lambda k: pl.BlockSpec((tm * chunks, LANES), lambda i: (k * n_tiles + i, 0))
    return pl.pallas_call(
        functools.partial(_shared_ln_kernel, alpha=alpha, eps=eps),
        grid=(n_tiles,),
        in_specs=[row] + [band(k) for k in range(TOP_K)] + [
            pl.BlockSpec((tm * chunks, TOP_K), lambda i: (i, 0)),
            pl.BlockSpec((d, ff), lambda i: (0, 0)), pl.BlockSpec((d, ff), lambda i: (0, 0)),
            pl.BlockSpec((ff, d), lambda i: (0, 0)), vec, vec],
        out_specs=row,
        out_shape=jax.ShapeDtypeStruct((t, d), F32),
        scratch_shapes=[pltpu.VMEM((2, tm * chunks, LANES), F32)],
        compiler_params=pltpu.CompilerParams(
            dimension_semantics=("parallel",), vmem_limit_bytes=VMEM_LIMIT),
        name="shared_expert_layernorm",
    )(h, *([ys] * TOP_K), top_w, w_gate, w_up, w_down, g.reshape(1, d), b.reshape(1, d))


LN_EPS = 1e-5
ROPE_THETA = 10000.0


def _rope_tables(seq, dim):
    inv = 1.0 / (ROPE_THETA ** (jnp.arange(0, dim, 2, dtype=F32) / dim))
    ang = jnp.arange(seq, dtype=F32)[:, None] * inv[None, :]
    ang = jnp.concatenate([ang, ang], axis=-1)
    sign = jnp.where(jnp.arange(dim) < dim // 2, -1.0, 1.0).astype(F32)
    return jnp.cos(ang), jnp.sin(ang) * sign


def _pad_lora(x, d_decay, d_aaa, axis):
    assert d_decay <= LORA_PAD and d_aaa <= LORA_PAD
    decay, aaa, gate = jnp.split(x, [d_decay, d_decay + d_aaa], axis=axis)
    pad = lambda p: jnp.pad(p, [(0, LORA_PAD - p.shape[a]) if a == axis % x.ndim else (0, 0)
                                for a in range(x.ndim)])
    return jnp.concatenate([pad(decay), pad(aaa), gate], axis=axis)


def _dispatch(top_idx, n_tokens, src_stride, dst_stride):
    n_assign = n_tokens * TOP_K
    i32 = jnp.int32
    e_flat = top_idx.reshape(n_assign).astype(i32)
    _, order = lax.sort((e_flat, jnp.arange(n_assign, dtype=i32)), num_keys=1, is_stable=True)
    experts = jnp.arange(N_EXPERTS, dtype=i32)
    counts = jnp.sum((e_flat.reshape(-1, LANES, 1) == experts).astype(i32), axis=(0, 1))
    grp_start = jnp.cumsum(counts) - counts
    padded = (counts + EXPERT_ROWS - 1) // EXPERT_ROWS * EXPERT_ROWS
    pad_end = jnp.cumsum(padded)
    pad_start = pad_end - padded
    n_blk = -(-n_assign // EXPERT_ROWS) + N_EXPERTS
    n_used = (pad_end[-1] // EXPERT_ROWS).astype(i32)
    first_row = jnp.arange(n_blk, dtype=i32) * EXPERT_ROWS
    blk_e = jnp.minimum(jnp.sum((pad_end[None, :] <= first_row[:, None]).astype(i32), axis=1),
                        N_EXPERTS - 1)
    into = first_row - pad_start[blk_e]
    blk_s0 = jnp.clip(grp_start[blk_e] + into, 0, n_assign)
    blk_nv = jnp.clip(counts[blk_e] - into, 0, EXPERT_ROWS)
    later = jnp.where(counts > 0, experts, N_EXPERTS)
    nxt = lax.cummin(jnp.concatenate([later[1:], jnp.full((1,), N_EXPERTS, i32)]), reverse=True)
    nxt_e = jnp.where(nxt < N_EXPERTS, nxt, -1)[blk_e]
    row = jnp.arange(EXPERT_ROWS, dtype=i32)[None, :]
    src = jnp.minimum(blk_s0[:, None] + row, n_assign - 1)
    pad = n_assign + (jnp.arange(n_blk, dtype=i32)[:, None] % 2) * EXPERT_ROWS + row
    valid = row < blk_nv[:, None]
    ids = order[src]
    codes = jnp.concatenate([jnp.where(valid, ids % n_tokens, 0) * src_stride,
                             jnp.where(valid, ids, pad) * dst_stride], axis=1)
    return codes.astype(i32), blk_e, nxt_e.astype(i32), n_used.reshape(1)


def kernel(x, w_in, shift_mu, w_lora_up, w0, a_lora_up, a0, g_lora_up, k_k, k_a, r_k, ln_x_w, ln_x_b,
           w_o_rwkv, w_o_moba, w_out, ln1_g, ln1_b, w_router, router_bias, w_gate_e, w_up_e,
           w_down_e, w_gate_s, w_up_s, w_down_s, ln2_g, ln2_b):
    batch, seq, d = x.shape
    depth = w_in.shape[0]
    alpha = (2 * depth) ** 0.25
    t = batch * seq
    c = d
    shift_w = shift_mu.shape[1]
    moba_heads = d // MOBA_HEAD
    cos, sin_signed = _rope_tables(seq, MOBA_HEAD)

    h = x.reshape(t, d)
    for layer in range(depth):
        wl = w_in[layer].astype(BF16)
        hb = h.astype(BF16)
        d_decay, d_aaa = w_lora_up.shape[1], a_lora_up.shape[1]
        w_lora = _pad_lora(wl[:, 3 * c:shift_w], d_decay, d_aaa, axis=1)
        assert w_lora.shape[1] % LANES == 0
        u_rkv = _matmul(hb, wl[:, :3 * c], tm=1024, tn=1024)
        u_lora = _matmul(hb, w_lora, tm=1024, tn=w_lora.shape[1])
        u_qkv = _matmul(hb, wl[:, shift_w:shift_w + 3 * d], tm=1024, tn=1024)
        u_gate = _matmul(hb, wl[:, shift_w + 3 * d:], tm=1024, tn=1024)

        pad_rows = lambda p: jnp.pad(p, ((0, LORA_PAD - p.shape[0]), (0, 0)))
        o_rwkv = _rwkv_mix(u_rkv, u_lora, shift_mu[layer][:3 * c],
                           _pad_lora(shift_mu[layer][3 * c:], d_decay, d_aaa, axis=0),
                           pad_rows(w_lora_up[layer]), w0[layer], pad_rows(a_lora_up[layer]),
                           a0[layer], g_lora_up[layer], k_k[layer], k_a[layer], r_k[layer],
                           ln_x_w[layer], ln_x_b[layer], batch=batch, seq=seq)
        o_moba = _moba(u_qkv, cos, sin_signed, batch=batch, seq=seq, heads=moba_heads)

        mixed = _merge(o_rwkv, o_moba, w_o_rwkv[layer].astype(BF16), w_o_moba[layer].astype(BF16),
                       u_gate, tm=512, tn=1024)
        h, h_packed = _proj_ln(mixed, w_out[layer].astype(BF16), h, ln1_g[layer], ln1_b[layer],
                               tm=512, alpha=alpha, eps=LN_EPS)

        top_idx, top_w = _router(h, w_router[layer].T, router_bias[layer], tm=512)
        chunks = d // LANES
        codes, blk_e, nxt_e, n_used = _dispatch(top_idx, t, chunks // 2, chunks // 2)
        ys = _experts(h_packed, codes, blk_e, nxt_e, n_used,
                      w_gate_e[layer], w_up_e[layer], w_down_e[layer])
        h = _shared_ln(h, ys, jnp.repeat(top_w.T, chunks // 2, axis=0), w_gate_s[layer].astype(BF16),
                       w_up_s[layer].astype(BF16), w_down_s[layer].astype(BF16), ln2_g[layer],
                       ln2_b[layer], tm=128, alpha=alpha, eps=LN_EPS)
    return h.reshape(batch, seq, d)
```

```python
import functools

import jax
import jax.numpy as jnp
from jax import lax
from jax.experimental import pallas as pl
from jax.experimental.pallas import tpu as pltpu

F32 = jnp.float32
BF16 = jnp.bfloat16

LANES = 128
WKV_HEAD = 64
WKV_CHUNK = 64


def _dot(a, b):
    return jnp.dot(a.astype(BF16), b.astype(BF16), preferred_element_type=F32)


def _dot_nt(a, b):
    return lax.dot_general(a.astype(BF16), b.astype(BF16), (((1,), (1,)), ((), ())),
                           preferred_element_type=F32)


assert WKV_CHUNK == WKV_HEAD
def _each(f, *lists):
    return [f(*xs) for xs in zip(*lists)]


def _wkv_pairs(r, ld, k, v, a, b, m_state):
    c = WKV_CHUNK
    cat0 = lambda *xs: jnp.concatenate(xs, axis=0)
    cat1 = lambda *xs: jnp.concatenate(xs, axis=1)
    lane = lax.broadcasted_iota(jnp.int32, (c, LANES), 1)
    step = lax.broadcasted_iota(jnp.int32, (c, LANES), 0)
    head0 = lane < WKV_HEAD
    other = jnp.where(head0, lane, lane - WKV_HEAD)

    def stack(x):
        x = x.astype(BF16)
        zero = jnp.zeros_like(x)
        return cat0(jnp.where(head0, x, zero), jnp.where(head0, zero, x))

    def cumsum_rows(x):
        shift = 1
        while shift < c:
            x = x + jnp.where(step >= shift, pltpu.roll(x, shift, axis=0), 0.0)
            shift *= 2
        return x

    dot = lambda p, q: jnp.dot(p.astype(BF16), q, preferred_element_type=F32)

    cw = _each(cumsum_rows, ld)
    e_out = _each(lambda x: jnp.exp(-x), cw)
    a_t = _each(lambda x, w, l: x * jnp.exp(w - l), a, cw, ld)
    r_t = _each(lambda x, w: x * jnp.exp(w), r, cw)
    bk_st = _each(lambda p, q, e: cat0(stack(p * e), stack(q * e)), b, k, e_out)
    v_st = _each(stack, v)
    aa = _each(lambda p, q, s: _dot_nt(cat0(p, q), s), a_t, r_t, bk_st)

    strict = step > other
    incl = step >= other
    n_ab = _each(lambda x: jnp.where(strict, x[:c, :LANES], 0.0), aa)
    a_ak = _each(lambda x: jnp.where(strict, x[:c, LANES:], 0.0), aa)
    a_rb = _each(lambda x: jnp.where(incl, x[c:, :LANES], 0.0), aa)
    a_rk = _each(lambda x: jnp.where(incl, x[c:, LANES:], 0.0), aa)

    eye = jnp.where(step == other, 1.0, 0.0)
    t_inv = _each(lambda n: eye + n, n_ab)
    w1 = _each(dot, a_ak, v_st)
    n_pow = _each(lambda n: dot(n, stack(n)), n_ab)
    for _ in range(c.bit_length() - 3):
        pt = _each(lambda n, t: dot(n, cat1(stack(n), stack(t))), n_pow, t_inv)
        n_pow = _each(lambda x: x[:, :LANES], pt)
        t_inv = _each(lambda t, x: t + x[:, LANES:], t_inv, pt)
    t_inv = _each(lambda t, n: t + dot(n, stack(t)), t_inv, n_pow)

    tw = _each(lambda t, w, x: dot(t, cat1(stack(w), stack(x))), t_inv, w1, a_t)
    rhs = _each(lambda x, v_: cat0(cat1(stack(x[:, LANES:]), stack(x[:, :LANES])),
                                   cat1(jnp.zeros_like(v_), v_)), tw, v_st)
    out_p = _each(lambda p, q, s: dot(cat1(p, q), s), a_rb, a_rk, rhs)
    tail = _each(lambda w: jnp.exp(w[c - 1:c, :] - w), cw)
    bkp_st = _each(lambda p, q, e: cat0(stack(p * e), stack(q * e)), b, k, tail)
    upd = _each(lambda p, s: lax.dot_general(p, s, (((0,), (0,)), ((), ())),
                                             preferred_element_type=F32), bkp_st, rhs)
    upd_p = _each(lambda u: u[:WKV_HEAD] + u[WKV_HEAD:], upd)
    r_hat = _each(lambda x, o: x + o[:, :LANES], r_t, out_p)
    g = _each(lambda w, u: jnp.where(step == other, jnp.exp(w[c - 1:c, :]), 0.0) + u[:, :LANES],
              cw, upd_p)

    def apply_state(lhs, m):
        l_hi = lhs.astype(BF16)
        l_lo = (lhs - l_hi.astype(F32)).astype(BF16)
        m_hi = m.astype(BF16)
        m_lo = m - m_hi.astype(F32)
        both = jnp.dot(l_hi, cat1(stack(m_hi), stack(m_lo)), preferred_element_type=F32)
        return (both[:, :LANES] + both[:, LANES:]) + jnp.dot(l_lo, stack(m_hi),
                                                             preferred_element_type=F32)

    rm = _each(lambda p, q, m: apply_state(cat0(p, q), m), r_hat, g, m_state)
    y = _each(lambda x, o: x[:c] + o[:, LANES:], rm, out_p)
    m_new = _each(lambda x, u: x[c:] + u[:, LANES:], rm, upd_p)
    return y, m_new


RWKV_GN_EPS = 64e-5
LORA_PAD = LANES
RWKV_SEQS_PER_STEP = 1


def _rwkv_kernel(u_ref, ul_ref, mu_ref, mul_ref, wl_ref, w0_ref, al_ref, a0_ref, gl_ref, kk_ref,
                 ka_ref, rk_ref, lnw_ref, lnb_ref, o_ref, m_ref, prev_ref, prevl_ref):
    c = WKV_CHUNK
    n_seq, _, width = o_ref.shape

    @pl.when(pl.program_id(1) == 0)
    def _():
        m_ref[...] = jnp.zeros_like(m_ref)
        prev_ref[...] = jnp.zeros_like(prev_ref)
        prevl_ref[...] = jnp.zeros_like(prevl_ref)

    def shifted(u, prev, mu):
        first = lax.broadcasted_iota(jnp.int32, u.shape, 0) == 0
        before = jnp.where(first, prev[7:8, :], pltpu.roll(u, 1, axis=0))
        return u + (before - u) * mu

    lanes = [slice(p * LANES, (p + 1) * LANES) for p in range(width // LANES)]
    split = lambda x: [x[:, sl] for sl in lanes]
    row = lax.broadcasted_iota(jnp.int32, (LANES, LANES), 0)
    col = lax.broadcasted_iota(jnp.int32, (LANES, LANES), 1)
    same_head = jnp.where((row < WKV_HEAD) == (col < WKV_HEAD), 1.0, 0.0).astype(BF16)

    def head_sums(x):
        stacked = jnp.concatenate(split(x), axis=0).astype(BF16)
        s = jnp.dot(stacked, same_head, preferred_element_type=F32)
        return [s[p * c:(p + 1) * c] for p in range(len(lanes))]

    def prepare(s):
        u = u_ref[s]
        ul = ul_ref[s]
        z = shifted(u, prev_ref[s], mu_ref[...])
        zl = shifted(ul, prevl_ref[s], mul_ref[...])
        prev_ref[s] = u[c - 8:, :]
        prevl_ref[s] = ul[c - 8:, :]
        r = z[:, :width]
        k_raw = z[:, width:2 * width]
        v = z[:, 2 * width:]
        w = w0_ref[...] + _dot(jnp.tanh(zl[:, :LORA_PAD]), wl_ref[...])
        ld = -jnp.exp(-(jnp.maximum(-w, 0.0) + jnp.log(1.0 + jnp.exp(-jnp.abs(w)))) - 0.5)
        a = _sigmoid(a0_ref[...] + _dot(zl[:, LORA_PAD:2 * LORA_PAD], al_ref[...]))
        g = _dot(_sigmoid(zl[:, 2 * LORA_PAD:]), gl_ref[...])
        kk_raw = k_raw * kk_ref[...]
        k = k_raw * (1.0 + (a - 1.0) * ka_ref[...])
        norm2 = head_sums(kk_raw * kk_raw)
        kk = _each(lambda x, n2: x * lax.rsqrt(jnp.maximum(n2, 1e-24)), split(kk_raw), norm2)
        bonus = head_sums(r * k * rk_ref[...])
        return r, ld, k, v, kk, a, g, bonus

    def recur(s, r, ld, k, v, kk, a):
        m_state = [m_ref[s, p] for p in range(len(lanes))]
        y, m_new = _wkv_pairs(split(r), split(ld), split(k), split(v), _each(lambda x: -x, kk),
                              _each(lambda x, a_: x * a_, kk, split(a)), m_state)
        for p in range(len(lanes)):
            m_ref[s, p] = m_new[p]
        return jnp.concatenate(y, axis=1)

    def finish(s, y, v, g, bonus):
        inv_n = 1.0 / WKV_HEAD
        mean = jnp.concatenate(head_sums(y), axis=1) * inv_n
        yc = y - mean
        var = jnp.concatenate(head_sums(yc * yc), axis=1) * inv_n
        yn = yc * lax.rsqrt(var + RWKV_GN_EPS) * lnw_ref[...] + lnb_ref[...]
        out = (yn + jnp.concatenate(bonus, axis=1) * v) * g
        o_ref[s] = out.astype(o_ref.dtype)

    prepared = [prepare(s) for s in range(n_seq)]
    ys = [recur(s, *prepared[s][:6]) for s in range(n_seq)]
    for s in range(n_seq):
        _, _, _, v, _, _, g, bonus = prepared[s]
        finish(s, ys[s], v, g, bonus)


def _rwkv_mix(u_rkv, u_lora, mu_rkv, mu_lora, w_lora, w0, a_lora, a0, g_lora, k_k, k_a, r_k,
              ln_w, ln_b, *, batch, seq):
    t = u_rkv.shape[0]
    width = u_rkv.shape[1] // 3
    n_lora = u_lora.shape[1]
    per = RWKV_SEQS_PER_STEP
    assert t == batch * seq and seq % WKV_CHUNK == 0 and width % LANES == 0 and batch % per == 0
    n_chunks = seq // WKV_CHUNK
    rows = lambda n: pl.BlockSpec((per, WKV_CHUNK, n), lambda bi, ci: (bi, ci, 0))
    whole = lambda x: pl.BlockSpec(x.shape, lambda bi, ci: (0,) * x.ndim)
    vec = lambda x: x.reshape(1, -1)
    params = [vec(mu_rkv), vec(mu_lora), w_lora.astype(BF16), vec(w0), a_lora.astype(BF16), vec(a0),
              g_lora.astype(BF16), vec(k_k), vec(k_a), vec(r_k), vec(ln_w), vec(ln_b)]
    out = pl.pallas_call(
        _rwkv_kernel,
        grid=(batch // per, n_chunks),
        in_specs=[rows(3 * width), rows(n_lora)] + [whole(p) for p in params],
        out_specs=rows(width),
        out_shape=jax.ShapeDtypeStruct((batch, seq, width), BF16),
        scratch_shapes=[pltpu.VMEM((per, width // LANES, WKV_HEAD, LANES), F32),
                        pltpu.VMEM((per, 8, 3 * width), F32), pltpu.VMEM((per, 8, n_lora), F32)],
        compiler_params=pltpu.CompilerParams(
            dimension_semantics=("parallel", "arbitrary"), vmem_limit_bytes=VMEM_LIMIT),
        name="rwkv7_time_mix",
    )(u_rkv.reshape(batch, seq, 3 * width), u_lora.reshape(batch, seq, n_lora), *params)
    return out.reshape(t, width)


MOBA_BLOCK = 256
MOBA_TOPK = 3
MOBA_HEAD = 128


def _rope(t, cos, sin_signed):
    return t * cos + pltpu.roll(t, MOBA_HEAD // 2, axis=1) * sin_signed


MOBA_HEADS_PER_STEP = 2


def _moba_kernel(q_ref, k_ref, v_ref, cos_ref, sin_ref, o_ref, kr_ref, vb_ref, kmean_ref, *,
                 nb, scale):
    blk = MOBA_BLOCK
    heads = [slice(g * MOBA_HEAD, (g + 1) * MOBA_HEAD) for g in range(MOBA_HEADS_PER_STEP)]
    kmean_ref[...] = jnp.zeros_like(kmean_ref)
    for n in range(nb):
        rows = slice(n * blk, (n + 1) * blk)
        for g, hd in enumerate(heads):
            kr = _rope(k_ref[rows, hd], cos_ref[rows, :], sin_ref[rows, :])
            kr_ref[g, rows, :] = kr.astype(BF16)
            vb_ref[g, rows, :] = v_ref[rows, hd].astype(BF16)
            kmean_ref[g, n:n + 1, :] = jnp.mean(kr, axis=0, keepdims=True)

    lane = lax.broadcasted_iota(jnp.int32, (blk, LANES), 1)
    row = lax.broadcasted_iota(jnp.int32, (blk, blk), 0)
    col = lax.broadcasted_iota(jnp.int32, (blk, blk), 1)
    causal = row >= col

    def masked_scores(i, g, hd):
        rows = slice(i * blk, (i + 1) * blk)
        q = _rope(q_ref[rows, hd], cos_ref[rows, :], sin_ref[rows, :])
        s = _dot_nt(q, kr_ref[g, :(i + 1) * blk, :]) * scale
        if i > MOBA_TOPK:
            gate = lax.dot_general(q, kmean_ref[g], (((1,), (1,)), ((), ())),
                                   precision=lax.Precision.HIGHEST, preferred_element_type=F32)
            gate = jnp.where(lane < i, gate, -jnp.inf)
            rank = jnp.zeros((blk, LANES), jnp.int32)
            for m in range(i):
                g_m = gate[:, m:m + 1]
                rank = rank + jnp.where(
                    g_m > gate, 1, jnp.where(g_m == gate, jnp.where(lane > m, 1, 0), 0))
            chosen = jnp.where(rank < MOBA_TOPK, 1.0, 0.0)
            parts = [jnp.where(chosen[:, n:n + 1] > 0.5, s[:, n * blk:(n + 1) * blk], -jnp.inf)
                     for n in range(i)]
        else:
            parts = [s[:, n * blk:(n + 1) * blk] for n in range(i)]
        parts.append(jnp.where(causal, s[:, i * blk:], -jnp.inf))
        return jnp.concatenate(parts, axis=1) if i else parts[0]

    for i in range(nb):
        rows = slice(i * blk, (i + 1) * blk)
        scores = [masked_scores(i, g, hd) for g, hd in enumerate(heads)]
        for g, hd in enumerate(heads):
            s = scores[g]
            m_row = jnp.max(s, axis=1, keepdims=True)
            p = jnp.exp(s - m_row)
            l_row = jnp.sum(p, axis=1, keepdims=True)
            acc = jnp.dot(p.astype(BF16), vb_ref[g, :(i + 1) * blk, :], preferred_element_type=F32)
            o_ref[rows, hd] = (acc / l_row).astype(o_ref.dtype)


def _moba(u_qkv, cos, sin_signed, *, batch, seq, heads):
    t = u_qkv.shape[0]
    per = MOBA_HEADS_PER_STEP
    assert seq % MOBA_BLOCK == 0 and heads % per == 0
    nb = seq // MOBA_BLOCK
    assert nb <= LANES
    kernel = functools.partial(_moba_kernel, nb=nb, scale=MOBA_HEAD ** -0.5)
    groups = heads // per
    full = lambda off: pl.BlockSpec((seq, per * MOBA_HEAD), lambda bi, hi: (bi, off + hi))
    table = pl.BlockSpec((seq, MOBA_HEAD), lambda bi, hi: (0, 0))
    return pl.pallas_call(
        kernel,
        grid=(batch, groups),
        in_specs=[full(0), full(groups), full(2 * groups), table, table],
        out_specs=full(0),
        out_shape=jax.ShapeDtypeStruct((t, heads * MOBA_HEAD), BF16),
        scratch_shapes=[pltpu.VMEM((per, seq, MOBA_HEAD), BF16), pltpu.VMEM((per, seq, MOBA_HEAD), BF16),
                        pltpu.VMEM((per, LANES, MOBA_HEAD), F32)],
        compiler_params=pltpu.CompilerParams(dimension_semantics=("parallel", "parallel")),
        name="moba_attention",
    )(u_qkv, u_qkv, u_qkv, cos, sin_signed)


VMEM_LIMIT = 56 * 1024 * 1024


def _matmul_kernel(x_ref, w_ref, o_ref):
    o_ref[...] = jnp.dot(x_ref[...], w_ref[...], preferred_element_type=F32).astype(o_ref.dtype)


def _matmul(x, w, *, tm, tn, out_dtype=F32):
    m, k = x.shape
    n = w.shape[1]
    assert m % tm == 0 and n % tn == 0
    return pl.pallas_call(
        _matmul_kernel,
        grid=(m // tm, n // tn),
        in_specs=[pl.BlockSpec((tm, k), lambda i, j: (i, 0)),
                  pl.BlockSpec((k, tn), lambda i, j: (0, j))],
        out_specs=pl.BlockSpec((tm, tn), lambda i, j: (i, j)),
        out_shape=jax.ShapeDtypeStruct((m, n), out_dtype),
        compiler_params=pltpu.CompilerParams(
            dimension_semantics=("parallel", "parallel"), vmem_limit_bytes=VMEM_LIMIT),
        name="matmul",
    )(x, w)


def _sigmoid(x):
    return 1.0 / (1.0 + jnp.exp(-x))


def _merge_kernel(a_ref, b_ref, pa_ref, pb_ref, ga_ref, gb_ref, o_ref):
    ya = jnp.dot(a_ref[...], pa_ref[...], preferred_element_type=F32)
    yb = jnp.dot(b_ref[...], pb_ref[...], preferred_element_type=F32)
    o_ref[...] = (_sigmoid(ga_ref[...]) * ya + _sigmoid(gb_ref[...]) * yb).astype(o_ref.dtype)


def _merge(o_a, o_b, p_a, p_b, gates, *, tm, tn):
    m, k = o_a.shape
    n = p_a.shape[1]
    nj = n // tn
    row = pl.BlockSpec((tm, k), lambda i, j: (i, 0))
    wcol = pl.BlockSpec((k, tn), lambda i, j: (0, j))
    return pl.pallas_call(
        _merge_kernel,
        grid=(m // tm, nj),
        in_specs=[row, row, wcol, wcol,
                  pl.BlockSpec((tm, tn), lambda i, j: (i, j)),
                  pl.BlockSpec((tm, tn), lambda i, j: (i, nj + j))],
        out_specs=pl.BlockSpec((tm, tn), lambda i, j: (i, j)),
        out_shape=jax.ShapeDtypeStruct((m, n), BF16),
        compiler_params=pltpu.CompilerParams(
            dimension_semantics=("parallel", "parallel"), vmem_limit_bytes=VMEM_LIMIT),
        name="gated_merge",
    )(o_a, o_b, p_a, p_b, gates, gates)


def _layer_norm(x, g, b, eps):
    mu = jnp.mean(x, axis=-1, keepdims=True)
    xc = x - mu
    var = jnp.mean(xc * xc, axis=-1, keepdims=True)
    return xc * lax.rsqrt(var + eps) * g + b


def _pack_words(x):
    half = x.shape[1] // 2
    bits = lax.bitcast_convert_type(x.astype(BF16).astype(F32), jnp.uint32)
    words = (bits[:, :half] >> 16) | bits[:, half:]
    words = jnp.stack([words[:, c * LANES:(c + 1) * LANES] for c in range(half // LANES)], axis=0)
    return pltpu.einshape("cjl->jcl", words)


def _proj_ln_kernel(m_ref, w_ref, x_ref, g_ref, b_ref, o_ref, packed_ref, *, alpha, eps):
    y = jnp.dot(m_ref[...], w_ref[...], preferred_element_type=F32)
    out = _layer_norm(alpha * x_ref[...] + y, g_ref[...], b_ref[...], eps)
    o_ref[...] = out
    packed_ref[...] = _pack_words(out)


def _proj_ln(mixed, w, x, g, b, *, tm, alpha, eps):
    m, k = mixed.shape
    n = w.shape[1]
    chunks = n // (2 * LANES)
    vec = pl.BlockSpec((1, n), lambda i: (0, 0))
    h, packed = pl.pallas_call(
        functools.partial(_proj_ln_kernel, alpha=alpha, eps=eps),
        grid=(m // tm,),
        in_specs=[pl.BlockSpec((tm, k), lambda i: (i, 0)),
                  pl.BlockSpec((k, n), lambda i: (0, 0)),
                  pl.BlockSpec((tm, n), lambda i: (i, 0)), vec, vec],
        out_specs=[pl.BlockSpec((tm, n), lambda i: (i, 0)),
                   pl.BlockSpec((tm, chunks, LANES), lambda i: (i, 0, 0))],
        out_shape=[jax.ShapeDtypeStruct((m, n), F32),
                   jax.ShapeDtypeStruct((m, chunks, LANES), jnp.uint32)],
        compiler_params=pltpu.CompilerParams(
            dimension_semantics=("parallel",), vmem_limit_bytes=VMEM_LIMIT),
        name="out_proj_layernorm",
    )(mixed, w, x, g.reshape(1, n), b.reshape(1, n))
    return h, packed.reshape(m * chunks, LANES)


N_EXPERTS = 256
N_GROUPS = 8
TOPK_GROUPS = 4
TOP_K = 8
ROUTED_SCALE = 2.5


def _router_kernel(h_ref, wt_ref, bias_ref, idx_ref, wgt_ref):
    tm = h_ref.shape[0]
    per = N_EXPERTS // N_GROUPS
    logits = lax.dot_general(wt_ref[...], h_ref[...], (((1,), (1,)), ((), ())),
                             precision=lax.Precision.HIGHEST, preferred_element_type=F32)
    scores = _sigmoid(logits)
    biased = scores + bias_ref[...]
    neg = -jnp.inf

    grp = biased.reshape(N_GROUPS, per, tm)
    slot = lax.broadcasted_iota(jnp.int32, (N_GROUPS, per, tm), 1)
    top1 = jnp.max(grp, axis=1, keepdims=True)
    first = jnp.min(jnp.where(grp == top1, slot, per), axis=1, keepdims=True)
    top2 = jnp.max(jnp.where(slot == first, neg, grp), axis=1, keepdims=True)
    grp_score = (top1 + top2).reshape(N_GROUPS, tm)

    gid = lax.broadcasted_iota(jnp.int32, (N_GROUPS, tm), 0)
    rank = jnp.zeros((N_GROUPS, tm), jnp.int32)
    for m in range(N_GROUPS):
        s_m = grp_score[m:m + 1, :]
        rank = rank + jnp.where(s_m > grp_score, 1,
                                jnp.where(s_m == grp_score, jnp.where(gid > m, 1, 0), 0))
    keep = (rank < TOPK_GROUPS).reshape(N_GROUPS, 1, tm)
    masked = jnp.where(keep, grp, neg).reshape(N_EXPERTS, tm)

    eid = lax.broadcasted_iota(jnp.int32, (N_EXPERTS, tm), 0)
    idxs, wgts = [], []
    for _ in range(TOP_K):
        best = jnp.max(masked, axis=0, keepdims=True)
        pick = jnp.min(jnp.where(masked == best, eid, N_EXPERTS), axis=0, keepdims=True)
        hit = eid == pick
        idxs.append(pick)
        wgts.append(jnp.sum(jnp.where(hit, scores, 0.0), axis=0, keepdims=True))
        masked = jnp.where(hit, neg, masked)
    wgt = jnp.concatenate(wgts, axis=0)
    wgt = wgt / jnp.sum(wgt, axis=0, keepdims=True) * ROUTED_SCALE
    idx_ref[...] = jnp.concatenate(idxs, axis=0)
    wgt_ref[...] = wgt


def _router(h, w_router_t, bias, *, tm):
    t, d = h.shape
    out = pl.BlockSpec((TOP_K, tm), lambda i: (0, i))
    return pl.pallas_call(
        _router_kernel,
        grid=(t // tm,),
        in_specs=[pl.BlockSpec((tm, d), lambda i: (i, 0)),
                  pl.BlockSpec((N_EXPERTS, d), lambda i: (0, 0)),
                  pl.BlockSpec((N_EXPERTS, 1), lambda i: (0, 0))],
        out_specs=[out, out],
        out_shape=[jax.ShapeDtypeStruct((TOP_K, t), jnp.int32),
                   jax.ShapeDtypeStruct((TOP_K, t), F32)],
        compiler_params=pltpu.CompilerParams(
            dimension_semantics=("parallel",), vmem_limit_bytes=VMEM_LIMIT),
        name="moe_router",
    )(h, w_router_t, bias.reshape(N_EXPERTS, 1))


EXPERT_ROWS = 128


def _silu(x):
    return x * _sigmoid(x)


GATHER_AHEAD = 3
X_SLOTS = 4
Y_SLOTS = 4
SCATTER_LAG = 3
IDX_SLOTS = 8


def _experts_kernel(blk_e_ref, nxt_e_ref, n_used_ref,
                    codes_hbm, h_hbm, wg_hbm, wu_hbm, wd_hbm, ys_hbm,
                    idx_ref, x_buf, y_buf, wg_st, wu_st, wd_st, wg_bf, wu_bf, wd_bf,
                    idx_sem, gat_sem, sct_sem, w_sem, *, n_tokens):
    i = pl.program_id(0)
    n_blk = pl.num_programs(0)
    n_used = n_used_ref[0]
    used = i < n_used
    rows = EXPERT_ROWS
    x_chunks = x_buf.shape[2]
    chunks = y_buf.shape[2]
    assert chunks == x_chunks

    def idx_copy(k):
        slot = k & (IDX_SLOTS - 1)
        return pltpu.make_async_copy(codes_hbm.at[pl.ds(k, 1), :], idx_ref.at[pl.ds(slot, 1), :],
                                     idx_sem.at[slot])

    def start_rows(k, gather):
        window = k & (IDX_SLOTS - 1)
        for j in range(rows):
            if gather:
                buf = k & (X_SLOTS - 1)
                src = idx_ref[window, j]
                pltpu.make_async_copy(h_hbm.at[pl.ds(src, x_chunks), :], x_buf.at[buf, j],
                                      gat_sem.at[buf]).start()
            else:
                buf = k & (Y_SLOTS - 1)
                dst = idx_ref[window, rows + j]
                pltpu.make_async_copy(y_buf.at[buf, j], ys_hbm.at[pl.ds(dst, chunks), :],
                                      sct_sem.at[buf]).start()

    def wait_gather(k):
        buf = k & (X_SLOTS - 1)
        pltpu.make_async_copy(x_buf.at[buf], x_buf.at[buf], gat_sem.at[buf]).wait()

    def wait_scatter(k):
        buf = k & (Y_SLOTS - 1)
        pltpu.make_async_copy(y_buf.at[buf], y_buf.at[buf], sct_sem.at[buf]).wait()

    def weight_copies(e):
        return (pltpu.make_async_copy(wg_hbm.at[e], wg_st, w_sem.at[0]),
                pltpu.make_async_copy(wu_hbm.at[e], wu_st, w_sem.at[1]),
                pltpu.make_async_copy(wd_hbm.at[e], wd_st, w_sem.at[2]))

    def compute(k):
        u = pltpu.einshape("jcl->cjl", x_buf[k & (X_SLOTS - 1)])
        u = jnp.concatenate([u[c] for c in range(x_chunks)], axis=1)
        lo = lax.bitcast_convert_type(u << 16, F32)
        hi = lax.bitcast_convert_type(u & jnp.uint32(0xFFFF0000), F32)
        x = jnp.concatenate([lo, hi], axis=1).astype(BF16)
        hid = _silu(jnp.dot(x, wg_bf[...], preferred_element_type=F32)) * jnp.dot(
            x, wu_bf[...], preferred_element_type=F32)
        y = jnp.dot(hid.astype(BF16), wd_bf[...], preferred_element_type=F32)
        y_buf[k & (Y_SLOTS - 1)] = _pack_words(y)

    @pl.when(i == 0)
    def _():
        for cp in weight_copies(blk_e_ref[0]):
            cp.start(priority=1)
        for k in range(GATHER_AHEAD):
            @pl.when(k < n_used)
            def _():
                idx_copy(k).start()
                idx_copy(k).wait()
                start_rows(k, gather=True)

        @pl.when(GATHER_AHEAD < n_used)
        def _():
            idx_copy(GATHER_AHEAD).start()

    @pl.when(i + GATHER_AHEAD + 1 < n_used)
    def _():
        idx_copy(i + GATHER_AHEAD + 1).start()

    has_ahead = i + GATHER_AHEAD < n_used

    @pl.when(has_ahead)
    def _():
        idx_copy(i + GATHER_AHEAD).wait()

    prev_e = blk_e_ref[jnp.maximum(i - 1, 0)]
    fresh = jnp.logical_or(i == 0, blk_e_ref[i] != prev_e)

    @pl.when(jnp.logical_and(used, fresh))
    def _():
        for cp in weight_copies(blk_e_ref[i]):
            cp.wait()
        wg_bf[...] = wg_st[...].astype(BF16)
        wu_bf[...] = wu_st[...].astype(BF16)
        wd_bf[...] = wd_st[...].astype(BF16)

        @pl.when(nxt_e_ref[i] >= 0)
        def _():
            for cp in weight_copies(nxt_e_ref[i]):
                cp.start(priority=1)

    @pl.when(jnp.logical_and(i >= SCATTER_LAG, i - SCATTER_LAG < n_used))
    def _():
        wait_scatter(i - SCATTER_LAG)

    interior = jnp.logical_and(i >= 1, has_ahead)

    @pl.when(interior)
    def _():
        wait_gather(i)
        start_rows(i - 1, gather=False)
        start_rows(i + GATHER_AHEAD, gather=True)
        compute(i)

    @pl.when(jnp.logical_and(used, jnp.logical_not(interior)))
    def _():
        @pl.when(has_ahead)
        def _():
            start_rows(i + GATHER_AHEAD, gather=True)

        @pl.when(i >= 1)
        def _():
            start_rows(i - 1, gather=False)

        wait_gather(i)
        compute(i)

        @pl.when(i + 1 == n_used)
        def _():
            start_rows(i, gather=False)

    @pl.when(i == n_blk - 1)
    def _():
        for back in range(SCATTER_LAG - 1, -1, -1):
            @pl.when(jnp.logical_and(i >= back, i - back < n_used))
            def _():
                wait_scatter(i - back)

        y_buf[...] = jnp.zeros_like(y_buf)
        for half in range(2):
            for j in range(rows):
                first = (n_tokens * TOP_K + half * rows + j) * chunks
                pltpu.make_async_copy(y_buf.at[half, j], ys_hbm.at[pl.ds(first, chunks), :],
                                      sct_sem.at[half]).start()
            wait_scatter(half)


def _experts(h, codes, blk_e, nxt_e, n_used, w_gate, w_up, w_down):
    d, ff = w_gate.shape[1], w_gate.shape[2]
    x_chunks = d // (2 * LANES)
    t = h.shape[0] // x_chunks
    n_blk = blk_e.shape[0]
    any_spec = pl.BlockSpec(memory_space=pl.ANY)
    grid_spec = pltpu.PrefetchScalarGridSpec(
        num_scalar_prefetch=3,
        grid=(n_blk,),
        in_specs=[any_spec] * 5,
        out_specs=any_spec,
        scratch_shapes=[pltpu.SMEM((IDX_SLOTS, 2 * EXPERT_ROWS), jnp.int32),
                        pltpu.VMEM((X_SLOTS, EXPERT_ROWS, x_chunks, LANES), jnp.uint32),
                        pltpu.VMEM((Y_SLOTS, EXPERT_ROWS, x_chunks, LANES), jnp.uint32),
                        pltpu.VMEM((d, ff), F32), pltpu.VMEM((d, ff), F32), pltpu.VMEM((ff, d), F32),
                        pltpu.VMEM((d, ff), BF16), pltpu.VMEM((d, ff), BF16),
                        pltpu.VMEM((ff, d), BF16),
                        pltpu.SemaphoreType.DMA((IDX_SLOTS,)), pltpu.SemaphoreType.DMA((X_SLOTS,)),
                        pltpu.SemaphoreType.DMA((Y_SLOTS,)), pltpu.SemaphoreType.DMA((3,))])
    return pl.pallas_call(
        functools.partial(_experts_kernel, n_tokens=t),
        grid_spec=grid_spec,
        out_shape=jax.ShapeDtypeStruct(((t * TOP_K + 2 * EXPERT_ROWS) * x_chunks, LANES), jnp.uint32),
        compiler_params=pltpu.CompilerParams(
            dimension_semantics=("arbitrary",), vmem_limit_bytes=VMEM_LIMIT),
        name="routed_experts",
    )(blk_e, nxt_e, n_used, codes, h, w_gate, w_up, w_down)


def _shared_ln_kernel(h_ref, *refs, alpha, eps):
    ys_refs = refs[:TOP_K]
    tw_ref, wg_ref, wu_ref, wd_ref, g_ref, b_ref, o_ref, routed_ref = refs[TOP_K:]
    h = h_ref[...]
    tm, d = h.shape
    chunks = d // (2 * LANES)
    hb = h.astype(BF16)
    hid = _silu(jnp.dot(hb, wg_ref[...], preferred_element_type=F32)) * jnp.dot(
        hb, wu_ref[...], preferred_element_type=F32)
    moe = jnp.dot(hid.astype(BF16), wd_ref[...], preferred_element_type=F32)
    lo = hi = None
    for k in range(TOP_K):
        words = ys_refs[k][...]
        w_k = tw_ref[:, k:k + 1]
        lo_k = lax.bitcast_convert_type(words << 16, F32) * w_k
        hi_k = lax.bitcast_convert_type(words & jnp.uint32(0xFFFF0000), F32) * w_k
        lo = lo_k if lo is None else lo + lo_k
        hi = hi_k if hi is None else hi + hi_k
    routed_ref[0] = lo
    routed_ref[1] = hi
    moe = moe + jnp.concatenate(
        [routed_ref[half, pl.ds(c, tm, stride=chunks), :] for half in range(2)
         for c in range(chunks)], axis=1)
    o_ref[...] = _layer_norm(alpha * h + moe, g_ref[...], b_ref[...], eps)


def _shared_ln(h, ys, top_w, w_gate, w_up, w_down, g, b, *, tm, alpha, eps):
    t, d = h.shape
    ff = w_gate.shape[1]
    chunks = d // (2 * LANES)
    n_tiles = t // tm
    row = pl.BlockSpec((tm, d), lambda i: (i, 0))
    vec = pl.BlockSpec((1, d), lambda i: (0, 0))
    band = lambda k: pl.BlockSpec((tm * chunks, LANES), lambda i: (k * n_tiles + i, 0))
    return pl.pallas_call(
        functools.partial(_shared_ln_kernel, alpha=alpha, eps=eps),
        grid=(n_tiles,),
        in_specs=[row] + [band(k) for k in range(TOP_K)] + [
            pl.BlockSpec((tm * chunks, TOP_K), lambda i: (i, 0)),
            pl.BlockSpec((d, ff), lambda i: (0, 0)), pl.BlockSpec((d, ff), lambda i: (0, 0)),
            pl.BlockSpec((ff, d), lambda i: (0, 0)), vec, vec],
        out_specs=row,
        out_shape=jax.ShapeDtypeStruct((t, d), F32),
        scratch_shapes=[pltpu.VMEM((2, tm * chunks, LANES), F32)],
        compiler_params=pltpu.CompilerParams(
            dimension_semantics=("parallel",), vmem_limit_bytes=VMEM_LIMIT),
        name="shared_expert_layernorm",
    )(h, *([ys] * TOP_K), top_w, w_gate, w_up, w_down, g.reshape(1, d), b.reshape(1, d))


LN_EPS = 1e-5
ROPE_THETA = 10000.0


def _rope_tables(seq, dim):
    inv = 1.0 / (ROPE_THETA ** (jnp.arange(0, dim, 2, dtype=F32) / dim))
    ang = jnp.arange(seq, dtype=F32)[:, None] * inv[None, :]
    ang = jnp.concatenate([ang, ang], axis=-1)
    sign = jnp.where(jnp.arange(dim) < dim // 2, -1.0, 1.0).astype(F32)
    return jnp.cos(ang), jnp.sin(ang) * sign


def _pad_lora(x, d_decay, d_aaa, axis):
    assert d_decay <= LORA_PAD and d_aaa <= LORA_PAD
    decay, aaa, gate = jnp.split(x, [d_decay, d_decay + d_aaa], axis=axis)
    pad = lambda p: jnp.pad(p, [(0, LORA_PAD - p.shape[a]) if a == axis % x.ndim else (0, 0)
                                for a in range(x.ndim)])
    return jnp.concatenate([pad(decay), pad(aaa), gate], axis=axis)


def _dispatch(top_idx, n_tokens, src_stride, dst_stride):
    n_assign = n_tokens * TOP_K
    i32 = jnp.int32
    e_flat = top_idx.reshape(n_assign).astype(i32)
    _, order = lax.sort((e_flat, jnp.arange(n_assign, dtype=i32)), num_keys=1, is_stable=True)
    experts = jnp.arange(N_EXPERTS, dtype=i32)
    counts = jnp.sum((e_flat.reshape(-1, LANES, 1) == experts).astype(i32), axis=(0, 1))
    grp_start = jnp.cumsum(counts) - counts
    padded = (counts + EXPERT_ROWS - 1) // EXPERT_ROWS * EXPERT_ROWS
    pad_end = jnp.cumsum(padded)
    pad_start = pad_end - padded
    n_blk = -(-n_assign // EXPERT_ROWS) + N_EXPERTS
    n_used = (pad_end[-1] // EXPERT_ROWS).astype(i32)
    first_row = jnp.arange(n_blk, dtype=i32) * EXPERT_ROWS
    blk_e = jnp.minimum(jnp.sum((pad_end[None, :] <= first_row[:, None]).astype(i32), axis=1),
                        N_EXPERTS - 1)
    into = first_row - pad_start[blk_e]
    blk_s0 = jnp.clip(grp_start[blk_e] + into, 0, n_assign)
    blk_nv = jnp.clip(counts[blk_e] - into, 0, EXPERT_ROWS)
    later = jnp.where(counts > 0, experts, N_EXPERTS)
    nxt = lax.cummin(jnp.concatenate([later[1:], jnp.full((1,), N_EXPERTS, i32)]), reverse=True)
    nxt_e = jnp.where(nxt < N_EXPERTS, nxt, -1)[blk_e]
    row = jnp.arange(EXPERT_ROWS, dtype=i32)[None, :]
    src = jnp.minimum(blk_s0[:, None] + row, n_assign - 1)
    pad = n_assign + (jnp.arange(n_blk, dtype=i32)[:, None] % 2) * EXPERT_ROWS + row
    valid = row < blk_nv[:, None]
    ids = order[src]
    codes = jnp.concatenate([jnp.where(valid, ids % n_tokens, 0) * src_stride,
                             jnp.where(valid, ids, pad) * dst_stride], axis=1)
    return codes.astype(i32), blk_e, nxt_e.astype(i32), n_used.reshape(1)


def kernel(x, w_in, shift_mu, w_lora_up, w0, a_lora_up, a0, g_lora_up, k_k, k_a, r_k, ln_x_w, ln_x_b,
           w_o_rwkv, w_o_moba, w_out, ln1_g, ln1_b, w_router, router_bias, w_gate_e, w_up_e,
           w_down_e, w_gate_s, w_up_s, w_down_s, ln2_g, ln2_b):
    batch, seq, d = x.shape
    depth = w_in.shape[0]
    alpha = (2 * depth) ** 0.25
    t = batch * seq
    c = d
    shift_w = shift_mu.shape[1]
    moba_heads = d // MOBA_HEAD
    cos, sin_signed = _rope_tables(seq, MOBA_HEAD)

    h = x.reshape(t, d)
    for layer in range(depth):
        wl = w_in[layer].astype(BF16)
        hb = h.astype(BF16)
        d_decay, d_aaa = w_lora_up.shape[1], a_lora_up.shape[1]
        w_lora = _pad_lora(wl[:, 3 * c:shift_w], d_decay, d_aaa, axis=1)
        assert w_lora.shape[1] % LANES == 0
        u_rkv = _matmul(hb, wl[:, :3 * c], tm=1024, tn=1024)
        u_lora = _matmul(hb, w_lora, tm=1024, tn=w_lora.shape[1])
        u_qkv = _matmul(hb, wl[:, shift_w:shift_w + 3 * d], tm=1024, tn=1024)
        u_gate = _matmul(hb, wl[:, shift_w + 3 * d:], tm=1024, tn=1024)

        pad_rows = lambda p: jnp.pad(p, ((0, LORA_PAD - p.shape[0]), (0, 0)))
        o_rwkv = _rwkv_mix(u_rkv, u_lora, shift_mu[layer][:3 * c],
                           _pad_lora(shift_mu[layer][3 * c:], d_decay, d_aaa, axis=0),
                           pad_rows(w_lora_up[layer]), w0[layer], pad_rows(a_lora_up[layer]),
                           a0[layer], g_lora_up[layer], k_k[layer], k_a[layer], r_k[layer],
                           ln_x_w[layer], ln_x_b[layer], batch=batch, seq=seq)
        o_moba = _moba(u_qkv, cos, sin_signed, batch=batch, seq=seq, heads=moba_heads)

        mixed = _merge(o_rwkv, o_moba, w_o_rwkv[layer].astype(BF16), w_o_moba[layer].astype(BF16),
                       u_gate, tm=512, tn=1024)
        h, h_packed = _proj_ln(mixed, w_out[layer].astype(BF16), h, ln1_g[layer], ln1_b[layer],
                               tm=512, alpha=alpha, eps=LN_EPS)

        top_idx, top_w = _router(h, w_router[layer].T, router_bias[layer], tm=512)
        chunks = d // LANES
        codes, blk_e, nxt_e, n_used = _dispatch(top_idx, t, chunks // 2, chunks // 2)
        ys = _experts(h_packed, codes, blk_e, nxt_e, n_used,
                      w_gate_e[layer], w_up_e[layer], w_down_e[layer])
        h = _shared_ln(h, ys, jnp.repeat(top_w.T, chunks // 2, axis=0), w_gate_s[layer].astype(BF16),
                       w_up_s[layer].astype(BF16), w_down_s[layer].astype(BF16), ln2_g[layer],
                       ln2_b[layer], tm=256, alpha=alpha, eps=LN_EPS)
    return h.reshape(batch, seq, d)
```

```python
import functools

import jax
import jax.numpy as jnp
from jax import lax
from jax.experimental import pallas as pl
from jax.experimental.pallas import tpu as pltpu

F32 = jnp.float32
BF16 = jnp.bfloat16

LANES = 128
WKV_HEAD = 64
WKV_CHUNK = 64


def _dot(a, b):
    return jnp.dot(a.astype(BF16), b.astype(BF16), preferred_element_type=F32)


def _dot_nt(a, b):
    return lax.dot_general(a.astype(BF16), b.astype(BF16), (((1,), (1,)), ((), ())),
                           preferred_element_type=F32)


assert WKV_CHUNK == WKV_HEAD
def _each(f, *lists):
    return [f(*xs) for xs in zip(*lists)]


def _wkv_pairs(r, ld, k, v, a, b, m_state):
    c = WKV_CHUNK
    cat0 = lambda *xs: jnp.concatenate(xs, axis=0)
    cat1 = lambda *xs: jnp.concatenate(xs, axis=1)
    lane = lax.broadcasted_iota(jnp.int32, (c, LANES), 1)
    step = lax.broadcasted_iota(jnp.int32, (c, LANES), 0)
    head0 = lane < WKV_HEAD
    other = jnp.where(head0, lane, lane - WKV_HEAD)

    def stack(x):
        x = x.astype(BF16)
        zero = jnp.zeros_like(x)
        return cat0(jnp.where(head0, x, zero), jnp.where(head0, zero, x))

    def cumsum_rows(x):
        shift = 1
        while shift < c:
            x = x + jnp.where(step >= shift, pltpu.roll(x, shift, axis=0), 0.0)
            shift *= 2
        return x

    dot = lambda p, q: jnp.dot(p.astype(BF16), q, preferred_element_type=F32)

    cw = _each(cumsum_rows, ld)
    e_out = _each(lambda x: jnp.exp(-x), cw)
    a_t = _each(lambda x, w, l: x * jnp.exp(w - l), a, cw, ld)
    r_t = _each(lambda x, w: x * jnp.exp(w), r, cw)
    bk_st = _each(lambda p, q, e: cat0(stack(p * e), stack(q * e)), b, k, e_out)
    v_st = _each(stack, v)
    aa = _each(lambda p, q, s: _dot_nt(cat0(p, q), s), a_t, r_t, bk_st)

    strict = step > other
    incl = step >= other
    n_ab = _each(lambda x: jnp.where(strict, x[:c, :LANES], 0.0), aa)
    a_ak = _each(lambda x: jnp.where(strict, x[:c, LANES:], 0.0), aa)
    a_rb = _each(lambda x: jnp.where(incl, x[c:, :LANES], 0.0), aa)
    a_rk = _each(lambda x: jnp.where(incl, x[c:, LANES:], 0.0), aa)

    eye = jnp.where(step == other, 1.0, 0.0)
    t_inv = _each(lambda n: eye + n, n_ab)
    w1 = _each(dot, a_ak, v_st)
    n_pow = _each(lambda n: dot(n, stack(n)), n_ab)
    for _ in range(c.bit_length() - 3):
        pt = _each(lambda n, t: dot(n, cat1(stack(n), stack(t))), n_pow, t_inv)
        n_pow = _each(lambda x: x[:, :LANES], pt)
        t_inv = _each(lambda t, x: t + x[:, LANES:], t_inv, pt)
    t_inv = _each(lambda t, n: t + dot(n, stack(t)), t_inv, n_pow)

    tw = _each(lambda t, w, x: dot(t, cat1(stack(w), stack(x))), t_inv, w1, a_t)
    rhs = _each(lambda x, v_: cat0(cat1(stack(x[:, LANES:]), stack(x[:, :LANES])),
                                   cat1(jnp.zeros_like(v_), v_)), tw, v_st)
    out_p = _each(lambda p, q, s: dot(cat1(p, q), s), a_rb, a_rk, rhs)
    tail = _each(lambda w: jnp.exp(w[c - 1:c, :] - w), cw)
    bkp_st = _each(lambda p, q, e: cat0(stack(p * e), stack(q * e)), b, k, tail)
    upd = _each(lambda p, s: lax.dot_general(p, s, (((0,), (0,)), ((), ())),
                                             preferred_element_type=F32), bkp_st, rhs)
    upd_p = _each(lambda u: u[:WKV_HEAD] + u[WKV_HEAD:], upd)
    r_hat = _each(lambda x, o: x + o[:, :LANES], r_t, out_p)
    g = _each(lambda w, u: jnp.where(step == other, jnp.exp(w[c - 1:c, :]), 0.0) + u[:, :LANES],
              cw, upd_p)

    def apply_state(lhs, m):
        l_hi = lhs.astype(BF16)
        l_lo = (lhs - l_hi.astype(F32)).astype(BF16)
        m_hi = m.astype(BF16)
        m_lo = m - m_hi.astype(F32)
        both = jnp.dot(l_hi, cat1(stack(m_hi), stack(m_lo)), preferred_element_type=F32)
        return (both[:, :LANES] + both[:, LANES:]) + jnp.dot(l_lo, stack(m_hi),
                                                             preferred_element_type=F32)

    rm = _each(lambda p, q, m: apply_state(cat0(p, q), m), r_hat, g, m_state)
    y = _each(lambda x, o: x[:c] + o[:, LANES:], rm, out_p)
    m_new = _each(lambda x, u: x[c:] + u[:, LANES:], rm, upd_p)
    return y, m_new


RWKV_GN_EPS = 64e-5
LORA_PAD = LANES
RWKV_SEQS_PER_STEP = 1


def _rwkv_kernel(u_ref, ul_ref, mu_ref, mul_ref, wl_ref, w0_ref, al_ref, a0_ref, gl_ref, kk_ref,
                 ka_ref, rk_ref, lnw_ref, lnb_ref, o_ref, m_ref, prev_ref, prevl_ref):
    c = WKV_CHUNK
    n_seq, _, width = o_ref.shape

    @pl.when(pl.program_id(1) == 0)
    def _():
        m_ref[...] = jnp.zeros_like(m_ref)
        prev_ref[...] = jnp.zeros_like(prev_ref)
        prevl_ref[...] = jnp.zeros_like(prevl_ref)

    def shifted(u, prev, mu):
        first = lax.broadcasted_iota(jnp.int32, u.shape, 0) == 0
        before = jnp.where(first, prev[7:8, :], pltpu.roll(u, 1, axis=0))
        return u + (before - u) * mu

    lanes = [slice(p * LANES, (p + 1) * LANES) for p in range(width // LANES)]
    split = lambda x: [x[:, sl] for sl in lanes]
    row = lax.broadcasted_iota(jnp.int32, (LANES, LANES), 0)
    col = lax.broadcasted_iota(jnp.int32, (LANES, LANES), 1)
    same_head = jnp.where((row < WKV_HEAD) == (col < WKV_HEAD), 1.0, 0.0).astype(BF16)

    def head_sums(x):
        stacked = jnp.concatenate(split(x), axis=0).astype(BF16)
        s = jnp.dot(stacked, same_head, preferred_element_type=F32)
        return [s[p * c:(p + 1) * c] for p in range(len(lanes))]

    def prepare(s):
        u = u_ref[s]
        ul = ul_ref[s]
        z = shifted(u, prev_ref[s], mu_ref[...])
        zl = shifted(ul, prevl_ref[s], mul_ref[...])
        prev_ref[s] = u[c - 8:, :]
        prevl_ref[s] = ul[c - 8:, :]
        r = z[:, :width]
        k_raw = z[:, width:2 * width]
        v = z[:, 2 * width:]
        w = w0_ref[...] + _dot(jnp.tanh(zl[:, :LORA_PAD]), wl_ref[...])
        ld = -jnp.exp(-(jnp.maximum(-w, 0.0) + jnp.log(1.0 + jnp.exp(-jnp.abs(w)))) - 0.5)
        a = _sigmoid(a0_ref[...] + _dot(zl[:, LORA_PAD:2 * LORA_PAD], al_ref[...]))
        g = _dot(_sigmoid(zl[:, 2 * LORA_PAD:]), gl_ref[...])
        kk_raw = k_raw * kk_ref[...]
        k = k_raw * (1.0 + (a - 1.0) * ka_ref[...])
        norm2 = head_sums(kk_raw * kk_raw)
        kk = _each(lambda x, n2: x * lax.rsqrt(jnp.maximum(n2, 1e-24)), split(kk_raw), norm2)
        bonus = head_sums(r * k * rk_ref[...])
        return r, ld, k, v, kk, a, g, bonus

    def recur(s, r, ld, k, v, kk, a):
        m_state = [m_ref[s, p] for p in range(len(lanes))]
        y, m_new = _wkv_pairs(split(r), split(ld), split(k), split(v), _each(lambda x: -x, kk),
                              _each(lambda x, a_: x * a_, kk, split(a)), m_state)
        for p in range(len(lanes)):
            m_ref[s, p] = m_new[p]
        return jnp.concatenate(y, axis=1)

    def finish(s, y, v, g, bonus):
        inv_n = 1.0 / WKV_HEAD
        mean = jnp.concatenate(head_sums(y), axis=1) * inv_n
        yc = y - mean
        var = jnp.concatenate(head_sums(yc * yc), axis=1) * inv_n
        yn = yc * lax.rsqrt(var + RWKV_GN_EPS) * lnw_ref[...] + lnb_ref[...]
        out = (yn + jnp.concatenate(bonus, axis=1) * v) * g
        o_ref[s] = out.astype(o_ref.dtype)

    prepared = [prepare(s) for s in range(n_seq)]
    ys = [recur(s, *prepared[s][:6]) for s in range(n_seq)]
    for s in range(n_seq):
        _, _, _, v, _, _, g, bonus = prepared[s]
        finish(s, ys[s], v, g, bonus)


def _rwkv_mix(u_rkv, u_lora, mu_rkv, mu_lora, w_lora, w0, a_lora, a0, g_lora, k_k, k_a, r_k,
              ln_w, ln_b, *, batch, seq):
    t = u_rkv.shape[0]
    width = u_rkv.shape[1] // 3
    n_lora = u_lora.shape[1]
    per = RWKV_SEQS_PER_STEP
    assert t == batch * seq and seq % WKV_CHUNK == 0 and width % LANES == 0 and batch % per == 0
    n_chunks = seq // WKV_CHUNK
    rows = lambda n: pl.BlockSpec((per, WKV_CHUNK, n), lambda bi, ci: (bi, ci, 0))
    whole = lambda x: pl.BlockSpec(x.shape, lambda bi, ci: (0,) * x.ndim)
    vec = lambda x: x.reshape(1, -1)
    params = [vec(mu_rkv), vec(mu_lora), w_lora.astype(BF16), vec(w0), a_lora.astype(BF16), vec(a0),
              g_lora.astype(BF16), vec(k_k), vec(k_a), vec(r_k), vec(ln_w), vec(ln_b)]
    out = pl.pallas_call(
        _rwkv_kernel,
        grid=(batch // per, n_chunks),
        in_specs=[rows(3 * width), rows(n_lora)] + [whole(p) for p in params],
        out_specs=rows(width),
        out_shape=jax.ShapeDtypeStruct((batch, seq, width), BF16),
        scratch_shapes=[pltpu.VMEM((per, width // LANES, WKV_HEAD, LANES), F32),
                        pltpu.VMEM((per, 8, 3 * width), F32), pltpu.VMEM((per, 8, n_lora), F32)],
        compiler_params=pltpu.CompilerParams(
            dimension_semantics=("parallel", "arbitrary"), vmem_limit_bytes=VMEM_LIMIT),
        name="rwkv7_time_mix",
    )(u_rkv.reshape(batch, seq, 3 * width), u_lora.reshape(batch, seq, n_lora), *params)
    return out.reshape(t, width)


MOBA_BLOCK = 256
MOBA_TOPK = 3
MOBA_HEAD = 128


def _rope(t, cos, sin_signed):
    return t * cos + pltpu.roll(t, MOBA_HEAD // 2, axis=1) * sin_signed


MOBA_HEADS_PER_STEP = 2


def _moba_kernel(q_ref, k_ref, v_ref, cos_ref, sin_ref, o_ref, kr_ref, vb_ref, kmean_ref, *,
                 nb, scale):
    blk = MOBA_BLOCK
    heads = [slice(g * MOBA_HEAD, (g + 1) * MOBA_HEAD) for g in range(MOBA_HEADS_PER_STEP)]
    kmean_ref[...] = jnp.zeros_like(kmean_ref)
    for n in range(nb):
        rows = slice(n * blk, (n + 1) * blk)
        for g, hd in enumerate(heads):
            kr = _rope(k_ref[rows, hd], cos_ref[rows, :], sin_ref[rows, :])
            kr_ref[g, rows, :] = kr.astype(BF16)
            vb_ref[g, rows, :] = v_ref[rows, hd].astype(BF16)
            kmean_ref[g, n:n + 1, :] = jnp.mean(kr, axis=0, keepdims=True)

    lane = lax.broadcasted_iota(jnp.int32, (blk, LANES), 1)
    row = lax.broadcasted_iota(jnp.int32, (blk, blk), 0)
    col = lax.broadcasted_iota(jnp.int32, (blk, blk), 1)
    causal = row >= col

    def masked_scores(i, g, hd):
        rows = slice(i * blk, (i + 1) * blk)
        q = _rope(q_ref[rows, hd], cos_ref[rows, :], sin_ref[rows, :])
        s = _dot_nt(q, kr_ref[g, :(i + 1) * blk, :]) * scale
        if i > MOBA_TOPK:
            gate = lax.dot_general(q, kmean_ref[g], (((1,), (1,)), ((), ())),
                                   precision=lax.Precision.HIGHEST, preferred_element_type=F32)
            gate = jnp.where(lane < i, gate, -jnp.inf)
            rank = jnp.zeros((blk, LANES), jnp.int32)
            for m in range(i):
                g_m = gate[:, m:m + 1]
                rank = rank + jnp.where(
                    g_m > gate, 1, jnp.where(g_m == gate, jnp.where(lane > m, 1, 0), 0))
            chosen = jnp.where(rank < MOBA_TOPK, 1.0, 0.0)
            parts = [jnp.where(chosen[:, n:n + 1] > 0.5, s[:, n * blk:(n + 1) * blk], -jnp.inf)
                     for n in range(i)]
        else:
            parts = [s[:, n * blk:(n + 1) * blk] for n in range(i)]
        parts.append(jnp.where(causal, s[:, i * blk:], -jnp.inf))
        return jnp.concatenate(parts, axis=1) if i else parts[0]

    for i in range(nb):
        rows = slice(i * blk, (i + 1) * blk)
        scores = [masked_scores(i, g, hd) for g, hd in enumerate(heads)]
        for g, hd in enumerate(heads):
            s = scores[g]
            m_row = jnp.max(s, axis=1, keepdims=True)
            p = jnp.exp(s - m_row)
            l_row = jnp.sum(p, axis=1, keepdims=True)
            acc = jnp.dot(p.astype(BF16), vb_ref[g, :(i + 1) * blk, :], preferred_element_type=F32)
            o_ref[rows, hd] = (acc / l_row).astype(o_ref.dtype)


def _moba(u_qkv, cos, sin_signed, *, batch, seq, heads):
    t = u_qkv.shape[0]
    per = MOBA_HEADS_PER_STEP
    assert seq % MOBA_BLOCK == 0 and heads % per == 0
    nb = seq // MOBA_BLOCK
    assert nb <= LANES
    kernel = functools.partial(_moba_kernel, nb=nb, scale=MOBA_HEAD ** -0.5)
    groups = heads // per
    full = lambda off: pl.BlockSpec((seq, per * MOBA_HEAD), lambda bi, hi: (bi, off + hi))
    table = pl.BlockSpec((seq, MOBA_HEAD), lambda bi, hi: (0, 0))
    return pl.pallas_call(
        kernel,
        grid=(batch, groups),
        in_specs=[full(0), full(groups), full(2 * groups), table, table],
        out_specs=full(0),
        out_shape=jax.ShapeDtypeStruct((t, heads * MOBA_HEAD), BF16),
        scratch_shapes=[pltpu.VMEM((per, seq, MOBA_HEAD), BF16), pltpu.VMEM((per, seq, MOBA_HEAD), BF16),
                        pltpu.VMEM((per, LANES, MOBA_HEAD), F32)],
        compiler_params=pltpu.CompilerParams(dimension_semantics=("parallel", "parallel")),
        name="moba_attention",
    )(u_qkv, u_qkv, u_qkv, cos, sin_signed)


VMEM_LIMIT = 56 * 1024 * 1024


def _matmul_kernel(x_ref, w_ref, o_ref):
    o_ref[...] = jnp.dot(x_ref[...], w_ref[...], preferred_element_type=F32).astype(o_ref.dtype)


def _matmul(x, w, *, tm, tn, out_dtype=F32):
    m, k = x.shape
    n = w.shape[1]
    assert m % tm == 0 and n % tn == 0
    return pl.pallas_call(
        _matmul_kernel,
        grid=(m // tm, n // tn),
        in_specs=[pl.BlockSpec((tm, k), lambda i, j: (i, 0)),
                  pl.BlockSpec((k, tn), lambda i, j: (0, j))],
        out_specs=pl.BlockSpec((tm, tn), lambda i, j: (i, j)),
        out_shape=jax.ShapeDtypeStruct((m, n), out_dtype),
        compiler_params=pltpu.CompilerParams(
            dimension_semantics=("parallel", "parallel"), vmem_limit_bytes=VMEM_LIMIT),
        name="matmul",
    )(x, w)


def _sigmoid(x):
    return 1.0 / (1.0 + jnp.exp(-x))


def _merge_kernel(a_ref, b_ref, pa_ref, pb_ref, ga_ref, gb_ref, o_ref):
    ya = jnp.dot(a_ref[...], pa_ref[...], preferred_element_type=F32)
    yb = jnp.dot(b_ref[...], pb_ref[...], preferred_element_type=F32)
    o_ref[...] = (_sigmoid(ga_ref[...]) * ya + _sigmoid(gb_ref[...]) * yb).astype(o_ref.dtype)


def _merge(o_a, o_b, p_a, p_b, gates, *, tm, tn):
    m, k = o_a.shape
    n = p_a.shape[1]
    nj = n // tn
    row = pl.BlockSpec((tm, k), lambda j, i: (i, 0))
    wcol = pl.BlockSpec((k, tn), lambda j, i: (0, j))
    return pl.pallas_call(
        _merge_kernel,
        grid=(nj, m // tm),
        in_specs=[row, row, wcol, wcol,
                  pl.BlockSpec((tm, tn), lambda j, i: (i, j)),
                  pl.BlockSpec((tm, tn), lambda j, i: (i, nj + j))],
        out_specs=pl.BlockSpec((tm, tn), lambda j, i: (i, j)),
        out_shape=jax.ShapeDtypeStruct((m, n), BF16),
        compiler_params=pltpu.CompilerParams(
            dimension_semantics=("parallel", "parallel"), vmem_limit_bytes=VMEM_LIMIT),
        name="gated_merge",
    )(o_a, o_b, p_a, p_b, gates, gates)


def _layer_norm(x, g, b, eps):
    mu = jnp.mean(x, axis=-1, keepdims=True)
    xc = x - mu
    var = jnp.mean(xc * xc, axis=-1, keepdims=True)
    return xc * lax.rsqrt(var + eps) * g + b


def _pack_words(x):
    half = x.shape[1] // 2
    bits = lax.bitcast_convert_type(x.astype(BF16).astype(F32), jnp.uint32)
    words = (bits[:, :half] >> 16) | bits[:, half:]
    words = jnp.stack([words[:, c * LANES:(c + 1) * LANES] for c in range(half // LANES)], axis=0)
    return pltpu.einshape("cjl->jcl", words)


def _proj_ln_kernel(m_ref, w_ref, x_ref, g_ref, b_ref, o_ref, packed_ref, *, alpha, eps):
    y = jnp.dot(m_ref[...], w_ref[...], preferred_element_type=F32)
    out = _layer_norm(alpha * x_ref[...] + y, g_ref[...], b_ref[...], eps)
    o_ref[...] = out
    packed_ref[...] = _pack_words(out)


def _proj_ln(mixed, w, x, g, b, *, tm, alpha, eps):
    m, k = mixed.shape
    n = w.shape[1]
    chunks = n // (2 * LANES)
    vec = pl.BlockSpec((1, n), lambda i: (0, 0))
    h, packed = pl.pallas_call(
        functools.partial(_proj_ln_kernel, alpha=alpha, eps=eps),
        grid=(m // tm,),
        in_specs=[pl.BlockSpec((tm, k), lambda i: (i, 0)),
                  pl.BlockSpec((k, n), lambda i: (0, 0)),
                  pl.BlockSpec((tm, n), lambda i: (i, 0)), vec, vec],
        out_specs=[pl.BlockSpec((tm, n), lambda i: (i, 0)),
                   pl.BlockSpec((tm, chunks, LANES), lambda i: (i, 0, 0))],
        out_shape=[jax.ShapeDtypeStruct((m, n), F32),
                   jax.ShapeDtypeStruct((m, chunks, LANES), jnp.uint32)],
        compiler_params=pltpu.CompilerParams(
            dimension_semantics=("parallel",), vmem_limit_bytes=VMEM_LIMIT),
        name="out_proj_layernorm",
    )(mixed, w, x, g.reshape(1, n), b.reshape(1, n))
    return h, packed.reshape(m * chunks, LANES)


N_EXPERTS = 256
N_GROUPS = 8
TOPK_GROUPS = 4
TOP_K = 8
ROUTED_SCALE = 2.5


def _router_kernel(h_ref, wt_ref, bias_ref, idx_ref, wgt_ref):
    tm = h_ref.shape[0]
    per = N_EXPERTS // N_GROUPS
    logits = lax.dot_general(wt_ref[...], h_ref[...], (((1,), (1,)), ((), ())),
                             precision=lax.Precision.HIGHEST, preferred_element_type=F32)
    scores = _sigmoid(logits)
    biased = scores + bias_ref[...]
    neg = -jnp.inf

    grp = biased.reshape(N_GROUPS, per, tm)
    slot = lax.broadcasted_iota(jnp.int32, (N_GROUPS, per, tm), 1)
    top1 = jnp.max(grp, axis=1, keepdims=True)
    first = jnp.min(jnp.where(grp == top1, slot, per), axis=1, keepdims=True)
    top2 = jnp.max(jnp.where(slot == first, neg, grp), axis=1, keepdims=True)
    grp_score = (top1 + top2).reshape(N_GROUPS, tm)

    gid = lax.broadcasted_iota(jnp.int32, (N_GROUPS, tm), 0)
    rank = jnp.zeros((N_GROUPS, tm), jnp.int32)
    for m in range(N_GROUPS):
        s_m = grp_score[m:m + 1, :]
        rank = rank + jnp.where(s_m > grp_score, 1,
                                jnp.where(s_m == grp_score, jnp.where(gid > m, 1, 0), 0))
    keep = (rank < TOPK_GROUPS).reshape(N_GROUPS, 1, tm)
    masked = jnp.where(keep, grp, neg).reshape(N_EXPERTS, tm)

    eid = lax.broadcasted_iota(jnp.int32, (N_EXPERTS, tm), 0)
    idxs, wgts = [], []
    for _ in range(TOP_K):
        best = jnp.max(masked, axis=0, keepdims=True)
        pick = jnp.min(jnp.where(masked == best, eid, N_EXPERTS), axis=0, keepdims=True)
        hit = eid == pick
        idxs.append(pick)
        wgts.append(jnp.sum(jnp.where(hit, scores, 0.0), axis=0, keepdims=True))
        masked = jnp.where(hit, neg, masked)
    wgt = jnp.concatenate(wgts, axis=0)
    wgt = wgt / jnp.sum(wgt, axis=0, keepdims=True) * ROUTED_SCALE
    idx_ref[...] = jnp.concatenate(idxs, axis=0)
    wgt_ref[...] = wgt


def _router(h, w_router_t, bias, *, tm):
    t, d = h.shape
    out = pl.BlockSpec((TOP_K, tm), lambda i: (0, i))
    return pl.pallas_call(
        _router_kernel,
        grid=(t // tm,),
        in_specs=[pl.BlockSpec((tm, d), lambda i: (i, 0)),
                  pl.BlockSpec((N_EXPERTS, d), lambda i: (0, 0)),
                  pl.BlockSpec((N_EXPERTS, 1), lambda i: (0, 0))],
        out_specs=[out, out],
        out_shape=[jax.ShapeDtypeStruct((TOP_K, t), jnp.int32),
                   jax.ShapeDtypeStruct((TOP_K, t), F32)],
        compiler_params=pltpu.CompilerParams(
            dimension_semantics=("parallel",), vmem_limit_bytes=VMEM_LIMIT),
        name="moe_router",
    )(h, w_router_t, bias.reshape(N_EXPERTS, 1))


EXPERT_ROWS = 128


def _silu(x):
    return x * _sigmoid(x)


GATHER_AHEAD = 3
X_SLOTS = 4
Y_SLOTS = 4
SCATTER_LAG = 3
IDX_SLOTS = 8


def _experts_kernel(blk_e_ref, nxt_e_ref, n_used_ref,
                    codes_hbm, h_hbm, wg_hbm, wu_hbm, wd_hbm, ys_hbm,
                    idx_ref, x_buf, y_buf, wg_st, wu_st, wd_st, wg_bf, wu_bf, wd_bf,
                    idx_sem, gat_sem, sct_sem, w_sem, *, n_tokens):
    i = pl.program_id(0)
    n_blk = pl.num_programs(0)
    n_used = n_used_ref[0]
    used = i < n_used
    rows = EXPERT_ROWS
    x_chunks = x_buf.shape[2]
    chunks = y_buf.shape[2]
    assert chunks == x_chunks

    def idx_copy(k):
        slot = k & (IDX_SLOTS - 1)
        return pltpu.make_async_copy(codes_hbm.at[pl.ds(k, 1), :], idx_ref.at[pl.ds(slot, 1), :],
                                     idx_sem.at[slot])

    def start_rows(k, gather):
        window = k & (IDX_SLOTS - 1)
        for j in range(rows):
            if gather:
                buf = k & (X_SLOTS - 1)
                src = idx_ref[window, j]
                pltpu.make_async_copy(h_hbm.at[pl.ds(src, x_chunks), :], x_buf.at[buf, j],
                                      gat_sem.at[buf]).start()
            else:
                buf = k & (Y_SLOTS - 1)
                dst = idx_ref[window, rows + j]
                pltpu.make_async_copy(y_buf.at[buf, j], ys_hbm.at[pl.ds(dst, chunks), :],
                                      sct_sem.at[buf]).start()

    def wait_gather(k):
        buf = k & (X_SLOTS - 1)
        pltpu.make_async_copy(x_buf.at[buf], x_buf.at[buf], gat_sem.at[buf]).wait()

    def wait_scatter(k):
        buf = k & (Y_SLOTS - 1)
        pltpu.make_async_copy(y_buf.at[buf], y_buf.at[buf], sct_sem.at[buf]).wait()

    def weight_copies(e):
        return (pltpu.make_async_copy(wg_hbm.at[e], wg_st, w_sem.at[0]),
                pltpu.make_async_copy(wu_hbm.at[e], wu_st, w_sem.at[1]),
                pltpu.make_async_copy(wd_hbm.at[e], wd_st, w_sem.at[2]))

    def compute(k):
        u = pltpu.einshape("jcl->cjl", x_buf[k & (X_SLOTS - 1)])
        u = jnp.concatenate([u[c] for c in range(x_chunks)], axis=1)
        lo = lax.bitcast_convert_type(u << 16, F32)
        hi = lax.bitcast_convert_type(u & jnp.uint32(0xFFFF0000), F32)
        x = jnp.concatenate([lo, hi], axis=1).astype(BF16)
        hid = _silu(jnp.dot(x, wg_bf[...], preferred_element_type=F32)) * jnp.dot(
            x, wu_bf[...], preferred_element_type=F32)
        y = jnp.dot(hid.astype(BF16), wd_bf[...], preferred_element_type=F32)
        y_buf[k & (Y_SLOTS - 1)] = _pack_words(y)

    @pl.when(i == 0)
    def _():
        for cp in weight_copies(blk_e_ref[0]):
            cp.start(priority=1)
        for k in range(GATHER_AHEAD):
            @pl.when(k < n_used)
            def _():
                idx_copy(k).start()
                idx_copy(k).wait()
                start_rows(k, gather=True)

        @pl.when(GATHER_AHEAD < n_used)
        def _():
            idx_copy(GATHER_AHEAD).start()

    @pl.when(i + GATHER_AHEAD + 1 < n_used)
    def _():
        idx_copy(i + GATHER_AHEAD + 1).start()

    has_ahead = i + GATHER_AHEAD < n_used

    @pl.when(has_ahead)
    def _():
        idx_copy(i + GATHER_AHEAD).wait()

    prev_e = blk_e_ref[jnp.maximum(i - 1, 0)]
    fresh = jnp.logical_or(i == 0, blk_e_ref[i] != prev_e)

    @pl.when(jnp.logical_and(used, fresh))
    def _():
        for cp in weight_copies(blk_e_ref[i]):
            cp.wait()
        wg_bf[...] = wg_st[...].astype(BF16)
        wu_bf[...] = wu_st[...].astype(BF16)
        wd_bf[...] = wd_st[...].astype(BF16)

        @pl.when(nxt_e_ref[i] >= 0)
        def _():
            for cp in weight_copies(nxt_e_ref[i]):
                cp.start(priority=1)

    @pl.when(jnp.logical_and(i >= SCATTER_LAG, i - SCATTER_LAG < n_used))
    def _():
        wait_scatter(i - SCATTER_LAG)

    interior = jnp.logical_and(i >= 1, has_ahead)

    @pl.when(interior)
    def _():
        wait_gather(i)
        start_rows(i - 1, gather=False)
        start_rows(i + GATHER_AHEAD, gather=True)
        compute(i)

    @pl.when(jnp.logical_and(used, jnp.logical_not(interior)))
    def _():
        @pl.when(has_ahead)
        def _():
            start_rows(i + GATHER_AHEAD, gather=True)

        @pl.when(i >= 1)
        def _():
            start_rows(i - 1, gather=False)

        wait_gather(i)
        compute(i)

        @pl.when(i + 1 == n_used)
        def _():
            start_rows(i, gather=False)

    @pl.when(i == n_blk - 1)
    def _():
        for back in range(SCATTER_LAG - 1, -1, -1):
            @pl.when(jnp.logical_and(i >= back, i - back < n_used))
            def _():
                wait_scatter(i - back)

        y_buf[...] = jnp.zeros_like(y_buf)
        for half in range(2):
            for j in range(rows):
                first = (n_tokens * TOP_K + half * rows + j) * chunks
                pltpu.make_async_copy(y_buf.at[half, j], ys_hbm.at[pl.ds(first, chunks), :],
                                      sct_sem.at[half]).start()
            wait_scatter(half)


def _experts(h, codes, blk_e, nxt_e, n_used, w_gate, w_up, w_down):
    d, ff = w_gate.shape[1], w_gate.shape[2]
    x_chunks = d // (2 * LANES)
    t = h.shape[0] // x_chunks
    n_blk = blk_e.shape[0]
    any_spec = pl.BlockSpec(memory_space=pl.ANY)
    grid_spec = pltpu.PrefetchScalarGridSpec(
        num_scalar_prefetch=3,
        grid=(n_blk,),
        in_specs=[any_spec] * 5,
        out_specs=any_spec,
        scratch_shapes=[pltpu.SMEM((IDX_SLOTS, 2 * EXPERT_ROWS), jnp.int32),
                        pltpu.VMEM((X_SLOTS, EXPERT_ROWS, x_chunks, LANES), jnp.uint32),
                        pltpu.VMEM((Y_SLOTS, EXPERT_ROWS, x_chunks, LANES), jnp.uint32),
                        pltpu.VMEM((d, ff), F32), pltpu.VMEM((d, ff), F32), pltpu.VMEM((ff, d), F32),
                        pltpu.VMEM((d, ff), BF16), pltpu.VMEM((d, ff), BF16),
                        pltpu.VMEM((ff, d), BF16),
                        pltpu.SemaphoreType.DMA((IDX_SLOTS,)), pltpu.SemaphoreType.DMA((X_SLOTS,)),
                        pltpu.SemaphoreType.DMA((Y_SLOTS,)), pltpu.SemaphoreType.DMA((3,))])
    return pl.pallas_call(
        functools.partial(_experts_kernel, n_tokens=t),
        grid_spec=grid_spec,
        out_shape=jax.ShapeDtypeStruct(((t * TOP_K + 2 * EXPERT_ROWS) * x_chunks, LANES), jnp.uint32),
        compiler_params=pltpu.CompilerParams(
            dimension_semantics=("arbitrary",), vmem_limit_bytes=VMEM_LIMIT),
        name="routed_experts",
    )(blk_e, nxt_e, n_used, codes, h, w_gate, w_up, w_down)


def _shared_ln_kernel(h_ref, *refs, alpha, eps):
    ys_refs = refs[:TOP_K]
    tw_ref, wg_ref, wu_ref, wd_ref, g_ref, b_ref, o_ref, routed_ref = refs[TOP_K:]
    h = h_ref[...]
    tm, d = h.shape
    chunks = d // (2 * LANES)
    hb = h.astype(BF16)
    hid = _silu(jnp.dot(hb, wg_ref[...], preferred_element_type=F32)) * jnp.dot(
        hb, wu_ref[...], preferred_element_type=F32)
    moe = jnp.dot(hid.astype(BF16), wd_ref[...], preferred_element_type=F32)
    lo = hi = None
    for k in range(TOP_K):
        words = ys_refs[k][...]
        w_k = tw_ref[:, k:k + 1]
        lo_k = lax.bitcast_convert_type(words << 16, F32) * w_k
        hi_k = lax.bitcast_convert_type(words & jnp.uint32(0xFFFF0000), F32) * w_k
        lo = lo_k if lo is None else lo + lo_k
        hi = hi_k if hi is None else hi + hi_k
    routed_ref[0] = lo
    routed_ref[1] = hi
    moe = moe + jnp.concatenate(
        [routed_ref[half, pl.ds(c, tm, stride=chunks), :] for half in range(2)
         for c in range(chunks)], axis=1)
    o_ref[...] = _layer_norm(alpha * h + moe, g_ref[...], b_ref[...], eps)


def _shared_ln(h, ys, top_w, w_gate, w_up, w_down, g, b, *, tm, alpha, eps):
    t, d = h.shape
    ff = w_gate.shape[1]
    chunks = d // (2 * LANES)
    n_tiles = t // tm
    row = pl.BlockSpec((tm, d), lambda i: (i, 0))
    vec = pl.BlockSpec((1, d), lambda i: (0, 0))
    band = lambda k: pl.BlockSpec((tm * chunks, LANES), lambda i: (k * n_tiles + i, 0))
    return pl.pallas_call(
        functools.partial(_shared_ln_kernel, alpha=alpha, eps=eps),
        grid=(n_tiles,),
        in_specs=[row] + [band(k) for k in range(TOP_K)] + [
            pl.BlockSpec((tm * chunks, TOP_K), lambda i: (i, 0)),
            pl.BlockSpec((d, ff), lambda i: (0, 0)), pl.BlockSpec((d, ff), lambda i: (0, 0)),
            pl.BlockSpec((ff, d), lambda i: (0, 0)), vec, vec],
        out_specs=row,
        out_shape=jax.ShapeDtypeStruct((t, d), F32),
        scratch_shapes=[pltpu.VMEM((2, tm * chunks, LANES), F32)],
        compiler_params=pltpu.CompilerParams(
            dimension_semantics=("parallel",), vmem_limit_bytes=VMEM_LIMIT),
        name="shared_expert_layernorm",
    )(h, *([ys] * TOP_K), top_w, w_gate, w_up, w_down, g.reshape(1, d), b.reshape(1, d))


LN_EPS = 1e-5
ROPE_THETA = 10000.0


def _rope_tables(seq, dim):
    inv = 1.0 / (ROPE_THETA ** (jnp.arange(0, dim, 2, dtype=F32) / dim))
    ang = jnp.arange(seq, dtype=F32)[:, None] * inv[None, :]
    ang = jnp.concatenate([ang, ang], axis=-1)
    sign = jnp.where(jnp.arange(dim) < dim // 2, -1.0, 1.0).astype(F32)
    return jnp.cos(ang), jnp.sin(ang) * sign


def _pad_lora(x, d_decay, d_aaa, axis):
    assert d_decay <= LORA_PAD and d_aaa <= LORA_PAD
    decay, aaa, gate = jnp.split(x, [d_decay, d_decay + d_aaa], axis=axis)
    pad = lambda p: jnp.pad(p, [(0, LORA_PAD - p.shape[a]) if a == axis % x.ndim else (0, 0)
                                for a in range(x.ndim)])
    return jnp.concatenate([pad(decay), pad(aaa), gate], axis=axis)


def _dispatch(top_idx, n_tokens, src_stride, dst_stride):
    n_assign = n_tokens * TOP_K
    i32 = jnp.int32
    e_flat = top_idx.reshape(n_assign).astype(i32)
    _, order = lax.sort((e_flat, jnp.arange(n_assign, dtype=i32)), num_keys=1, is_stable=True)
    experts = jnp.arange(N_EXPERTS, dtype=i32)
    counts = jnp.sum((e_flat.reshape(-1, LANES, 1) == experts).astype(i32), axis=(0, 1))
    grp_start = jnp.cumsum(counts) - counts
    padded = (counts + EXPERT_ROWS - 1) // EXPERT_ROWS * EXPERT_ROWS
    pad_end = jnp.cumsum(padded)
    pad_start = pad_end - padded
    n_blk = -(-n_assign // EXPERT_ROWS) + N_EXPERTS
    n_used = (pad_end[-1] // EXPERT_ROWS).astype(i32)
    first_row = jnp.arange(n_blk, dtype=i32) * EXPERT_ROWS
    blk_e = jnp.minimum(jnp.sum((pad_end[None, :] <= first_row[:, None]).astype(i32), axis=1),
                        N_EXPERTS - 1)
    into = first_row - pad_start[blk_e]
    blk_s0 = jnp.clip(grp_start[blk_e] + into, 0, n_assign)
    blk_nv = jnp.clip(counts[blk_e] - into, 0, EXPERT_ROWS)
    later = jnp.where(counts > 0, experts, N_EXPERTS)
    nxt = lax.cummin(jnp.concatenate([later[1:], jnp.full((1,), N_EXPERTS, i32)]), reverse=True)
    nxt_e = jnp.where(nxt < N_EXPERTS, nxt, -1)[blk_e]
    row = jnp.arange(EXPERT_ROWS, dtype=i32)[None, :]
    src = jnp.minimum(blk_s0[:, None] + row, n_assign - 1)
    pad = n_assign + (jnp.arange(n_blk, dtype=i32)[:, None] % 2) * EXPERT_ROWS + row
    valid = row < blk_nv[:, None]
    ids = order[src]
    codes = jnp.concatenate([jnp.where(valid, ids % n_tokens, 0) * src_stride,
                             jnp.where(valid, ids, pad) * dst_stride], axis=1)
    return codes.astype(i32), blk_e, nxt_e.astype(i32), n_used.reshape(1)


def kernel(x, w_in, shift_mu, w_lora_up, w0, a_lora_up, a0, g_lora_up, k_k, k_a, r_k, ln_x_w, ln_x_b,
           w_o_rwkv, w_o_moba, w_out, ln1_g, ln1_b, w_router, router_bias, w_gate_e, w_up_e,
           w_down_e, w_gate_s, w_up_s, w_down_s, ln2_g, ln2_b):
    batch, seq, d = x.shape
    depth = w_in.shape[0]
    alpha = (2 * depth) ** 0.25
    t = batch * seq
    c = d
    shift_w = shift_mu.shape[1]
    moba_heads = d // MOBA_HEAD
    cos, sin_signed = _rope_tables(seq, MOBA_HEAD)

    h = x.reshape(t, d)
    for layer in range(depth):
        wl = w_in[layer].astype(BF16)
        hb = h.astype(BF16)
        d_decay, d_aaa = w_lora_up.shape[1], a_lora_up.shape[1]
        w_lora = _pad_lora(wl[:, 3 * c:shift_w], d_decay, d_aaa, axis=1)
        assert w_lora.shape[1] % LANES == 0
        u_rkv = _matmul(hb, wl[:, :3 * c], tm=1024, tn=1024)
        u_lora = _matmul(hb, w_lora, tm=1024, tn=w_lora.shape[1])
        u_qkv = _matmul(hb, wl[:, shift_w:shift_w + 3 * d], tm=1024, tn=1024)
        u_gate = _matmul(hb, wl[:, shift_w + 3 * d:], tm=1024, tn=1024)

        pad_rows = lambda p: jnp.pad(p, ((0, LORA_PAD - p.shape[0]), (0, 0)))
        o_rwkv = _rwkv_mix(u_rkv, u_lora, shift_mu[layer][:3 * c],
                           _pad_lora(shift_mu[layer][3 * c:], d_decay, d_aaa, axis=0),
                           pad_rows(w_lora_up[layer]), w0[layer], pad_rows(a_lora_up[layer]),
                           a0[layer], g_lora_up[layer], k_k[layer], k_a[layer], r_k[layer],
                           ln_x_w[layer], ln_x_b[layer], batch=batch, seq=seq)
        o_moba = _moba(u_qkv, cos, sin_signed, batch=batch, seq=seq, heads=moba_heads)

        mixed = _merge(o_rwkv, o_moba, w_o_rwkv[layer].astype(BF16), w_o_moba[layer].astype(BF16),
                       u_gate, tm=512, tn=1024)
        h, h_packed = _proj_ln(mixed, w_out[layer].astype(BF16), h, ln1_g[layer], ln1_b[layer],
                               tm=512, alpha=alpha, eps=LN_EPS)

        top_idx, top_w = _router(h, w_router[layer].T, router_bias[layer], tm=512)
        chunks = d // LANES
        codes, blk_e, nxt_e, n_used = _dispatch(top_idx, t, chunks // 2, chunks // 2)
        ys = _experts(h_packed, codes, blk_e, nxt_e, n_used,
                      w_gate_e[layer], w_up_e[layer], w_down_e[layer])
        h = _shared_ln(h, ys, jnp.repeat(top_w.T, chunks // 2, axis=0), w_gate_s[layer].astype(BF16),
                       w_up_s[layer].astype(BF16), w_down_s[layer].astype(BF16), ln2_g[layer],
                       ln2_b[layer], tm=256, alpha=alpha, eps=LN_EPS)
    return h.reshape(batch, seq, d)
```
